```python
import math
import jax
import jax.numpy as jnp
from jax import lax
import numpy as np

D_MODEL = 1024
BATCH = 8
SEQ = 2048
DEPTH = 2
DEC_BATCH = 32
DEC_SEQ = 4
PAST_LEN = 16384
PAGE_SIZE = 128

N_MIXERS = 2
N_MLSTM_LAYERS = (DEPTH + 1) // 2
N_NSA_LAYERS = DEPTH // 2
NORM_EPS = 1e-6

M_HEADS = 8
M_DK = D_MODEL // 16
M_DV = D_MODEL // M_HEADS
M_CHUNK = 64
M_PROJ = 2 * M_HEADS * M_DK + 2 * M_HEADS * M_DV + 2 * M_HEADS

A_HEADS = 16
A_KV = 4
A_REP = A_HEADS // A_KV
A_DH = D_MODEL // A_HEADS
ROT_DIM = A_DH // 4
ROPE_THETA = 500000.0
CMP_STRIDE = 16
CMP_LEN = 2 * CMP_STRIDE
CMP_HID = A_DH
SEL_BLK = 64
N_SEL = 16
WINDOW = 512
SEL_QBLK = 64
WIN_QBLK = 128
A_QW = A_HEADS * A_DH
A_KVW = 3 * 2 * A_KV * A_DH
A_PROJ = A_QW + A_KVW + 3 * A_HEADS
ATTN_SCALE = A_DH ** -0.5

P_HEADS = 8
P_NKEYS = 128
P_EXPERTS = P_NKEYS * P_NKEYS
P_QDIM = 256
P_TOPK = 16
P_TOK_BLK = 256

kernel_name = 'hybrid_mlstm_nsa_peer_step'


def rmsnorm(x, g):
    xf = x.astype(jnp.float32)
    xf = xf * lax.rsqrt(jnp.mean(xf * xf, axis=-1, keepdims=True) + NORM_EPS)
    return (xf * g.astype(jnp.float32)).astype(x.dtype)


def masked_softmax(s, valid, axis=-1):
    s = jnp.where(valid, s.astype(jnp.float32), -jnp.inf)
    m = jnp.max(s, axis=axis, keepdims=True)
    m = jnp.where(jnp.isfinite(m), m, 0.0)
    e = jnp.exp(s - m)
    return e / jnp.maximum(jnp.sum(e, axis=axis, keepdims=True), 1e-30)


def partial_rotary(x, pos):
    half = ROT_DIM // 2
    inv_freq = ROPE_THETA ** (-(jnp.arange(half, dtype=jnp.float32) * (2.0 / ROT_DIM)))
    ang = pos.astype(jnp.float32)[:, None] * inv_freq[None, :]
    shape = (1, x.shape[1]) + (1,) * (x.ndim - 3) + (half,)
    cos = jnp.cos(ang).reshape(shape)
    sin = jnp.sin(ang).reshape(shape)
    xf = x.astype(jnp.float32)
    x1 = xf[..., :half]
    x2 = xf[..., half:ROT_DIM]
    out = jnp.concatenate([x1 * cos - x2 * sin, x2 * cos + x1 * sin, xf[..., ROT_DIM:]], axis=-1)
    return out.astype(x.dtype)


def mlstm_mixer(h, c0, n0, m0, w_in, b_i, b_f, head_norm, w_out):
    bsz, t_len, _ = h.shape
    proj = h @ w_in
    o1 = M_HEADS * M_DK
    o2 = 2 * o1
    o3 = o2 + M_HEADS * M_DV
    o4 = o3 + M_HEADS * M_DV
    o5 = o4 + M_HEADS
    q = proj[..., :o1].reshape(bsz, t_len, M_HEADS, M_DK).astype(jnp.float32)
    k = proj[..., o1:o2].reshape(bsz, t_len, M_HEADS, M_DK).astype(jnp.float32) * (M_DK ** -0.5)
    v = proj[..., o2:o3].reshape(bsz, t_len, M_HEADS, M_DV).astype(jnp.float32)
    o_gate = jax.nn.sigmoid(proj[..., o3:o4].astype(jnp.float32))
    log_i = proj[..., o4:o5].astype(jnp.float32) + b_i.astype(jnp.float32)
    log_f = jax.nn.log_sigmoid(proj[..., o5:].astype(jnp.float32) + b_f.astype(jnp.float32))
    chunk = math.gcd(t_len, M_CHUNK)
    n_chunks = t_len // chunk

    def to_chunks(a):
        return jnp.moveaxis(a.reshape((bsz, n_chunks, chunk) + a.shape[2:]), 1, 0)

    causal = jnp.tril(jnp.ones((chunk, chunk), dtype=bool))

    def chunk_step(carry, xs):
        c_prev, n_prev, m_prev = carry
        qc, kc, vc, lic, lfc = xs
        b = jnp.cumsum(lfc, axis=1)
        a = b + m_prev[:, None, :]
        d = b[:, :, None, :] - b[:, None, :, :] + lic[:, None, :, :]
        d = jnp.where(causal[None, :, :, None], d, -jnp.inf)
        m_t = jnp.maximum(a, jnp.max(d, axis=2))
        w_d = jnp.exp(d - m_t[:, :, None, :])
        w_a = jnp.exp(a - m_t)
        s = jnp.einsum('bthd,bshd->btsh', qc, kc) * w_d
        num = w_a[..., None] * jnp.einsum('bthd,bhde->bthe', qc, c_prev) + jnp.einsum('btsh,bshe->bthe', s, vc)
        den = w_a * jnp.einsum('bthd,bhd->bth', qc, n_prev) + jnp.sum(s, axis=2)
        h_out = num / jnp.maximum(jnp.abs(den), jnp.exp(-m_t))[..., None]
        m_new = m_t[:, -1]
        w_last = jnp.exp(d[:, -1] - m_new[:, None, :])
        decay = jnp.exp(a[:, -1] - m_new)
        c_new = decay[..., None, None] * c_prev + jnp.einsum('bsh,bshd,bshe->bhde', w_last, kc, vc)
        n_new = decay[..., None] * n_prev + jnp.einsum('bsh,bshd->bhd', w_last, kc)
        return (c_new, n_new, m_new), h_out

    init = (c0.astype(jnp.float32), n0.astype(jnp.float32), m0.astype(jnp.float32))
    (c_t, n_t, m_t), h_chunks = lax.scan(
        chunk_step, init,
        (to_chunks(q), to_chunks(k), to_chunks(v), to_chunks(log_i), to_chunks(log_f)))
    hid = jnp.moveaxis(h_chunks, 0, 1).reshape(bsz, t_len, M_HEADS, M_DV)
    hid = hid * lax.rsqrt(jnp.mean(hid * hid, axis=-1, keepdims=True) + NORM_EPS)
    hid = hid.reshape(bsz, t_len, M_HEADS * M_DV) * head_norm.astype(jnp.float32)
    y = (o_gate * hid).astype(h.dtype) @ w_out
    return y, c_t, n_t, m_t


def nsa_project(h, pos, w_in):
    bsz, t_len, _ = h.shape
    proj = h @ w_in
    q = partial_rotary(proj[..., :A_QW].reshape(bsz, t_len, A_HEADS, A_DH), pos)
    kv = proj[..., A_QW:A_QW + A_KVW].reshape(bsz, t_len, 3, 2, A_KV, A_DH)
    k_rot = partial_rotary(kv[:, :, :, 0], pos)
    kv = jnp.stack([k_rot, kv[:, :, :, 1]], axis=3)
    gate_pre = proj[..., A_QW + A_KVW:].reshape(bsz, t_len, A_HEADS, 3)
    return q, kv[:, :, 0], kv[:, :, 1], kv[:, :, 2], gate_pre


def pad_to_blocks(kv):
    t_k = kv.shape[1]
    t_p = -(-t_k // SEL_BLK) * SEL_BLK
    return jnp.pad(kv, ((0, 0), (0, t_p - t_k), (0, 0), (0, 0), (0, 0)))


def compressed_branch(q, pos, kv, w1, b1, w2):
    bsz, t_q = q.shape[:2]
    kv = pad_to_blocks(kv)
    sub = kv.reshape(bsz, kv.shape[1] // CMP_STRIDE, CMP_STRIDE, 2, A_KV, A_DH)
    first = jnp.einsum('bnlcgd,cldh->bncgh', sub, w1[:, :CMP_STRIDE])
    second = jnp.einsum('bnlcgd,cldh->bncgh', sub, w1[:, CMP_STRIDE:])
    hid = jax.nn.gelu((first[:, :-1] + second[:, 1:]).astype(jnp.float32) + b1.astype(jnp.float32)[:, None, :], approximate=False)
    ckv = jnp.einsum('bncgh,chd->bncgd', hid.astype(kv.dtype), w2)
    n_cmp = ckv.shape[1]
    block_end = jnp.arange(n_cmp) * CMP_STRIDE + CMP_LEN - 1
    valid = block_end[None, :] <= pos[:, None]
    qg = q.reshape(bsz, t_q, A_KV, A_REP, A_DH)
    s = jnp.einsum('bqgrd,bngd->bgrqn', qg, ckv[:, :, 0]).astype(jnp.float32) * ATTN_SCALE
    p = masked_softmax(s, valid)
    o = jnp.einsum('bgrqn,bngd->bqgrd', p.astype(ckv.dtype), ckv[:, :, 1])
    return o.reshape(bsz, t_q, A_HEADS, A_DH), p


def selected_branch(q, pos, kv, p_cmp):
    bsz, t_q = q.shape[:2]
    kv = pad_to_blocks(kv)
    n_blk = kv.shape[1] // SEL_BLK
    n_cmp = p_cmp.shape[-1]
    c_start = jnp.arange(n_cmp) * CMP_STRIDE
    s_start = jnp.arange(n_blk) * SEL_BLK
    overlap = jnp.clip(jnp.minimum(c_start[:, None] + CMP_LEN, s_start[None, :] + SEL_BLK)
                       - jnp.maximum(c_start[:, None], s_start[None, :]), 0, None).astype(jnp.float32) / CMP_STRIDE
    imp = jnp.einsum('bgrqn,nj->bgqj', p_cmp, overlap)
    cur = pos // SEL_BLK
    j = jnp.arange(n_blk)
    forced = (j[None, :] == 0) | (j[None, :] == cur[:, None]) | (j[None, :] == cur[:, None] - 1)
    score = jnp.where(forced, jnp.inf, jnp.where(j[None, :] <= cur[:, None], imp, -jnp.inf))
    k_sel = min(N_SEL, n_blk)
    _, idx = lax.top_k(score, k_sel)
    blocks = kv.reshape(bsz, n_blk, SEL_BLK, 2, A_KV, A_DH).transpose(0, 4, 1, 2, 3, 5)
    qc_len = math.gcd(t_q, SEL_QBLK)
    n_q = t_q // qc_len
    q_all = jnp.moveaxis(q.reshape(bsz, n_q, qc_len, A_KV, A_REP, A_DH), 1, 0)
    idx_all = jnp.moveaxis(idx.reshape(bsz, A_KV, n_q, qc_len, k_sel), 2, 0)
    pos_all = pos.reshape(n_q, qc_len)
    b_ix = jnp.arange(bsz)[:, None, None, None]
    g_ix = jnp.arange(A_KV)[None, :, None, None]
    offs = jnp.arange(SEL_BLK)

    def one_block(args):
        qc, ic, pc = args
        kvg = blocks[b_ix, g_ix, ic]
        s = jnp.einsum('bqgrd,bgqksd->bgrqks', qc, kvg[..., 0, :]).astype(jnp.float32) * ATTN_SCALE
        kpos = ic[..., None] * SEL_BLK + offs
        valid = (kpos <= pc[None, None, :, None, None])[:, :, None]
        p = masked_softmax(s, valid, axis=(-2, -1))
        return jnp.einsum('bgrqks,bgqksd->bqgrd', p.astype(kvg.dtype), kvg[..., 1, :])

    o = lax.map(one_block, (q_all, idx_all, pos_all))
    return jnp.moveaxis(o, 0, 1).reshape(bsz, t_q, A_HEADS, A_DH)


def window_branch(q, pos, kv_ext):
    bsz, t_q = q.shape[:2]
    qb_len = math.gcd(t_q, WIN_QBLK)
    n_qb = t_q // qb_len
    band = WINDOW + qb_len
    q_all = jnp.moveaxis(q.reshape(bsz, n_qb, qb_len, A_KV, A_REP, A_DH), 1, 0)
    starts = jnp.arange(n_qb) * qb_len
    pos_all = pos.reshape(n_qb, qb_len)
    base = pos[0] - WINDOW

    def one_block(args):
        qc, st, qp = args
        kb = lax.dynamic_slice_in_dim(kv_ext, st, band, axis=1)
        kpos = base + st + jnp.arange(band)
        diff = qp[:, None] - kpos[None, :]
        valid = (diff >= 0) & (diff <= WINDOW) & (kpos >= 0)[None, :]
        s = jnp.einsum('bqgrd,bkgd->bgrqk', qc, kb[:, :, 0]).astype(jnp.float32) * ATTN_SCALE
        p = masked_softmax(s, valid)
        return jnp.einsum('bgrqk,bkgd->bqgrd', p.astype(kb.dtype), kb[:, :, 1])

    o = lax.map(one_block, (q_all, starts, pos_all))
    return jnp.moveaxis(o, 0, 1).reshape(bsz, t_q, A_HEADS, A_DH)


def nsa_mixer(h, pos, past_cmp, past_sel, past_win, w_in, b_gate, cmp_w1, cmp_b1, cmp_w2, w_out):
    bsz, t_len, _ = h.shape
    q, cmp_new, sel_new, win_new, gate_pre = nsa_project(h, pos, w_in)
    if past_cmp is None:
        cmp_all, sel_all = cmp_new, sel_new
        rows = min(WINDOW, t_len)
        win_ext = jnp.pad(win_new, ((0, 0), (WINDOW, 0), (0, 0), (0, 0), (0, 0)))
    else:
        cmp_all = jnp.concatenate([past_cmp, cmp_new], axis=1)
        sel_all = jnp.concatenate([past_sel, sel_new], axis=1)
        rows = past_win.shape[1]
        win_ext = jnp.concatenate(
            [jnp.pad(past_win, ((0, 0), (WINDOW - rows, 0), (0, 0), (0, 0), (0, 0))), win_new], axis=1)
    win_buf = win_ext[:, win_ext.shape[1] - rows:]
    o_cmp, p_cmp = compressed_branch(q, pos, cmp_all, cmp_w1, cmp_b1, cmp_w2)
    o_sel = selected_branch(q, pos, sel_all, p_cmp)
    o_win = window_branch(q, pos, win_ext)
    g = jax.nn.sigmoid((gate_pre + b_gate).astype(jnp.float32))
    o = g[..., 0:1] * o_cmp + g[..., 1:2] * o_sel + g[..., 2:3] * o_win
    y = o.reshape(bsz, t_len, A_QW).astype(h.dtype) @ w_out
    return y, cmp_new, sel_new, win_buf


def peer_ffn(h, w_q, sub_keys, u_tab, v_tab):
    bsz, t_len, d = h.shape
    n_tok = bsz * t_len
    blk = math.gcd(n_tok, P_TOK_BLK)
    half = P_QDIM // 2

    def one_block(x):
        q = (x @ w_q).reshape(blk, P_HEADS, P_QDIM)
        s1 = jnp.einsum('nhd,hkd->nhk', q[..., :half], sub_keys[:, 0]).astype(jnp.float32)
        s2 = jnp.einsum('nhd,hkd->nhk', q[..., half:], sub_keys[:, 1]).astype(jnp.float32)
        v1, i1 = lax.top_k(s1, P_TOPK)
        v2, i2 = lax.top_k(s2, P_TOPK)
        cand = (v1[..., :, None] + v2[..., None, :]).reshape(blk, P_HEADS, P_TOPK * P_TOPK)
        vals, ci = lax.top_k(cand, P_TOPK)
        expert = (jnp.take_along_axis(i1, ci // P_TOPK, axis=-1) * P_NKEYS
                  + jnp.take_along_axis(i2, ci % P_TOPK, axis=-1))
        gate = jax.nn.softmax(vals, axis=-1)
        u = u_tab[expert]
        v = v_tab[expert]
        act = jax.nn.gelu(jnp.einsum('nd,nhkd->nhk', x, u).astype(jnp.float32), approximate=False)
        return jnp.einsum('nhk,nhkd->nd', (gate * act).astype(v.dtype), v)

    out = lax.map(one_block, h.reshape(n_tok // blk, blk, d))
    return out.reshape(bsz, t_len, d)


def setup_inputs(seed: int = 0) -> dict:
    key = jax.random.key(seed)
    ks = jax.random.split(key, 32)
    n_pages = PAST_LEN // PAGE_SIZE
    n_used = DEC_BATCH * n_pages
    pool = n_used + n_used // 4
    win_rows = min(WINDOW, PAST_LEN)

    def nrm(k, shape, scale):
        return scale * jax.random.normal(k, shape, jnp.float32)

    page_table = jax.random.permutation(ks[0], pool)[:n_used].reshape(DEC_BATCH, n_pages).astype(jnp.int32)
    return {
        'x_prompt': nrm(ks[1], (BATCH, SEQ, D_MODEL), 1.0),
        'x_sample': nrm(ks[2], (DEC_BATCH, DEC_SEQ, D_MODEL), 1.0),
        'cache_cmp_kv': nrm(ks[3], (N_NSA_LAYERS, pool, PAGE_SIZE, 2, A_KV, A_DH), 1.0),
        'cache_sel_kv': nrm(ks[4], (N_NSA_LAYERS, pool, PAGE_SIZE, 2, A_KV, A_DH), 1.0),
        'state_win_kv': nrm(ks[5], (N_NSA_LAYERS, DEC_BATCH, win_rows, 2, A_KV, A_DH), 1.0),
        'state_C': nrm(ks[6], (N_MLSTM_LAYERS, DEC_BATCH, M_HEADS, M_DK, M_DV), 0.5),
        'state_n': nrm(ks[7], (N_MLSTM_LAYERS, DEC_BATCH, M_HEADS, M_DK), 0.5),
        'state_m': nrm(ks[8], (N_MLSTM_LAYERS, DEC_BATCH, M_HEADS), 1.0),
        'page_table': page_table,
        'norm_mix': 1.0 + nrm(ks[9], (DEPTH, D_MODEL), 0.05),
        'norm_ffn': 1.0 + nrm(ks[10], (DEPTH, D_MODEL), 0.05),
        'norm_final': 1.0 + nrm(ks[11], (D_MODEL,), 0.05),
        'mlstm_w_in': nrm(ks[12], (N_MLSTM_LAYERS, D_MODEL, M_PROJ), D_MODEL ** -0.5),
        'mlstm_b_i': nrm(ks[13], (N_MLSTM_LAYERS, M_HEADS), 0.1),
        'mlstm_b_f': 3.0 + nrm(ks[14], (N_MLSTM_LAYERS, M_HEADS), 0.5),
        'mlstm_head_norm': 1.0 + nrm(ks[15], (N_MLSTM_LAYERS, M_HEADS * M_DV), 0.05),
        'mlstm_w_out': nrm(ks[16], (N_MLSTM_LAYERS, M_HEADS * M_DV, D_MODEL), (M_HEADS * M_DV) ** -0.5),
        'nsa_w_in': nrm(ks[17], (N_NSA_LAYERS, D_MODEL, A_PROJ), D_MODEL ** -0.5),
        'nsa_b_gate': nrm(ks[18], (N_NSA_LAYERS, A_HEADS, 3), 0.1),
        'nsa_cmp_w1': nrm(ks[19], (N_NSA_LAYERS, 2, CMP_LEN, A_DH, CMP_HID), (CMP_LEN * A_DH) ** -0.5),
        'nsa_cmp_b1': nrm(ks[20], (N_NSA_LAYERS, 2, CMP_HID), 0.1),
        'nsa_cmp_w2': nrm(ks[21], (N_NSA_LAYERS, 2, CMP_HID, A_DH), CMP_HID ** -0.5),
        'nsa_w_out': nrm(ks[22], (N_NSA_LAYERS, A_QW, D_MODEL), A_QW ** -0.5),
        'peer_w_q': nrm(ks[23], (DEPTH, D_MODEL, P_HEADS * P_QDIM), D_MODEL ** -0.5),
        'peer_sub_keys': nrm(ks[24], (DEPTH, P_HEADS, 2, P_NKEYS, P_QDIM // 2), (P_QDIM // 2) ** -0.5),
        'peer_u': nrm(ks[25], (DEPTH, P_EXPERTS, D_MODEL), D_MODEL ** -0.5),
        'peer_v': nrm(ks[26], (DEPTH, P_EXPERTS, D_MODEL), P_HEADS ** -0.5),
    }


def reference(x_prompt, x_sample, cache_cmp_kv, cache_sel_kv, state_win_kv, state_C, state_n, state_m,
              page_table, norm_mix, norm_ffn, norm_final, mlstm_w_in, mlstm_b_i, mlstm_b_f,
              mlstm_head_norm, mlstm_w_out, nsa_w_in, nsa_b_gate, nsa_cmp_w1, nsa_cmp_b1, nsa_cmp_w2,
              nsa_w_out, peer_w_q, peer_sub_keys, peer_u, peer_v):
    bsz, t_len, _ = x_prompt.shape
    dbsz, t_s, _ = x_sample.shape
    past_len = page_table.shape[1] * PAGE_SIZE
    pos_p = jnp.arange(t_len, dtype=jnp.int32)
    pos_s = past_len + jnp.arange(t_s, dtype=jnp.int32)
    xp, xs = x_prompt, x_sample
    cmp_p, sel_p, win_p, c_p, n_p, m_p = [], [], [], [], [], []
    cmp_s, sel_s, win_s, c_s, n_s, m_s = [], [], [], [], [], []
    for layer in range(DEPTH):
        hp = rmsnorm(xp, norm_mix[layer])
        hs = rmsnorm(xs, norm_mix[layer])
        if layer % N_MIXERS == 0:
            a = layer // N_MIXERS
            zc = jnp.zeros((bsz, M_HEADS, M_DK, M_DV), jnp.float32)
            zn = jnp.zeros((bsz, M_HEADS, M_DK), jnp.float32)
            zm = jnp.zeros((bsz, M_HEADS), jnp.float32)
            yp, cpt, npt, mpt = mlstm_mixer(hp, zc, zn, zm, mlstm_w_in[a], mlstm_b_i[a], mlstm_b_f[a],
                                            mlstm_head_norm[a], mlstm_w_out[a])
            ys, cst, nst, mst = mlstm_mixer(hs, state_C[a], state_n[a], state_m[a], mlstm_w_in[a],
                                            mlstm_b_i[a], mlstm_b_f[a], mlstm_head_norm[a], mlstm_w_out[a])
            c_p.append(cpt.astype(state_C.dtype))
            n_p.append(npt.astype(state_n.dtype))
            m_p.append(mpt.astype(state_m.dtype))
            c_s.append(cst.astype(state_C.dtype))
            n_s.append(nst.astype(state_n.dtype))
            m_s.append(mst.astype(state_m.dtype))
        else:
            l = layer // N_MIXERS
            yp, ckp, skp, wkp = nsa_mixer(hp, pos_p, None, None, None, nsa_w_in[l], nsa_b_gate[l],
                                          nsa_cmp_w1[l], nsa_cmp_b1[l], nsa_cmp_w2[l], nsa_w_out[l])
            past_cmp = cache_cmp_kv[l, page_table].reshape(dbsz, past_len, 2, A_KV, A_DH)
            past_sel = cache_sel_kv[l, page_table].reshape(dbsz, past_len, 2, A_KV, A_DH)
            ys, cks, sks, wks = nsa_mixer(hs, pos_s, past_cmp, past_sel, state_win_kv[l], nsa_w_in[l],
                                          nsa_b_gate[l], nsa_cmp_w1[l], nsa_cmp_b1[l], nsa_cmp_w2[l],
                                          nsa_w_out[l])
            cmp_p.append(ckp)
            sel_p.append(skp)
            win_p.append(wkp)
            cmp_s.append(cks)
            sel_s.append(sks)
            win_s.append(wks)
        xp = xp + yp
        xs = xs + ys
        xp = xp + peer_ffn(rmsnorm(xp, norm_ffn[layer]), peer_w_q[layer], peer_sub_keys[layer],
                           peer_u[layer], peer_v[layer])
        xs = xs + peer_ffn(rmsnorm(xs, norm_ffn[layer]), peer_w_q[layer], peer_sub_keys[layer],
                           peer_u[layer], peer_v[layer])
    y_prompt = rmsnorm(xp, norm_final)
    y_sample = rmsnorm(xs, norm_final)
    return (y_prompt, y_sample,
            jnp.stack(cmp_p), jnp.stack(sel_p), jnp.stack(win_p),
            jnp.stack(c_p), jnp.stack(n_p), jnp.stack(m_p),
            jnp.stack(cmp_s), jnp.stack(sel_s), jnp.stack(win_s),
            jnp.stack(c_s), jnp.stack(n_s), jnp.stack(m_s))
```

```python
import functools
import math

import numpy as np
import jax
import jax.numpy as jnp
from jax import lax
from jax.experimental import pallas as pl
from jax.experimental.pallas import tpu as pltpu

F32 = jnp.float32
BF16 = jnp.bfloat16
HI = lax.Precision.HIGHEST

V7X_VMEM_BYTES = 64 * 1024 * 1024
VMEM_LIMIT = V7X_VMEM_BYTES - 8 * 1024 * 1024
LANE = 128
SUBLANE = 8

D_MODEL = 1024
NORM_EPS = 1e-6

M_HEADS = 8
M_DK = 64
M_DV = 128
M_CHUNK = 64
M_PROJ = 2 * M_HEADS * M_DK + 2 * M_HEADS * M_DV + 2 * M_HEADS
M_PROJ_PAD = 3200

A_HEADS = 16
A_KV = 4
A_REP = 4
A_DH = 64
ROT_DIM = 16
ROPE_THETA = 500000.0
CMP_STRIDE = 16
CMP_LEN = 32
SEL_BLK = 64
SEL_BLK_LOG2 = 6
N_SEL = 16
WINDOW = 512
A_QW = 1024
A_KVW = 1536
A_PROJ = A_QW + A_KVW + 3 * A_HEADS
A_PROJ_PAD = 2688
ATTN_SCALE = A_DH ** -0.5
GRP_W = A_KV * A_DH
ROW_W = 2 * GRP_W
PAGE_SIZE = 128

P_HEADS = 8
P_NKEYS = 128
P_EXPERTS = P_NKEYS * P_NKEYS
P_QDIM = 256
P_TOPK = 16

NEG_INF = float("-inf")


def _params(n_axes):
    return pltpu.CompilerParams(dimension_semantics=("arbitrary",) * n_axes,
                                vmem_limit_bytes=VMEM_LIMIT)


def _nn(a, b, precision=None):
    return jnp.dot(a, b, preferred_element_type=F32, precision=precision)


def _nt(a, b, precision=None):
    return lax.dot_general(a, b, (((1,), (1,)), ((), ())), preferred_element_type=F32,
                           precision=precision)


def _tn(a, b, precision=None):
    return lax.dot_general(a, b, (((0,), (0,)), ((), ())), preferred_element_type=F32,
                           precision=precision)


def _gelu(x):
    return 0.5 * x * (1.0 + lax.erf(x * (2.0 ** -0.5)))


def _rms(x, g):
    return x * lax.rsqrt(jnp.mean(x * x, axis=-1, keepdims=True) + NORM_EPS) * g


def _masked_softmax(s, valid):
    s = jnp.where(valid, s, NEG_INF)
    m = jnp.max(s, axis=-1, keepdims=True)
    m = jnp.where(m == NEG_INF, 0.0, m)
    e = jnp.exp(s - m)
    return e / jnp.maximum(jnp.sum(e, axis=-1, keepdims=True), 1e-30)


def _iota(shape, dim):
    return lax.broadcasted_iota(jnp.int32, shape, dim)


def _rms_mm_body(x_ref, g_ref, w_ref, o_ref):
    o_ref[...] = _nn(_rms(x_ref[...], g_ref[...]).astype(BF16), w_ref[...])


def rms_matmul(x, g, w, tm):
    n, d = x.shape
    nout = w.shape[1]
    return pl.pallas_call(
        _rms_mm_body,
        out_shape=jax.ShapeDtypeStruct((n, nout), F32),
        grid=(n // tm,),
        in_specs=[pl.BlockSpec((tm, d), lambda i: (i, 0)),
                  pl.BlockSpec((1, d), lambda i: (0, 0)),
                  pl.BlockSpec((d, nout), lambda i: (0, 0))],
        out_specs=pl.BlockSpec((tm, nout), lambda i: (i, 0)),
        compiler_params=_params(1), name="rms_matmul",
    )(x, g.reshape(1, d), w)


def _mm_res_body(a_ref, w_ref, x_ref, o_ref):
    o_ref[...] = x_ref[...] + _nn(a_ref[...].astype(BF16), w_ref[...])


def matmul_residual(a, w, x, tm):
    n, k = a.shape
    d = w.shape[1]
    return pl.pallas_call(
        _mm_res_body,
        out_shape=jax.ShapeDtypeStruct((n, d), F32),
        grid=(n // tm,),
        in_specs=[pl.BlockSpec((tm, k), lambda i: (i, 0)),
                  pl.BlockSpec((k, d), lambda i: (0, 0)),
                  pl.BlockSpec((tm, d), lambda i: (i, 0))],
        out_specs=pl.BlockSpec((tm, d), lambda i: (i, 0)),
        compiler_params=_params(1), name="matmul_residual",
    )(a, w, x)


def _final_norm_body(x_ref, g_ref, o_ref):
    o_ref[...] = _rms(x_ref[...], g_ref[...])


def final_norm(x, g, tm):
    n, d = x.shape
    return pl.pallas_call(
        _final_norm_body,
        out_shape=jax.ShapeDtypeStruct((n, d), F32),
        grid=(n // tm,),
        in_specs=[pl.BlockSpec((tm, d), lambda i: (i, 0)),
                  pl.BlockSpec((1, d), lambda i: (0, 0))],
        out_specs=pl.BlockSpec((tm, d), lambda i: (i, 0)),
        compiler_params=_params(1), name="final_norm",
    )(x, g.reshape(1, d))


def _nsa_proj_body(x_ref, g_ref, w_ref, c_ref, sa_ref, sb_ref,
                   q_ref, cmp_ref, sel_ref, win_ref, gate_ref):
    y = _nn(_rms(x_ref[...], g_ref[...]).astype(BF16), w_ref[...])
    c = c_ref[...]
    sa = sa_ref[...]
    sb = sb_ref[...]

    def rot(z):
        return (z * c + pltpu.roll(z, GRP_W - ROT_DIM // 2, 1) * sa
                + pltpu.roll(z, ROT_DIM // 2, 1) * sb)

    for j in range(A_QW // GRP_W):
        q_ref[:, j * GRP_W:(j + 1) * GRP_W] = rot(y[:, j * GRP_W:(j + 1) * GRP_W])
    for ref, base in ((cmp_ref, A_QW), (sel_ref, A_QW + ROW_W), (win_ref, A_QW + 2 * ROW_W)):
        ref[:, 0:GRP_W] = rot(y[:, base:base + GRP_W])
        ref[:, GRP_W:ROW_W] = y[:, base + GRP_W:base + ROW_W]
    gate_ref[...] = y[:, A_QW + A_KVW:A_PROJ_PAD]


def nsa_project(x, g, w, tabs, tm):
    n, d = x.shape
    n_tab = tabs[0].shape[0] // tm
    tab_spec = pl.BlockSpec((tm, GRP_W), lambda i: (i % n_tab, 0))
    row = lambda w_: pl.BlockSpec((tm, w_), lambda i: (i, 0))
    return pl.pallas_call(
        _nsa_proj_body,
        out_shape=(jax.ShapeDtypeStruct((n, A_QW), F32),
                   jax.ShapeDtypeStruct((n, ROW_W), F32),
                   jax.ShapeDtypeStruct((n, ROW_W), F32),
                   jax.ShapeDtypeStruct((n, ROW_W), F32),
                   jax.ShapeDtypeStruct((n, LANE), F32)),
        grid=(n // tm,),
        in_specs=[row(d), pl.BlockSpec((1, d), lambda i: (0, 0)),
                  pl.BlockSpec((d, A_PROJ_PAD), lambda i: (0, 0)),
                  tab_spec, tab_spec, tab_spec],
        out_specs=(row(A_QW), row(ROW_W), row(ROW_W), row(ROW_W), row(LANE)),
        compiler_params=_params(1), name="nsa_project",
    )(x, g.reshape(1, d), w, *tabs)


def rope_tables(pos):
    half = ROT_DIM // 2
    inv_freq = ROPE_THETA ** (-(jnp.arange(half, dtype=F32) * (2.0 / ROT_DIM)))
    ang = pos.astype(F32)[:, None] * inv_freq[None, :]
    cos, sin = jnp.cos(ang), jnp.sin(ang)
    t = pos.shape[0]
    one = jnp.ones((t, A_DH - ROT_DIM), F32)
    zero = jnp.zeros((t, A_DH - ROT_DIM), F32)
    z8 = jnp.zeros((t, half), F32)
    c = jnp.concatenate([cos, cos, one], axis=1)
    sa = jnp.concatenate([-sin, z8, zero], axis=1)
    sb = jnp.concatenate([z8, sin, zero], axis=1)
    return tuple(jnp.tile(a, (1, A_KV)) for a in (c, sa, sb))


def _mlstm_body(q_ref, k_ref, v_ref, og_ref, gt_ref, c0_ref, n0_ref, m0_ref, gb_ref, hn_ref,
                hid_ref, c_out, n_out, m_out, c_s, n_s, m_s, *, chunk, t_last):
    ci = pl.program_id(1)

    @pl.when(ci == 0)
    def _():
        c_s[...] = c0_ref[0]
        n_s[...] = n0_ref[0]
        m_s[...] = m0_ref[0]

    L = chunk
    gates = gt_ref[0] + gb_ref[...]
    log_f = jnp.minimum(gates, 0.0) - jnp.log1p(jnp.exp(-jnp.abs(gates)))
    row = _iota((L, L), 0)
    col = _iota((L, L), 1)
    causal = row >= col
    tril = causal.astype(F32)
    b_all = _nn(tril, log_f, HI)
    g_mat = gates - pltpu.roll(b_all, LANE - M_HEADS, 1)
    eye8 = (_iota((SUBLANE, LANE), 0) == _iota((SUBLANE, LANE), 1)).astype(F32)
    g_rows = _nt(eye8, g_mat, HI)
    rcol = _iota((L, 1), 0)
    lane1 = _iota((1, LANE), 1)
    m_prev_row = m_s[...]
    m_row = m_prev_row

    for h in range(M_HEADS):
        b_col = b_all[:, M_HEADS + h:M_HEADS + h + 1]
        m_prev = m_prev_row[:, h:h + 1]
        a_col = b_col + m_prev
        dm = jnp.where(causal, b_col + g_rows[h:h + 1, :], NEG_INF)
        m_t = jnp.maximum(a_col, jnp.max(dm, axis=1, keepdims=True))
        w_d = jnp.exp(dm - m_t)
        w_a = jnp.exp(a_col - m_t)
        qh = q_ref[0, :, h * M_DK:(h + 1) * M_DK]
        kh = k_ref[0, :, h * M_DK:(h + 1) * M_DK] * (M_DK ** -0.5)
        vh = v_ref[0, :, h * M_DV:(h + 1) * M_DV]
        s = _nt(qh, kh, HI) * w_d
        c_h = c_s[h]
        n_h = n_s[h:h + 1, :]
        num = w_a * _nn(qh, c_h, HI) + _nn(s, vh, HI)
        den = (w_a * jnp.sum(qh * n_h, axis=1, keepdims=True)
               + jnp.sum(s, axis=1, keepdims=True))
        h_out = num / jnp.maximum(jnp.abs(den), jnp.exp(-m_t))
        hid = h_out * lax.rsqrt(jnp.mean(h_out * h_out, axis=1, keepdims=True) + NORM_EPS)
        hid = hid * hn_ref[:, h * M_DV:(h + 1) * M_DV]
        hid_ref[0, :, h * M_DV:(h + 1) * M_DV] = (
            jax.nn.sigmoid(og_ref[0, :, h * M_DV:(h + 1) * M_DV]) * hid)

        m_new = m_t[t_last:t_last + 1, :]
        b_last = b_col[t_last:t_last + 1, :]
        d_last = jnp.where(rcol <= t_last, b_last - b_col + gates[:, h:h + 1], NEG_INF)
        w_last = jnp.exp(d_last - m_new)
        decay = jnp.exp(a_col[t_last:t_last + 1, :] - m_new)
        kw = kh * w_last
        c_s[h] = decay * c_h + _tn(kw, vh, HI)
        n_s[h:h + 1, :] = decay * n_h + jnp.sum(kw, axis=0, keepdims=True)
        m_row = jnp.where(lane1 == h, m_new, m_row)

    m_s[...] = m_row

    @pl.when(ci == pl.num_programs(1) - 1)
    def _():
        c_out[0] = c_s[...]
        n_out[0] = n_s[...]
        m_out[0] = m_s[...]


def mlstm_scan(proj, c0, n0, m0, gate_bias, head_norm, chunk, t_last):
    b, t, _ = proj.shape
    n_chunks = t // chunk
    hd = M_HEADS * M_DK
    vd = M_HEADS * M_DV
    m0p = jnp.pad(m0, ((0, 0), (0, LANE - M_HEADS))).reshape(b, 1, LANE)
    body = functools.partial(_mlstm_body, chunk=chunk, t_last=t_last)
    hid, c_t, n_t, m_t = pl.pallas_call(
        body,
        out_shape=(jax.ShapeDtypeStruct((b, t, vd), F32),
                   jax.ShapeDtypeStruct((b, M_HEADS, M_DK, M_DV), F32),
                   jax.ShapeDtypeStruct((b, M_HEADS, M_DK), F32),
                   jax.ShapeDtypeStruct((b, 1, LANE), F32)),
        grid=(b, n_chunks),
        in_specs=[pl.BlockSpec((1, chunk, hd), lambda i, c: (i, c, 0)),
                  pl.BlockSpec((1, chunk, hd), lambda i, c: (i, c, 1)),
                  pl.BlockSpec((1, chunk, vd), lambda i, c: (i, c, 1)),
                  pl.BlockSpec((1, chunk, vd), lambda i, c: (i, c, 2)),
                  pl.BlockSpec((1, chunk, LANE), lambda i, c: (i, c, (2 * hd + 2 * vd) // LANE)),
                  pl.BlockSpec((1, M_HEADS, M_DK, M_DV), lambda i, c: (i, 0, 0, 0)),
                  pl.BlockSpec((1, M_HEADS, M_DK), lambda i, c: (i, 0, 0)),
                  pl.BlockSpec((1, 1, LANE), lambda i, c: (i, 0, 0)),
                  pl.BlockSpec((1, LANE), lambda i, c: (0, 0)),
                  pl.BlockSpec((1, vd), lambda i, c: (0, 0))],
        out_specs=(pl.BlockSpec((1, chunk, vd), lambda i, c: (i, c, 0)),
                   pl.BlockSpec((1, M_HEADS, M_DK, M_DV), lambda i, c: (i, 0, 0, 0)),
                   pl.BlockSpec((1, M_HEADS, M_DK), lambda i, c: (i, 0, 0)),
                   pl.BlockSpec((1, 1, LANE), lambda i, c: (i, 0, 0))),
        scratch_shapes=[pltpu.VMEM((M_HEADS, M_DK, M_DV), F32),
                        pltpu.VMEM((M_HEADS, M_DK), F32),
                        pltpu.VMEM((1, LANE), F32)],
        compiler_params=_params(2), name="mlstm_scan",
    )(proj, proj, proj, proj, proj, c0, n0, m0p, gate_bias, head_norm.reshape(1, vd))
    return hid, c_t, n_t, m_t[:, 0, :M_HEADS]


def _top_rows(x, dst, k):
    for r in range(k):
        m = jnp.max(x, axis=0, keepdims=True)
        dst[r:r + 1, :] = m
        x = jnp.where(x == m, NEG_INF, x)


N_CAND = (2 + SUBLANE) * SUBLANE


def _peer_select_body(qp_ref, k1_ref, k2_ref, s1_ref, s2_ref, e2_ref, c_ref, tau_ref,
                      v1_s, v2_s, cand_s, vals_s):
    for h in range(P_HEADS):
        q1 = qp_ref[:, h * P_QDIM:h * P_QDIM + P_QDIM // 2].astype(BF16)
        q2 = qp_ref[:, h * P_QDIM + P_QDIM // 2:(h + 1) * P_QDIM].astype(BF16)
        s1 = _nt(k1_ref[h], q1)
        s2 = _nt(k2_ref[h], q2)
        _top_rows(s1, v1_s, P_TOPK)
        _top_rows(s2, v2_s, P_TOPK)
        v2_lo = v2_s[0:SUBLANE, :]
        cand_s[0:SUBLANE, :] = v2_lo + v1_s[0:1, :]
        cand_s[SUBLANE:2 * SUBLANE, :] = v2_s[SUBLANE:2 * SUBLANE, :] + v1_s[0:1, :]
        cand_s[2 * SUBLANE:3 * SUBLANE, :] = v1_s[SUBLANE:2 * SUBLANE, :] + v2_s[0:1, :]
        for r in range(1, SUBLANE):
            cand_s[(2 + r) * SUBLANE:(3 + r) * SUBLANE, :] = v2_lo + v1_s[r:r + 1, :]
        _top_rows(cand_s[...], vals_s, P_TOPK)
        top = vals_s[0:1, :]
        z = jnp.sum(jnp.exp(vals_s[...] - top), axis=0, keepdims=True)
        s1_ref[h] = s1
        s2_ref[h] = s2
        e2_ref[h] = jnp.exp(s2 - v2_s[0:1, :])
        c_ref[h] = jnp.exp(s1 - v1_s[0:1, :]) / z
        tau_ref[h:h + 1, :] = vals_s[P_TOPK - 1:P_TOPK, :]


def peer_select(qp, k1, k2, tm):
    n = qp.shape[0]
    big = jax.ShapeDtypeStruct((P_HEADS, P_NKEYS, n), F32)
    big_spec = pl.BlockSpec((P_HEADS, P_NKEYS, tm), lambda i: (0, 0, i))
    key_spec = pl.BlockSpec((P_HEADS, P_NKEYS, P_QDIM // 2), lambda i: (0, 0, 0))
    return pl.pallas_call(
        _peer_select_body,
        out_shape=(big, big, big, big, jax.ShapeDtypeStruct((P_HEADS, n), F32)),
        grid=(n // tm,),
        in_specs=[pl.BlockSpec((tm, P_HEADS * P_QDIM), lambda i: (i, 0)), key_spec, key_spec],
        out_specs=(big_spec, big_spec, big_spec, big_spec,
                   pl.BlockSpec((P_HEADS, tm), lambda i: (0, i))),
        scratch_shapes=[pltpu.VMEM((P_TOPK, tm), F32), pltpu.VMEM((P_TOPK, tm), F32),
                        pltpu.VMEM((N_CAND, tm), F32), pltpu.VMEM((P_TOPK, tm), F32)],
        compiler_params=_params(1), name="peer_select",
    )(qp, k1, k2)


def _peer_main_body(x_ref, g_ref, u_ref, vt_ref, s1_ref, s2_ref, e2_ref, c_ref, tau_ref,
                    o_ref, hn_s, acc_s, *, a_tile):
    t = pl.program_id(1)

    @pl.when(t == 0)
    def _():
        hn_s[...] = _rms(x_ref[...], g_ref[...]).astype(BF16)
        acc_s[...] = jnp.zeros_like(acc_s)

    act = _gelu(_nt(u_ref[...], hn_s[...]))
    parts = []
    for aa in range(a_tile):
        a = t * a_tile + aa
        w = None
        for h in range(P_HEADS):
            sc = s2_ref[h] + s1_ref[h, pl.ds(a, 1), :]
            term = jnp.where(sc >= tau_ref[h:h + 1, :], e2_ref[h] * c_ref[h, pl.ds(a, 1), :], 0.0)
            w = term if w is None else w + term
        parts.append((w * act[aa * P_NKEYS:(aa + 1) * P_NKEYS]).astype(BF16))
    p = jnp.concatenate(parts, axis=0)
    acc_s[...] += _nn(vt_ref[...], p)

    @pl.when(t == pl.num_programs(1) - 1)
    def _():
        o_ref[...] = x_ref[...] + acc_s[...].T


def peer_main(x, g, u, vt, sel, tm, te):
    n, d = x.shape
    s1, s2, e2, c, tau = sel
    a_tile = te // P_NKEYS
    big_spec = pl.BlockSpec((P_HEADS, P_NKEYS, tm), lambda i, t: (0, 0, i))
    body = functools.partial(_peer_main_body, a_tile=a_tile)
    return pl.pallas_call(
        body,
        out_shape=jax.ShapeDtypeStruct((n, d), F32),
        grid=(n // tm, P_EXPERTS // te),
        in_specs=[pl.BlockSpec((tm, d), lambda i, t: (i, 0)),
                  pl.BlockSpec((1, d), lambda i, t: (0, 0)),
                  pl.BlockSpec((te, d), lambda i, t: (t, 0)),
                  pl.BlockSpec((d, te), lambda i, t: (0, t)),
                  big_spec, big_spec, big_spec, big_spec,
                  pl.BlockSpec((P_HEADS, tm), lambda i, t: (0, i))],
        out_specs=pl.BlockSpec((tm, d), lambda i, t: (i, 0)),
        scratch_shapes=[pltpu.VMEM((tm, d), BF16), pltpu.VMEM((d, tm), F32)],
        compiler_params=_params(2), name="peer_main",
    )(x, g.reshape(1, d), u, vt, s1, s2, e2, c, tau)


def peer_ffn_residual(x, g, w_q, k1, k2, u, vt, tm, te):
    qp = rms_matmul(x, g, w_q, min(tm, 256))
    sel = peer_select(qp, k1, k2, tm)
    return peer_main(x, g, u, vt, sel, tm, te)


def _topk_mask(score, lane_f, k):
    sel = jnp.zeros(score.shape, F32)
    for _ in range(k):
        m = jnp.max(score, axis=1, keepdims=True)
        idx = jnp.min(jnp.where(score == m, lane_f, 1e9), axis=1, keepdims=True)
        hit = lane_f == idx
        sel = jnp.where(hit, 1.0, sel)
        score = jnp.where(hit, NEG_INF, score)
    return sel


def _block_scores(imp, pos_col, lane):
    cur = lax.shift_right_arithmetic(pos_col, SEL_BLK_LOG2)
    forced = (lane == 0) | (lane == cur) | (lane == cur - 1)
    return jnp.where(forced, jnp.inf, jnp.where(lane <= cur, imp, NEG_INF))


def _stack_heads(q, g):
    return jnp.concatenate(
        [q[:, (g * A_REP + r) * A_DH:(g * A_REP + r + 1) * A_DH] for r in range(A_REP)], axis=0)


def _combine_heads(o_ref, gate, o_cmp, o_sel, o_win, g, rows):
    for r in range(A_REP):
        h = g * A_REP + r
        rs = slice(r * rows, (r + 1) * rows)
        o_ref[:, h * A_DH:(h + 1) * A_DH] = (gate[:, 3 * h:3 * h + 1] * o_cmp[rs]
                                             + gate[:, 3 * h + 1:3 * h + 2] * o_sel[rs]
                                             + gate[:, 3 * h + 2:3 * h + 3] * o_win[rs])


def _compress_weights(w1, b1, w2):
    eye = jnp.eye(A_KV, dtype=F32)
    w1bd = jnp.einsum("cldh,gk->clgdkh", w1, eye).reshape(2, CMP_LEN, GRP_W, GRP_W).astype(BF16)
    w2bd = jnp.einsum("chd,gk->cghkd", w2, eye).reshape(2, GRP_W, GRP_W).astype(BF16)
    b1t = jnp.tile(b1, (1, A_KV)).reshape(2, 1, GRP_W)
    return w1bd, b1t, w2bd


def _overlap_matrix(n_rows, n_cols, shift):
    c_start = (np.arange(n_rows) - shift) * CMP_STRIDE
    s_start = np.arange(n_cols) * SEL_BLK
    ov = np.clip(np.minimum(c_start[:, None] + CMP_LEN, s_start[None, :] + SEL_BLK)
                 - np.maximum(c_start[:, None], s_start[None, :]), 0, None) / CMP_STRIDE
    ov[c_start < 0] = 0.0
    return jnp.asarray(ov, F32)


def _cmp_prompt_body(r_ref, w1_ref, b1_ref, w2_ref, ck_ref, cv_ref):
    for c, out_ref in ((0, ck_ref), (1, cv_ref)):
        first = None
        second = None
        for l in range(CMP_STRIDE):
            lo = l * ROW_W + c * GRP_W
            x = r_ref[0, :, lo:lo + GRP_W].astype(BF16)
            f = _nn(x, w1_ref[c, l])
            s = _nn(x, w1_ref[c, CMP_STRIDE + l])
            first = f if first is None else first + f
            second = s if second is None else second + s
        n = first.shape[0]
        hid = _gelu(first + pltpu.roll(second, n - 1, 0) + b1_ref[c])
        out_ref[0] = _nn(hid.astype(BF16), w2_ref[c])


def compress_prompt(rows16, w1bd, b1t, w2bd):
    b, n, w = rows16.shape
    out = jax.ShapeDtypeStruct((b, n, GRP_W), F32)
    return pl.pallas_call(
        _cmp_prompt_body,
        out_shape=(out, out),
        grid=(b,),
        in_specs=[pl.BlockSpec((1, n, w), lambda i: (i, 0, 0)),
                  pl.BlockSpec(w1bd.shape, lambda i: (0, 0, 0, 0)),
                  pl.BlockSpec(b1t.shape, lambda i: (0, 0, 0)),
                  pl.BlockSpec(w2bd.shape, lambda i: (0, 0, 0))],
        out_specs=(pl.BlockSpec((1, n, GRP_W), lambda i: (i, 0, 0)),
                   pl.BlockSpec((1, n, GRP_W), lambda i: (i, 0, 0))),
        compiler_params=_params(1), name="compress_prompt",
    )(rows16, w1bd, b1t, w2bd)


def _nsa_prompt_body(q_ref, gate_ref, ck_ref, cv_ref, sel_ref, win_ref, ov_ref, ex_ref, bg_ref,
                     o_ref, *, tq, t_len):
    i = pl.program_id(1)
    q0 = i * tq
    q = q_ref[...]
    pos = q0 + _iota((tq, 1), 0)
    pos4 = jnp.concatenate([pos] * A_REP, axis=0)
    gate = jax.nn.sigmoid(gate_ref[...] + bg_ref[...])
    n_cmp_rows = ck_ref.shape[1]
    n_lane = _iota((1, n_cmp_rows), 1)
    valid_cmp = (n_lane * CMP_STRIDE + (CMP_LEN - 1) <= pos4) & (n_lane < n_cmp_rows - 1)
    blk_lane = _iota((tq, LANE), 1)
    blk_lane_f = blk_lane.astype(F32)
    key_lane = _iota((1, t_len), 1)
    band = WINDOW + tq
    w_start = pl.multiple_of(jnp.maximum(q0 - WINDOW, 0), tq)
    kpos_w = w_start + _iota((1, band), 1)
    diff_w = pos4 - kpos_w
    valid_w = (diff_w >= 0) & (diff_w <= WINDOW)

    for g in range(A_KV):
        gs = slice(g * A_DH, (g + 1) * A_DH)
        vs_ = slice(GRP_W + g * A_DH, GRP_W + (g + 1) * A_DH)
        qg = _stack_heads(q, g).astype(BF16)

        p_cmp = _masked_softmax(_nt(qg, ck_ref[0, :, gs].astype(BF16)) * ATTN_SCALE, valid_cmp)
        o_cmp = _nn(p_cmp.astype(BF16), cv_ref[0, :, gs].astype(BF16))
        p_sum = p_cmp[0:tq]
        for r in range(1, A_REP):
            p_sum = p_sum + p_cmp[r * tq:(r + 1) * tq]
        imp = _nn(p_sum, ov_ref[...], HI)
        mask = _topk_mask(_block_scores(imp, pos, blk_lane), blk_lane_f, N_SEL)
        m_keys = _nn(mask.astype(BF16), ex_ref[...])
        m_keys4 = jnp.concatenate([m_keys] * A_REP, axis=0)
        valid_sel = (m_keys4 > 0.5) & (key_lane <= pos4)
        s_sel = _nt(qg, sel_ref[:, gs].astype(BF16)) * ATTN_SCALE
        p_sel = _masked_softmax(s_sel, valid_sel)
        o_sel = _nn(p_sel.astype(BF16), sel_ref[:, vs_].astype(BF16))

        kw = win_ref[pl.ds(w_start, band), gs].astype(BF16)
        vw = win_ref[pl.ds(w_start, band), vs_].astype(BF16)
        p_win = _masked_softmax(_nt(qg, kw) * ATTN_SCALE, valid_w)
        o_win = _nn(p_win.astype(BF16), vw)

        _combine_heads(o_ref, gate, o_cmp, o_sel, o_win, g, tq)


def nsa_prompt_attention(q, gate_pre, ck, cv, sel_rows, win_rows, b_gate_row, bsz, t_len, tq):
    n = q.shape[0]
    n_q = t_len // tq
    n_blk = t_len // SEL_BLK
    ov = jnp.pad(_overlap_matrix(ck.shape[1], n_blk, 0), ((0, 0), (0, LANE - n_blk)))
    ex = (np.arange(LANE)[:, None] == (np.arange(t_len)[None, :] // SEL_BLK))
    ex = jnp.asarray(ex, BF16)
    body = functools.partial(_nsa_prompt_body, tq=tq, t_len=t_len)
    return pl.pallas_call(
        body,
        out_shape=jax.ShapeDtypeStruct((n, A_QW), F32),
        grid=(bsz, n_q),
        in_specs=[pl.BlockSpec((tq, A_QW), lambda b, i: (b * n_q + i, 0)),
                  pl.BlockSpec((tq, LANE), lambda b, i: (b * n_q + i, 0)),
                  pl.BlockSpec((1,) + ck.shape[1:], lambda b, i: (b, 0, 0)),
                  pl.BlockSpec((1,) + cv.shape[1:], lambda b, i: (b, 0, 0)),
                  pl.BlockSpec((t_len, ROW_W), lambda b, i: (b, 0)),
                  pl.BlockSpec((t_len, ROW_W), lambda b, i: (b, 0)),
                  pl.BlockSpec(ov.shape, lambda b, i: (0, 0)),
                  pl.BlockSpec(ex.shape, lambda b, i: (0, 0)),
                  pl.BlockSpec((1, LANE), lambda b, i: (0, 0))],
        out_specs=pl.BlockSpec((tq, A_QW), lambda b, i: (b * n_q + i, 0)),
        compiler_params=_params(2), name="nsa_prompt_attention",
    )(q, gate_pre, ck, cv, sel_rows, win_rows, ov, ex, b_gate_row)


DEC_ROWS = 8
CMP_PAGES_PER_STEP = 16
SEL_PAGES_PER_STEP = 8
R16_PER_PAGE = PAGE_SIZE // CMP_STRIDE


def _dec_cmp_body(pt_ref, *refs, past_len, n_blk_pad):
    del pt_ref
    pages = refs[:CMP_PAGES_PER_STEP]
    (w1_ref, b1_ref, w2_ref, q_ref, ov_ref, ocmp_ref, msel_ref,
     ck_s, cv_s, carry_s) = refs[CMP_PAGES_PER_STEP:]
    s = pl.program_id(1)
    rows = CMP_PAGES_PER_STEP * R16_PER_PAGE

    @pl.when(s == 0)
    def _():
        carry_s[...] = jnp.zeros_like(carry_s)

    rid = _iota((rows, 1), 0)
    for c, dst in ((0, ck_s), (1, cv_s)):
        first = None
        second = None
        for l in range(CMP_STRIDE):
            lo = l * ROW_W + c * GRP_W
            x = jnp.concatenate([p[0, :, lo:lo + GRP_W] for p in pages], axis=0).astype(BF16)
            f = _nn(x, w1_ref[c, l])
            sc = _nn(x, w1_ref[c, CMP_STRIDE + l])
            first = f if first is None else first + f
            second = sc if second is None else second + sc
        prev = carry_s[c, SUBLANE - 1:SUBLANE, :]
        shifted = jnp.where(rid == 0, prev, pltpu.roll(first, 1, 0))
        hid = _gelu(shifted + second + b1_ref[c])
        dst[pl.ds(pl.multiple_of(s * rows, rows), rows), :] = _nn(hid.astype(BF16), w2_ref[c])
        carry_s[c] = first[rows - SUBLANE:rows, :]

    @pl.when(s == pl.num_programs(1) - 1)
    def _():
        n_rows = ck_s.shape[0]
        q = q_ref[0]
        t_col = _iota((DEC_ROWS, 1), 0)
        pos = past_len + t_col
        pos4 = jnp.concatenate([pos] * A_REP, axis=0)
        m_lane = _iota((1, n_rows), 1)
        valid = (m_lane >= 1) & ((m_lane - 1) * CMP_STRIDE + (CMP_LEN - 1) <= pos4)
        imps = []
        for g in range(A_KV):
            gs = slice(g * A_DH, (g + 1) * A_DH)
            qg = _stack_heads(q, g).astype(BF16)
            p = _masked_softmax(_nt(qg, ck_s[:, gs].astype(BF16)) * ATTN_SCALE, valid)
            o = _nn(p.astype(BF16), cv_s[:, gs].astype(BF16))
            for r in range(A_REP):
                h = g * A_REP + r
                ocmp_ref[0, :, h * A_DH:(h + 1) * A_DH] = o[r * DEC_ROWS:(r + 1) * DEC_ROWS]
            p_sum = p[0:DEC_ROWS]
            for r in range(1, A_REP):
                p_sum = p_sum + p[r * DEC_ROWS:(r + 1) * DEC_ROWS]
            imps.append(_nn(p_sum, ov_ref[...], HI))
        imp = jnp.concatenate(imps, axis=0)
        lane = _iota((A_KV * DEC_ROWS, n_blk_pad), 1)
        score = _block_scores(imp, pos4, lane)
        msel_ref[0] = _topk_mask(score, lane.astype(F32), N_SEL)


def nsa_decode_compress(cache16, page_table, w1bd, b1t, w2bd, q8, past_len, n_blk_pad):
    dbsz, n_pages = page_table.shape
    assert n_pages % CMP_PAGES_PER_STEP == 0
    n_steps = n_pages // CMP_PAGES_PER_STEP
    n_rows = n_pages * R16_PER_PAGE
    ov = _overlap_matrix(n_rows, n_blk_pad, 1)

    def page_spec(k):
        return pl.BlockSpec((1, R16_PER_PAGE, CMP_STRIDE * ROW_W),
                            lambda b, s, pt: (pt[b, s * CMP_PAGES_PER_STEP + k], 0, 0))

    grid_spec = pltpu.PrefetchScalarGridSpec(
        num_scalar_prefetch=1,
        grid=(dbsz, n_steps),
        in_specs=[page_spec(k) for k in range(CMP_PAGES_PER_STEP)] + [
            pl.BlockSpec(w1bd.shape, lambda b, s, pt: (0, 0, 0, 0)),
            pl.BlockSpec(b1t.shape, lambda b, s, pt: (0, 0, 0)),
            pl.BlockSpec(w2bd.shape, lambda b, s, pt: (0, 0, 0)),
            pl.BlockSpec((1, DEC_ROWS, A_QW), lambda b, s, pt: (b, 0, 0)),
            pl.BlockSpec(ov.shape, lambda b, s, pt: (0, 0))],
        out_specs=(pl.BlockSpec((1, DEC_ROWS, A_QW), lambda b, s, pt: (b, 0, 0)),
                   pl.BlockSpec((1, A_KV * DEC_ROWS, n_blk_pad), lambda b, s, pt: (b, 0, 0))),
        scratch_shapes=[pltpu.VMEM((n_rows, GRP_W), F32), pltpu.VMEM((n_rows, GRP_W), F32),
                        pltpu.VMEM((2, SUBLANE, GRP_W), F32)])
    body = functools.partial(_dec_cmp_body, past_len=past_len, n_blk_pad=n_blk_pad)
    return pl.pallas_call(
        body,
        out_shape=(jax.ShapeDtypeStruct((dbsz, DEC_ROWS, A_QW), F32),
                   jax.ShapeDtypeStruct((dbsz, A_KV * DEC_ROWS, n_blk_pad), F32)),
        grid_spec=grid_spec,
        compiler_params=_params(2), name="nsa_decode_compress",
    )(page_table, *([cache16] * CMP_PAGES_PER_STEP), w1bd, b1t, w2bd, q8, ov)


def _dec_sel_body(pt_ref, *refs, past_len, t_new, n_blk_pad):
    del pt_ref
    pages = refs[:SEL_PAGES_PER_STEP]
    (q_ref, msel_ref, ocmp_ref, snew_ref, wpast_ref, wnew_ref, gate_ref, bg_ref,
     o_ref, m_s, l_s, acc_s) = refs[SEL_PAGES_PER_STEP:]
    s = pl.program_id(1)
    keys = SEL_PAGES_PER_STEP * PAGE_SIZE
    blks = keys // SEL_BLK
    rows4 = A_REP * DEC_ROWS

    @pl.when(s == 0)
    def _():
        m_s[...] = jnp.full_like(m_s, NEG_INF)
        l_s[...] = jnp.zeros_like(l_s)
        acc_s[...] = jnp.zeros_like(acc_s)

    q = q_ref[0]
    expand = (_iota((n_blk_pad, keys), 0) == s * blks + lax.shift_right_arithmetic(_iota((n_blk_pad, keys), 1), SEL_BLK_LOG2))
    expand = jnp.where(expand, 1.0, 0.0).astype(BF16)

    def online(g, sc, valid, v):
        sc = jnp.where(valid, sc, NEG_INF)
        m_old = m_s[g]
        m_new = jnp.maximum(m_old, jnp.max(sc, axis=1, keepdims=True))
        m_safe = jnp.where(m_new == NEG_INF, 0.0, m_new)
        alpha = jnp.exp(m_old - m_safe)
        e = jnp.exp(sc - m_safe)
        l_s[g] = alpha * l_s[g] + jnp.sum(e, axis=1, keepdims=True)
        acc_s[g] = alpha * acc_s[g] + _nn(e.astype(BF16), v)
        m_s[g] = m_new

    for g in range(A_KV):
        gs = slice(g * A_DH, (g + 1) * A_DH)
        vs_ = slice(GRP_W + g * A_DH, GRP_W + (g + 1) * A_DH)
        qg = _stack_heads(q, g).astype(BF16)
        k = jnp.concatenate([p[0, :, gs] for p in pages], axis=0).astype(BF16)
        v = jnp.concatenate([p[0, :, vs_] for p in pages], axis=0).astype(BF16)
        mrow = msel_ref[0, g * DEC_ROWS:(g + 1) * DEC_ROWS, :].astype(BF16)
        m_keys = _nn(jnp.concatenate([mrow] * A_REP, axis=0), expand)
        online(g, _nt(qg, k) * ATTN_SCALE, m_keys > 0.5, v)

    @pl.when(s == pl.num_programs(1) - 1)
    def _():
        t_col = _iota((DEC_ROWS, 1), 0)
        t4 = jnp.concatenate([t_col] * A_REP, axis=0)
        t_key = _iota((1, DEC_ROWS), 1)
        valid_new = (t_key <= t4) & (t_key < t_new)
        gate = jax.nn.sigmoid(gate_ref[0] + bg_ref[...])
        n_win = wpast_ref.shape[1]
        kpos_w = past_len - n_win + _iota((1, n_win), 1)
        diff_w = (past_len + t4) - kpos_w
        valid_wp = (diff_w >= 0) & (diff_w <= WINDOW)
        new_blk = past_len // SEL_BLK
        for g in range(A_KV):
            gs = slice(g * A_DH, (g + 1) * A_DH)
            vs_ = slice(GRP_W + g * A_DH, GRP_W + (g + 1) * A_DH)
            qg = _stack_heads(q, g).astype(BF16)
            mnew = msel_ref[0, g * DEC_ROWS:(g + 1) * DEC_ROWS, new_blk:new_blk + 1]
            mnew4 = jnp.concatenate([mnew] * A_REP, axis=0)
            online(g, _nt(qg, snew_ref[0, :, gs].astype(BF16)) * ATTN_SCALE,
                   valid_new & (mnew4 > 0.5), snew_ref[0, :, vs_].astype(BF16))
            o_sel = acc_s[g] / jnp.maximum(l_s[g], 1e-30)

            s_p = jnp.where(valid_wp, _nt(qg, wpast_ref[0, :, gs].astype(BF16)) * ATTN_SCALE, NEG_INF)
            s_n = jnp.where(valid_new, _nt(qg, wnew_ref[0, :, gs].astype(BF16)) * ATTN_SCALE, NEG_INF)
            mx = jnp.maximum(jnp.max(s_p, axis=1, keepdims=True), jnp.max(s_n, axis=1, keepdims=True))
            mx = jnp.where(mx == NEG_INF, 0.0, mx)
            e_p = jnp.exp(s_p - mx)
            e_n = jnp.exp(s_n - mx)
            den = jnp.maximum(jnp.sum(e_p, axis=1, keepdims=True)
                              + jnp.sum(e_n, axis=1, keepdims=True), 1e-30)
            p_p = e_p / den
            p_n = e_n / den
            o_win = (_nn(p_p.astype(BF16), wpast_ref[0, :, vs_].astype(BF16))
                     + _nn(p_n.astype(BF16), wnew_ref[0, :, vs_].astype(BF16)))
            o_cmp = _stack_heads(ocmp_ref[0], g)
            _combine_heads(o_ref.at[0], gate, o_cmp, o_sel, o_win, g, DEC_ROWS)


def nsa_decode_attention(cache_rows, page_table, q8, msel, ocmp, sel_new, win_past, win_new,
                         gate8, b_gate_row, past_len, t_new, n_blk_pad):
    dbsz, n_pages = page_table.shape
    assert n_pages % SEL_PAGES_PER_STEP == 0 and past_len % SEL_BLK == 0
    n_steps = n_pages // SEL_PAGES_PER_STEP
    n_win = win_past.shape[1]

    def page_spec(k):
        return pl.BlockSpec((1, PAGE_SIZE, ROW_W),
                            lambda b, s, pt: (pt[b, s * SEL_PAGES_PER_STEP + k], 0, 0))

    per_b = lambda shp: pl.BlockSpec((1,) + shp, lambda b, s, pt: (b, 0, 0))
    grid_spec = pltpu.PrefetchScalarGridSpec(
        num_scalar_prefetch=1,
        grid=(dbsz, n_steps),
        in_specs=[page_spec(k) for k in range(SEL_PAGES_PER_STEP)] + [
            per_b((DEC_ROWS, A_QW)), per_b((A_KV * DEC_ROWS, n_blk_pad)), per_b((DEC_ROWS, A_QW)),
            per_b((DEC_ROWS, ROW_W)), per_b((n_win, ROW_W)), per_b((DEC_ROWS, ROW_W)),
            per_b((DEC_ROWS, LANE)), pl.BlockSpec((1, LANE), lambda b, s, pt: (0, 0))],
        out_specs=per_b((DEC_ROWS, A_QW)),
        scratch_shapes=[pltpu.VMEM((A_KV, A_REP * DEC_ROWS, 1), F32),
                        pltpu.VMEM((A_KV, A_REP * DEC_ROWS, 1), F32),
                        pltpu.VMEM((A_KV, A_REP * DEC_ROWS, A_DH), F32)])
    body = functools.partial(_dec_sel_body, past_len=past_len, t_new=t_new, n_blk_pad=n_blk_pad)
    return pl.pallas_call(
        body,
        out_shape=jax.ShapeDtypeStruct((dbsz, DEC_ROWS, A_QW), F32),
        grid_spec=grid_spec,
        compiler_params=_params(2), name="nsa_decode_attention",
    )(page_table, *([cache_rows] * SEL_PAGES_PER_STEP), q8, msel, ocmp, sel_new, win_past,
      win_new, gate8, b_gate_row)


PROMPT_TM = 256
PEER_TM = 512
PEER_TE = 512
NSA_TQ = 128


def kernel(x_prompt, x_sample, cache_cmp_kv, cache_sel_kv, state_win_kv, state_C, state_n, state_m,
           page_table, norm_mix, norm_ffn, norm_final, mlstm_w_in, mlstm_b_i, mlstm_b_f,
           mlstm_head_norm, mlstm_w_out, nsa_w_in, nsa_b_gate, nsa_cmp_w1, nsa_cmp_b1, nsa_cmp_w2,
           nsa_w_out, peer_w_q, peer_sub_keys, peer_u, peer_v):
    bsz, t_len, d = x_prompt.shape
    dbsz, t_s, _ = x_sample.shape
    n_pages = page_table.shape[1]
    past_len = n_pages * PAGE_SIZE
    pool = cache_cmp_kv.shape[1]
    assert norm_mix.shape[0] == 2 and d == D_MODEL and t_s <= DEC_ROWS // 2
    assert t_len % M_CHUNK == 0 and t_len % NSA_TQ == 0 and t_len >= WINDOW

    xp = x_prompt.reshape(bsz * t_len, d)
    reps = DEC_ROWS // t_s
    xs = jnp.concatenate([x_sample] * reps, axis=1).reshape(dbsz * DEC_ROWS, d)
    n_s = dbsz * DEC_ROWS

    def peer(x, layer, tm):
        keys = peer_sub_keys[layer].astype(BF16)
        return peer_ffn_residual(x, norm_ffn[layer], peer_w_q[layer].astype(BF16),
                                 keys[:, 0], keys[:, 1], peer_u[layer].astype(BF16),
                                 peer_v[layer].T.astype(BF16), tm, PEER_TE)

    w_in = jnp.pad(mlstm_w_in[0], ((0, 0), (0, M_PROJ_PAD - M_PROJ))).astype(BF16)
    w_out = mlstm_w_out[0].astype(BF16)
    gate_bias = jnp.pad(jnp.concatenate([mlstm_b_i[0], mlstm_b_f[0]]),
                        (0, LANE - 2 * M_HEADS)).reshape(1, LANE)
    proj_p = rms_matmul(xp, norm_mix[0], w_in, PROMPT_TM).reshape(bsz, t_len, M_PROJ_PAD)
    proj_s = rms_matmul(xs, norm_mix[0], w_in, n_s).reshape(dbsz, DEC_ROWS, M_PROJ_PAD)
    hid_p, c_p, n_p, m_p = mlstm_scan(
        proj_p, jnp.zeros((bsz, M_HEADS, M_DK, M_DV), F32), jnp.zeros((bsz, M_HEADS, M_DK), F32),
        jnp.zeros((bsz, M_HEADS), F32), gate_bias, mlstm_head_norm[0], M_CHUNK, M_CHUNK - 1)
    hid_s, c_s, n_st, m_st = mlstm_scan(
        proj_s, state_C[0], state_n[0], state_m[0], gate_bias, mlstm_head_norm[0],
        DEC_ROWS, t_s - 1)
    xp = matmul_residual(hid_p.reshape(bsz * t_len, d), w_out, xp, PROMPT_TM)
    xs = matmul_residual(hid_s.reshape(n_s, d), w_out, xs, n_s)
    xp = peer(xp, 0, PEER_TM)
    xs = peer(xs, 0, n_s)

    w_in_a = jnp.pad(nsa_w_in[0], ((0, 0), (0, A_PROJ_PAD - A_PROJ))).astype(BF16)
    w_out_a = nsa_w_out[0].astype(BF16)
    b_gate_row = jnp.pad(nsa_b_gate[0].reshape(-1), (0, LANE - 3 * A_HEADS)).reshape(1, LANE)
    w1bd, b1t, w2bd = _compress_weights(nsa_cmp_w1[0], nsa_cmp_b1[0], nsa_cmp_w2[0])
    tabs_p = rope_tables(jnp.arange(t_len, dtype=jnp.int32))
    pos_s = past_len + jnp.arange(DEC_ROWS, dtype=jnp.int32)
    tabs_s = tuple(jnp.tile(a, (dbsz, 1)) for a in rope_tables(pos_s))

    q_p, cmp_p, sel_p, win_p, gate_p = nsa_project(xp, norm_mix[1], w_in_a, tabs_p, PROMPT_TM)
    q_s, cmp_s, sel_s, win_s, gate_s = nsa_project(xs, norm_mix[1], w_in_a, tabs_s, n_s)

    ck, cv = compress_prompt(cmp_p.reshape(bsz, t_len // CMP_STRIDE, CMP_STRIDE * ROW_W),
                             w1bd, b1t, w2bd)
    o_p = nsa_prompt_attention(q_p, gate_p, ck, cv, sel_p, win_p, b_gate_row, bsz, t_len, NSA_TQ)
    xp = matmul_residual(o_p, w_out_a, xp, PROMPT_TM)

    n_blk = -(-(past_len + t_s) // SEL_BLK)
    n_blk_pad = -(-n_blk // LANE) * LANE
    q8 = q_s.reshape(dbsz, DEC_ROWS, A_QW)
    cache16 = cache_cmp_kv[0].reshape(pool, R16_PER_PAGE, CMP_STRIDE * ROW_W)
    o_cmp, msel = nsa_decode_compress(cache16, page_table, w1bd, b1t, w2bd, q8, past_len, n_blk_pad)
    win_past = state_win_kv[0].reshape(dbsz, -1, ROW_W)
    o_s = nsa_decode_attention(
        cache_sel_kv[0].reshape(pool, PAGE_SIZE, ROW_W), page_table, q8, msel, o_cmp,
        sel_s.reshape(dbsz, DEC_ROWS, ROW_W), win_past, win_s.reshape(dbsz, DEC_ROWS, ROW_W),
        gate_s.reshape(dbsz, DEC_ROWS, LANE), b_gate_row, past_len, t_s, n_blk_pad)
    xs = matmul_residual(o_s.reshape(n_s, d), w_out_a, xs, n_s)
    xp = peer(xp, 1, PEER_TM)
    xs = peer(xs, 1, n_s)

    y_p = final_norm(xp, norm_final, PROMPT_TM).reshape(bsz, t_len, d)
    y_s = final_norm(xs, norm_final, n_s).reshape(dbsz, DEC_ROWS, d)[:, :t_s]

    row_shape = (2, A_KV, A_DH)
    kv_p = lambda a: a.reshape((1, bsz, t_len) + row_shape)
    kv_s = lambda a: a.reshape((dbsz, DEC_ROWS) + row_shape)[None, :, :t_s]
    win_rows = min(WINDOW, t_len)
    win_buf_p = win_p.reshape((bsz, t_len) + row_shape)[None, :, t_len - win_rows:]
    win_new_s = win_s.reshape((dbsz, DEC_ROWS) + row_shape)[:, :t_s]
    win_buf_s = jnp.concatenate([state_win_kv[0], win_new_s], axis=1)[None, :, t_s:]
    return (y_p, y_s,
            kv_p(cmp_p), kv_p(sel_p), win_buf_p,
            c_p[None], n_p[None], m_p[None],
            kv_s(cmp_s), kv_s(sel_s), win_buf_s,
            c_s[None], n_st[None], m_st[None])
```

```python
import functools

import numpy as np
import jax
import jax.numpy as jnp
from jax import lax
from jax.experimental import pallas as pl
from jax.experimental.pallas import tpu as pltpu

F32 = jnp.float32
BF16 = jnp.bfloat16
HI = lax.Precision.HIGHEST

V7X_VMEM_BYTES = 64 * 1024 * 1024
VMEM_LIMIT = V7X_VMEM_BYTES - 8 * 1024 * 1024
LANE = 128
SUBLANE = 8

D_MODEL = 1024
NORM_EPS = 1e-6

M_HEADS = 8
M_DK = 64
M_DV = 128
M_CHUNK = 64
M_PROJ = 2 * M_HEADS * M_DK + 2 * M_HEADS * M_DV + 2 * M_HEADS
M_PROJ_PAD = 3200

A_HEADS = 16
A_KV = 4
A_REP = 4
A_DH = 64
ROT_DIM = 16
ROPE_THETA = 500000.0
CMP_STRIDE = 16
CMP_LEN = 32
SEL_BLK = 64
SEL_BLK_LOG2 = 6
N_SEL = 16
WINDOW = 512
A_QW = 1024
A_KVW = 1536
A_PROJ = A_QW + A_KVW + 3 * A_HEADS
A_PROJ_PAD = 2688
ATTN_SCALE = A_DH ** -0.5
GRP_W = A_KV * A_DH
ROW_W = 2 * GRP_W
PAGE_SIZE = 128

P_HEADS = 8
P_NKEYS = 128
P_EXPERTS = P_NKEYS * P_NKEYS
P_QDIM = 256
P_TOPK = 16

NEG_INF = float("-inf")


def _params(n_axes):
    return pltpu.CompilerParams(dimension_semantics=("arbitrary",) * n_axes,
                                vmem_limit_bytes=VMEM_LIMIT)


def _nn(a, b, precision=None):
    return jnp.dot(a, b, preferred_element_type=F32, precision=precision)


def _nt(a, b, precision=None):
    return lax.dot_general(a, b, (((1,), (1,)), ((), ())), preferred_element_type=F32,
                           precision=precision)


def _tn(a, b, precision=None):
    return lax.dot_general(a, b, (((0,), (0,)), ((), ())), preferred_element_type=F32,
                           precision=precision)


def _gelu(x):
    return 0.5 * x * (1.0 + lax.erf(x * (2.0 ** -0.5)))


def _rms(x, g):
    return x * lax.rsqrt(jnp.mean(x * x, axis=-1, keepdims=True) + NORM_EPS) * g


def _masked_softmax(s, valid):
    s = jnp.where(valid, s, NEG_INF)
    m = jnp.max(s, axis=-1, keepdims=True)
    m = jnp.where(m == NEG_INF, 0.0, m)
    e = jnp.exp(s - m)
    return e / jnp.maximum(jnp.sum(e, axis=-1, keepdims=True), 1e-30)


def _iota(shape, dim):
    return lax.broadcasted_iota(jnp.int32, shape, dim)


def _rms_mm_body(x_ref, g_ref, w_ref, o_ref):
    o_ref[...] = _nn(_rms(x_ref[...], g_ref[...]).astype(BF16), w_ref[...])


def rms_matmul(x, g, w, tm):
    n, d = x.shape
    nout = w.shape[1]
    return pl.pallas_call(
        _rms_mm_body,
        out_shape=jax.ShapeDtypeStruct((n, nout), F32),
        grid=(n // tm,),
        in_specs=[pl.BlockSpec((tm, d), lambda i: (i, 0)),
                  pl.BlockSpec((1, d), lambda i: (0, 0)),
                  pl.BlockSpec((d, nout), lambda i: (0, 0))],
        out_specs=pl.BlockSpec((tm, nout), lambda i: (i, 0)),
        compiler_params=_params(1), name="rms_matmul",
    )(x, g.reshape(1, d), w)


def _mm_res_body(a_ref, w_ref, x_ref, o_ref):
    o_ref[...] = x_ref[...] + _nn(a_ref[...].astype(BF16), w_ref[...])


def matmul_residual(a, w, x, tm):
    n, k = a.shape
    d = w.shape[1]
    return pl.pallas_call(
        _mm_res_body,
        out_shape=jax.ShapeDtypeStruct((n, d), F32),
        grid=(n // tm,),
        in_specs=[pl.BlockSpec((tm, k), lambda i: (i, 0)),
                  pl.BlockSpec((k, d), lambda i: (0, 0)),
                  pl.BlockSpec((tm, d), lambda i: (i, 0))],
        out_specs=pl.BlockSpec((tm, d), lambda i: (i, 0)),
        compiler_params=_params(1), name="matmul_residual",
    )(a, w, x)


def _final_norm_body(x_ref, g_ref, o_ref):
    o_ref[...] = _rms(x_ref[...], g_ref[...])


def final_norm(x, g, tm):
    n, d = x.shape
    return pl.pallas_call(
        _final_norm_body,
        out_shape=jax.ShapeDtypeStruct((n, d), F32),
        grid=(n // tm,),
        in_specs=[pl.BlockSpec((tm, d), lambda i: (i, 0)),
                  pl.BlockSpec((1, d), lambda i: (0, 0))],
        out_specs=pl.BlockSpec((tm, d), lambda i: (i, 0)),
        compiler_params=_params(1), name="final_norm",
    )(x, g.reshape(1, d))


def _nsa_proj_body(x_ref, g_ref, w_ref, c_ref, sa_ref, sb_ref,
                   q_ref, cmp_ref, sel_ref, win_ref, gate_ref):
    y = _nn(_rms(x_ref[...], g_ref[...]).astype(BF16), w_ref[...])
    c = c_ref[...]
    sa = sa_ref[...]
    sb = sb_ref[...]

    def rot(z):
        return (z * c + pltpu.roll(z, GRP_W - ROT_DIM // 2, 1) * sa
                + pltpu.roll(z, ROT_DIM // 2, 1) * sb)

    for j in range(A_QW // GRP_W):
        q_ref[:, j * GRP_W:(j + 1) * GRP_W] = rot(y[:, j * GRP_W:(j + 1) * GRP_W])
    for ref, base in ((cmp_ref, A_QW), (sel_ref, A_QW + ROW_W), (win_ref, A_QW + 2 * ROW_W)):
        ref[:, 0:GRP_W] = rot(y[:, base:base + GRP_W])
        ref[:, GRP_W:ROW_W] = y[:, base + GRP_W:base + ROW_W]
    gate_ref[...] = y[:, A_QW + A_KVW:A_PROJ_PAD]


def nsa_project(x, g, w, tabs, tm):
    n, d = x.shape
    n_tab = tabs[0].shape[0] // tm
    tab_spec = pl.BlockSpec((tm, GRP_W), lambda i: (i % n_tab, 0))
    row = lambda w_: pl.BlockSpec((tm, w_), lambda i: (i, 0))
    return pl.pallas_call(
        _nsa_proj_body,
        out_shape=(jax.ShapeDtypeStruct((n, A_QW), F32),
                   jax.ShapeDtypeStruct((n, ROW_W), F32),
                   jax.ShapeDtypeStruct((n, ROW_W), F32),
                   jax.ShapeDtypeStruct((n, ROW_W), F32),
                   jax.ShapeDtypeStruct((n, LANE), F32)),
        grid=(n // tm,),
        in_specs=[row(d), pl.BlockSpec((1, d), lambda i: (0, 0)),
                  pl.BlockSpec((d, A_PROJ_PAD), lambda i: (0, 0)),
                  tab_spec, tab_spec, tab_spec],
        out_specs=(row(A_QW), row(ROW_W), row(ROW_W), row(ROW_W), row(LANE)),
        compiler_params=_params(1), name="nsa_project",
    )(x, g.reshape(1, d), w, *tabs)


def rope_tables(pos):
    half = ROT_DIM // 2
    inv_freq = ROPE_THETA ** (-(jnp.arange(half, dtype=F32) * (2.0 / ROT_DIM)))
    ang = pos.astype(F32)[:, None] * inv_freq[None, :]
    cos, sin = jnp.cos(ang), jnp.sin(ang)
    t = pos.shape[0]
    one = jnp.ones((t, A_DH - ROT_DIM), F32)
    zero = jnp.zeros((t, A_DH - ROT_DIM), F32)
    z8 = jnp.zeros((t, half), F32)
    c = jnp.concatenate([cos, cos, one], axis=1)
    sa = jnp.concatenate([-sin, z8, zero], axis=1)
    sb = jnp.concatenate([z8, sin, zero], axis=1)
    return tuple(jnp.tile(a, (1, A_KV)) for a in (c, sa, sb))


def _mlstm_body(q_ref, k_ref, v_ref, og_ref, gt_ref, c0_ref, n0_ref, m0_ref, gb_ref, hn_ref,
                hid_ref, c_out, n_out, m_out, c_s, n_s, m_s, *, chunk, t_last, nb):
    ci = pl.program_id(1)

    @pl.when(ci == 0)
    def _():
        c_s[...] = c0_ref[...]
        n_s[...] = n0_ref[...]
        m_s[...] = m0_ref[...]

    L = chunk
    row = _iota((L, L), 0)
    col = _iota((L, L), 1)
    causal = row >= col
    tril = causal.astype(F32)
    eye8 = (_iota((SUBLANE, LANE), 0) == _iota((SUBLANE, LANE), 1)).astype(F32)
    rcol = _iota((L, 1), 0)
    lane1 = _iota((1, LANE), 1)

    units = [(bi, h) for bi in range(nb) for h in range(M_HEADS)]
    gates, b_all, g_rows = [], [], []
    for bi in range(nb):
        gt = gt_ref[bi] + gb_ref[...]
        log_f = jnp.minimum(gt, 0.0) - jnp.log1p(jnp.exp(-jnp.abs(gt)))
        ba = _nn(tril, log_f, HI)
        g_mat = gt - pltpu.roll(ba, LANE - M_HEADS, 1)
        gates.append(gt)
        b_all.append(ba)
        g_rows.append(_nt(eye8, g_mat, HI))

    qs, ks, vs, cs, ns = {}, {}, {}, {}, {}
    s_raw, q_c = {}, {}
    for u in units:
        bi, h = u
        qs[u] = q_ref[bi, :, h * M_DK:(h + 1) * M_DK]
        ks[u] = k_ref[bi, :, h * M_DK:(h + 1) * M_DK] * (M_DK ** -0.5)
        vs[u] = v_ref[bi, :, h * M_DV:(h + 1) * M_DV]
        cs[u] = c_s[bi, h]
        ns[u] = n_s[bi, h:h + 1, :]
        s_raw[u] = _nt(qs[u].astype(BF16), ks[u].astype(BF16))
        q_c[u] = _nn(qs[u].astype(BF16), cs[u].astype(BF16))

    b_col, a_col, m_t, w_d, w_a = {}, {}, {}, {}, {}
    for u in units:
        bi, h = u
        b_col[u] = b_all[bi][:, M_HEADS + h:M_HEADS + h + 1]
        a_col[u] = b_col[u] + m_s[bi][:, h:h + 1]
        dm = jnp.where(causal, b_col[u] + g_rows[bi][h:h + 1, :], NEG_INF)
        m_t[u] = jnp.maximum(a_col[u], jnp.max(dm, axis=1, keepdims=True))
        w_d[u] = jnp.exp(dm - m_t[u])
        w_a[u] = jnp.exp(a_col[u] - m_t[u])

    for u in units:
        bi, h = u
        s = s_raw[u] * w_d[u]
        num = w_a[u] * q_c[u] + _nn(s.astype(BF16), vs[u].astype(BF16))
        den = (w_a[u] * jnp.sum(qs[u] * ns[u], axis=1, keepdims=True)
               + jnp.sum(s, axis=1, keepdims=True))
        h_out = num / jnp.maximum(jnp.abs(den), jnp.exp(-m_t[u]))
        hid = h_out * lax.rsqrt(jnp.mean(h_out * h_out, axis=1, keepdims=True) + NORM_EPS)
        hid = hid * hn_ref[:, h * M_DV:(h + 1) * M_DV]
        hid_ref[bi, :, h * M_DV:(h + 1) * M_DV] = (
            jax.nn.sigmoid(og_ref[bi, :, h * M_DV:(h + 1) * M_DV]) * hid)

    m_rows = [m_s[bi] for bi in range(nb)]
    for u in units:
        bi, h = u
        m_new = m_t[u][t_last:t_last + 1, :]
        b_last = b_col[u][t_last:t_last + 1, :]
        d_last = jnp.where(rcol <= t_last, b_last - b_col[u] + gates[bi][:, h:h + 1], NEG_INF)
        w_last = jnp.exp(d_last - m_new)
        decay = jnp.exp(a_col[u][t_last:t_last + 1, :] - m_new)
        kw = ks[u] * w_last
        c_s[bi, h] = decay * cs[u] + _tn(kw, vs[u], HI)
        n_s[bi, h:h + 1, :] = decay * ns[u] + jnp.sum(kw, axis=0, keepdims=True)
        m_rows[bi] = jnp.where(lane1 == h, m_new, m_rows[bi])
    for bi in range(nb):
        m_s[bi] = m_rows[bi]

    @pl.when(ci == pl.num_programs(1) - 1)
    def _():
        c_out[...] = c_s[...]
        n_out[...] = n_s[...]
        m_out[...] = m_s[...]


MLSTM_SEQ_PER_STEP = 2


def mlstm_scan(proj, c0, n0, m0, gate_bias, head_norm, chunk, t_last):
    b, t, _ = proj.shape
    nb = MLSTM_SEQ_PER_STEP
    assert b % nb == 0
    n_chunks = t // chunk
    hd = M_HEADS * M_DK
    vd = M_HEADS * M_DV
    m0p = jnp.pad(m0, ((0, 0), (0, LANE - M_HEADS))).reshape(b, 1, LANE)
    body = functools.partial(_mlstm_body, chunk=chunk, t_last=t_last, nb=nb)
    hid, c_t, n_t, m_t = pl.pallas_call(
        body,
        out_shape=(jax.ShapeDtypeStruct((b, t, vd), F32),
                   jax.ShapeDtypeStruct((b, M_HEADS, M_DK, M_DV), F32),
                   jax.ShapeDtypeStruct((b, M_HEADS, M_DK), F32),
                   jax.ShapeDtypeStruct((b, 1, LANE), F32)),
        grid=(b // nb, n_chunks),
        in_specs=[pl.BlockSpec((nb, chunk, hd), lambda i, c: (i, c, 0)),
                  pl.BlockSpec((nb, chunk, hd), lambda i, c: (i, c, 1)),
                  pl.BlockSpec((nb, chunk, vd), lambda i, c: (i, c, 1)),
                  pl.BlockSpec((nb, chunk, vd), lambda i, c: (i, c, 2)),
                  pl.BlockSpec((nb, chunk, LANE), lambda i, c: (i, c, (2 * hd + 2 * vd) // LANE)),
                  pl.BlockSpec((nb, M_HEADS, M_DK, M_DV), lambda i, c: (i, 0, 0, 0)),
                  pl.BlockSpec((nb, M_HEADS, M_DK), lambda i, c: (i, 0, 0)),
                  pl.BlockSpec((nb, 1, LANE), lambda i, c: (i, 0, 0)),
                  pl.BlockSpec((1, LANE), lambda i, c: (0, 0)),
                  pl.BlockSpec((1, vd), lambda i, c: (0, 0))],
        out_specs=(pl.BlockSpec((nb, chunk, vd), lambda i, c: (i, c, 0)),
                   pl.BlockSpec((nb, M_HEADS, M_DK, M_DV), lambda i, c: (i, 0, 0, 0)),
                   pl.BlockSpec((nb, M_HEADS, M_DK), lambda i, c: (i, 0, 0)),
                   pl.BlockSpec((nb, 1, LANE), lambda i, c: (i, 0, 0))),
        scratch_shapes=[pltpu.VMEM((nb, M_HEADS, M_DK, M_DV), F32),
                        pltpu.VMEM((nb, M_HEADS, M_DK), F32),
                        pltpu.VMEM((nb, 1, LANE), F32)],
        compiler_params=_params(2), name="mlstm_scan",
    )(proj, proj, proj, proj, proj, c0, n0, m0p, gate_bias, head_norm.reshape(1, vd))
    return hid, c_t, n_t, m_t[:, 0, :M_HEADS]


N_CAND = (2 + SUBLANE) * SUBLANE
RANK_OUT = float(P_TOPK)


def _top_rows(x, dst, k, with_rank=False):
    rank = jnp.full(x.shape, RANK_OUT, F32) if with_rank else None
    for r in range(k):
        m = jnp.max(x, axis=0, keepdims=True)
        dst[r:r + 1, :] = m
        hit = x == m
        if with_rank:
            rank = jnp.where(hit, float(r), rank)
        x = jnp.where(hit, NEG_INF, x)
    return rank


def _peer_select_body(qp_ref, k1_ref, k2_ref, r2_ref, e2_ref, l_ref, c_ref,
                      v1_s, v2_s, cand_s, vals_s):
    for h in range(P_HEADS):
        q1 = qp_ref[:, h * P_QDIM:h * P_QDIM + P_QDIM // 2].astype(BF16)
        q2 = qp_ref[:, h * P_QDIM + P_QDIM // 2:(h + 1) * P_QDIM].astype(BF16)
        s1 = _nt(k1_ref[h], q1)
        s2 = _nt(k2_ref[h], q2)
        _top_rows(s1, v1_s, P_TOPK)
        rank2 = _top_rows(s2, v2_s, P_TOPK, with_rank=True)
        v2_lo = v2_s[0:SUBLANE, :]
        cand_s[0:SUBLANE, :] = v2_lo + v1_s[0:1, :]
        cand_s[SUBLANE:2 * SUBLANE, :] = v2_s[SUBLANE:2 * SUBLANE, :] + v1_s[0:1, :]
        cand_s[2 * SUBLANE:3 * SUBLANE, :] = v1_s[SUBLANE:2 * SUBLANE, :] + v2_s[0:1, :]
        for r in range(1, SUBLANE):
            cand_s[(2 + r) * SUBLANE:(3 + r) * SUBLANE, :] = v2_lo + v1_s[r:r + 1, :]
        _top_rows(cand_s[...], vals_s, P_TOPK)
        top = vals_s[0:1, :]
        tau = vals_s[P_TOPK - 1:P_TOPK, :]
        z = jnp.sum(jnp.exp(vals_s[...] - top), axis=0, keepdims=True)
        n_keep = jnp.zeros(s1.shape, F32)
        for r in range(P_TOPK):
            n_keep = n_keep + jnp.where(v2_s[r:r + 1, :] + s1 >= tau, 1.0, 0.0)
        r2_ref[h] = rank2.astype(BF16)
        e2_ref[h] = jnp.exp(s2 - v2_s[0:1, :]).astype(BF16)
        l_ref[h] = n_keep
        c_ref[h] = jnp.exp(s1 - v1_s[0:1, :]) / z


def peer_select(qp, k1, k2, tm):
    n = qp.shape[0]
    spec = pl.BlockSpec((P_HEADS, P_NKEYS, tm), lambda i: (0, 0, i))
    key_spec = pl.BlockSpec((P_HEADS, P_NKEYS, P_QDIM // 2), lambda i: (0, 0, 0))
    shp = lambda dt: jax.ShapeDtypeStruct((P_HEADS, P_NKEYS, n), dt)
    return pl.pallas_call(
        _peer_select_body,
        out_shape=(shp(BF16), shp(BF16), shp(F32), shp(F32)),
        grid=(n // tm,),
        in_specs=[pl.BlockSpec((tm, P_HEADS * P_QDIM), lambda i: (i, 0)), key_spec, key_spec],
        out_specs=(spec, spec, spec, spec),
        scratch_shapes=[pltpu.VMEM((P_TOPK, tm), F32), pltpu.VMEM((P_TOPK, tm), F32),
                        pltpu.VMEM((N_CAND, tm), F32), pltpu.VMEM((P_TOPK, tm), F32)],
        compiler_params=_params(1), name="peer_select",
    )(qp, k1, k2)


def _peer_main_body(x_ref, g_ref, u_ref, vt_ref, r2_ref, e2_ref, l_ref, c_ref,
                    o_ref, hn_s, acc_s, *, a_tile):
    t = pl.program_id(1)

    @pl.when(t == 0)
    def _():
        hn_s[...] = _rms(x_ref[...], g_ref[...]).astype(BF16)
        acc_s[...] = jnp.zeros_like(acc_s)

    act = _gelu(_nt(u_ref[...], hn_s[...])).astype(BF16)
    tm = act.shape[1]
    zero = jnp.zeros((P_NKEYS, tm), BF16)
    parts = []
    for aa in range(a_tile):
        w = None
        for h in range(P_HEADS):
            keep = jnp.broadcast_to(l_ref[h, aa:aa + 1, :], (P_NKEYS, tm)).astype(BF16)
            coef = jnp.broadcast_to(c_ref[h, aa:aa + 1, :], (P_NKEYS, tm)).astype(BF16)
            term = jnp.where(r2_ref[h] < keep, e2_ref[h], zero) * coef
            w = term if w is None else w + term
        parts.append(w * act[aa * P_NKEYS:(aa + 1) * P_NKEYS])
    acc_s[...] += _nn(vt_ref[...], jnp.concatenate(parts, axis=0))

    @pl.when(t == pl.num_programs(1) - 1)
    def _():
        o_ref[...] = x_ref[...] + acc_s[...].T


def peer_main(x, g, u, vt, sel, tm, te):
    n, d = x.shape
    r2, e2, n_keep, coef = sel
    a_tile = te // P_NKEYS
    assert a_tile % SUBLANE == 0
    full_spec = pl.BlockSpec((P_HEADS, P_NKEYS, tm), lambda i, t: (0, 0, i))
    row_spec = pl.BlockSpec((P_HEADS, a_tile, tm), lambda i, t: (0, t, i))
    body = functools.partial(_peer_main_body, a_tile=a_tile)
    return pl.pallas_call(
        body,
        out_shape=jax.ShapeDtypeStruct((n, d), F32),
        grid=(n // tm, P_EXPERTS // te),
        in_specs=[pl.BlockSpec((tm, d), lambda i, t: (i, 0)),
                  pl.BlockSpec((1, d), lambda i, t: (0, 0)),
                  pl.BlockSpec((te, d), lambda i, t: (t, 0)),
                  pl.BlockSpec((d, te), lambda i, t: (0, t)),
                  full_spec, full_spec, row_spec, row_spec],
        out_specs=pl.BlockSpec((tm, d), lambda i, t: (i, 0)),
        scratch_shapes=[pltpu.VMEM((tm, d), BF16), pltpu.VMEM((d, tm), F32)],
        compiler_params=_params(2), name="peer_main",
    )(x, g.reshape(1, d), u, vt, r2, e2, n_keep, coef)


def peer_ffn_residual(x, g, w_q, k1, k2, u, vt, tm, te):
    qp = rms_matmul(x, g, w_q, min(tm, 256))
    sel = peer_select(qp, k1, k2, tm)
    return peer_main(x, g, u, vt, sel, tm, te)


def _topk_mask(score, lane_f, k):
    sel = jnp.zeros(score.shape, F32)
    for _ in range(k):
        m = jnp.max(score, axis=1, keepdims=True)
        idx = jnp.min(jnp.where(score == m, lane_f, 1e9), axis=1, keepdims=True)
        hit = lane_f == idx
        sel = jnp.where(hit, 1.0, sel)
        score = jnp.where(hit, NEG_INF, score)
    return sel


def _block_scores(imp, pos_col, lane):
    cur = lax.shift_right_arithmetic(pos_col, SEL_BLK_LOG2)
    forced = (lane == 0) | (lane == cur) | (lane == cur - 1)
    return jnp.where(forced, jnp.inf, jnp.where(lane <= cur, imp, NEG_INF))


def _stack_heads(q, g):
    return jnp.concatenate(
        [q[:, (g * A_REP + r) * A_DH:(g * A_REP + r + 1) * A_DH] for r in range(A_REP)], axis=0)


def _combine_heads(o_ref, gate, o_cmp, o_sel, o_win, g, rows):
    for r in range(A_REP):
        h = g * A_REP + r
        rs = slice(r * rows, (r + 1) * rows)
        o_ref[:, h * A_DH:(h + 1) * A_DH] = (gate[:, 3 * h:3 * h + 1] * o_cmp[rs]
                                             + gate[:, 3 * h + 1:3 * h + 2] * o_sel[rs]
                                             + gate[:, 3 * h + 2:3 * h + 3] * o_win[rs])


def _online_softmax_step(m_ref, l_ref, acc_ref, g, sc, valid, pv):
    sc = jnp.where(valid, sc, NEG_INF)
    m_old = m_ref[g]
    m_new = jnp.maximum(m_old, jnp.max(sc, axis=1, keepdims=True))
    m_safe = jnp.where(m_new == NEG_INF, 0.0, m_new)
    alpha = jnp.exp(m_old - m_safe)
    e = jnp.exp(sc - m_safe)
    l_ref[g] = alpha * l_ref[g] + jnp.sum(e, axis=1, keepdims=True)
    acc_ref[g] = alpha * acc_ref[g] + pv(e.astype(BF16))
    m_ref[g] = m_new


def _compress_weights(w1, b1, w2):
    eye = jnp.eye(A_KV, dtype=F32)
    w1bd = jnp.einsum("cldh,gk->clgdkh", w1, eye).reshape(2, CMP_LEN, GRP_W, GRP_W).astype(BF16)
    w2bd = jnp.einsum("chd,gk->cghkd", w2, eye).reshape(2, GRP_W, GRP_W).astype(BF16)
    b1t = jnp.tile(b1, (1, A_KV)).reshape(2, 1, GRP_W)
    return w1bd, b1t, w2bd


def _overlap_matrix(n_rows, n_cols, shift):
    c_start = (np.arange(n_rows) - shift) * CMP_STRIDE
    s_start = np.arange(n_cols) * SEL_BLK
    ov = np.clip(np.minimum(c_start[:, None] + CMP_LEN, s_start[None, :] + SEL_BLK)
                 - np.maximum(c_start[:, None], s_start[None, :]), 0, None) / CMP_STRIDE
    ov[c_start < 0] = 0.0
    return jnp.asarray(ov, F32)


def _cmp_prompt_body(r_ref, w1_ref, b1_ref, w2_ref, ck_ref, cv_ref):
    for c, out_ref in ((0, ck_ref), (1, cv_ref)):
        first = None
        second = None
        for l in range(CMP_STRIDE):
            lo = l * ROW_W + c * GRP_W
            x = r_ref[0, :, lo:lo + GRP_W].astype(BF16)
            f = _nn(x, w1_ref[c, l])
            s = _nn(x, w1_ref[c, CMP_STRIDE + l])
            first = f if first is None else first + f
            second = s if second is None else second + s
        n = first.shape[0]
        hid = _gelu(first + pltpu.roll(second, n - 1, 0) + b1_ref[c])
        out_ref[0] = _nn(hid.astype(BF16), w2_ref[c])


def compress_prompt(rows16, w1bd, b1t, w2bd):
    b, n, w = rows16.shape
    out = jax.ShapeDtypeStruct((b, n, GRP_W), F32)
    return pl.pallas_call(
        _cmp_prompt_body,
        out_shape=(out, out),
        grid=(b,),
        in_specs=[pl.BlockSpec((1, n, w), lambda i: (i, 0, 0)),
                  pl.BlockSpec(w1bd.shape, lambda i: (0, 0, 0, 0)),
                  pl.BlockSpec(b1t.shape, lambda i: (0, 0, 0)),
                  pl.BlockSpec(w2bd.shape, lambda i: (0, 0, 0))],
        out_specs=(pl.BlockSpec((1, n, GRP_W), lambda i: (i, 0, 0)),
                   pl.BlockSpec((1, n, GRP_W), lambda i: (i, 0, 0))),
        compiler_params=_params(1), name="compress_prompt",
    )(rows16, w1bd, b1t, w2bd)


NSA_KEY_CHUNK = 512


def _nsa_prompt_body(q_ref, gate_ref, ck_ref, cv_ref, sel_ref, win_ref, ov_ref, ex_ref, bg_ref,
                     o_ref, mask_s, m_s, l_s, acc_s, ocmp_s, *, tq, t_len):
    i = pl.program_id(1)
    q0 = i * tq
    q = q_ref[...]
    pos = q0 + _iota((tq, 1), 0)
    pos4 = jnp.concatenate([pos] * A_REP, axis=0)
    n_cmp_rows = ck_ref.shape[1]
    n_lane = _iota((1, n_cmp_rows), 1)
    valid_cmp = (n_lane * CMP_STRIDE + (CMP_LEN - 1) <= pos4) & (n_lane < n_cmp_rows - 1)
    qgs = [_stack_heads(q, g).astype(BF16) for g in range(A_KV)]

    imps = []
    for g in range(A_KV):
        gs = slice(g * A_DH, (g + 1) * A_DH)
        p_cmp = _masked_softmax(_nt(qgs[g], ck_ref[0, :, gs].astype(BF16)) * ATTN_SCALE, valid_cmp)
        ocmp_s[g] = _nn(p_cmp.astype(BF16), cv_ref[0, :, gs].astype(BF16))
        p_sum = p_cmp[0:tq]
        for r in range(1, A_REP):
            p_sum = p_sum + p_cmp[r * tq:(r + 1) * tq]
        imps.append(_nn(p_sum, ov_ref[...], HI))
    blk_lane = _iota((A_KV * tq, LANE), 1)
    score = _block_scores(jnp.concatenate(imps, axis=0), pos4, blk_lane)
    few_blocks = q0 + tq <= N_SEL * SEL_BLK

    @pl.when(few_blocks)
    def _():
        mask_s[...] = jnp.where(score == NEG_INF, 0.0, 1.0)

    @pl.when(jnp.logical_not(few_blocks))
    def _():
        mask_s[...] = _topk_mask(score, blk_lane.astype(F32), N_SEL)

    m_s[...] = jnp.full_like(m_s, NEG_INF)
    l_s[...] = jnp.zeros_like(l_s)
    acc_s[...] = jnp.zeros_like(acc_s)
    kc = NSA_KEY_CHUNK
    for c in range(t_len // kc):
        @pl.when(c * kc < q0 + tq)
        def _(c=c):
            key_lane = c * kc + _iota((1, kc), 1)
            causal = key_lane <= pos4
            for g in range(A_KV):
                m_keys = _nn(mask_s[g * tq:(g + 1) * tq, :].astype(BF16),
                             ex_ref[:, c * kc:(c + 1) * kc])
                valid = (jnp.concatenate([m_keys] * A_REP, axis=0) > 0.5) & causal
                k = sel_ref[c * kc:(c + 1) * kc, g * A_DH:(g + 1) * A_DH].astype(BF16)
                v = sel_ref[c * kc:(c + 1) * kc,
                            GRP_W + g * A_DH:GRP_W + (g + 1) * A_DH].astype(BF16)
                _online_softmax_step(m_s, l_s, acc_s, g, _nt(qgs[g], k) * ATTN_SCALE, valid,
                                     lambda e, v=v: _nn(e, v))

    gate = jax.nn.sigmoid(gate_ref[...] + bg_ref[...])
    band = WINDOW + tq
    w_start = pl.multiple_of(jnp.maximum(q0 - WINDOW, 0), tq)
    kpos_w = w_start + _iota((1, band), 1)
    diff_w = pos4 - kpos_w
    valid_w = (diff_w >= 0) & (diff_w <= WINDOW)
    for g in range(A_KV):
        gs = slice(g * A_DH, (g + 1) * A_DH)
        vs_ = slice(GRP_W + g * A_DH, GRP_W + (g + 1) * A_DH)
        kw = win_ref[pl.ds(w_start, band), gs].astype(BF16)
        vw = win_ref[pl.ds(w_start, band), vs_].astype(BF16)
        s_w = jnp.where(valid_w, _nt(qgs[g], kw) * ATTN_SCALE, NEG_INF)
        e_w = jnp.exp(s_w - jnp.max(s_w, axis=1, keepdims=True))
        o_win = _nn(e_w.astype(BF16), vw) / jnp.sum(e_w, axis=1, keepdims=True)
        o_sel = acc_s[g] / jnp.maximum(l_s[g], 1e-30)
        _combine_heads(o_ref, gate, ocmp_s[g], o_sel, o_win, g, tq)


def nsa_prompt_attention(q, gate_pre, ck, cv, sel_rows, win_rows, b_gate_row, bsz, t_len, tq):
    n = q.shape[0]
    n_q = t_len // tq
    n_blk = t_len // SEL_BLK
    assert t_len % NSA_KEY_CHUNK == 0 and n_blk <= LANE
    ov = jnp.pad(_overlap_matrix(ck.shape[1], n_blk, 0), ((0, 0), (0, LANE - n_blk)))
    ex = (np.arange(LANE)[:, None] == (np.arange(t_len)[None, :] // SEL_BLK))
    ex = jnp.asarray(ex, BF16)
    rows4 = A_REP * tq
    body = functools.partial(_nsa_prompt_body, tq=tq, t_len=t_len)
    return pl.pallas_call(
        body,
        out_shape=jax.ShapeDtypeStruct((n, A_QW), F32),
        grid=(bsz, n_q),
        in_specs=[pl.BlockSpec((tq, A_QW), lambda b, i: (b * n_q + i, 0)),
                  pl.BlockSpec((tq, LANE), lambda b, i: (b * n_q + i, 0)),
                  pl.BlockSpec((1,) + ck.shape[1:], lambda b, i: (b, 0, 0)),
                  pl.BlockSpec((1,) + cv.shape[1:], lambda b, i: (b, 0, 0)),
                  pl.BlockSpec((t_len, ROW_W), lambda b, i: (b, 0)),
                  pl.BlockSpec((t_len, ROW_W), lambda b, i: (b, 0)),
                  pl.BlockSpec(ov.shape, lambda b, i: (0, 0)),
                  pl.BlockSpec(ex.shape, lambda b, i: (0, 0)),
                  pl.BlockSpec((1, LANE), lambda b, i: (0, 0))],
        out_specs=pl.BlockSpec((tq, A_QW), lambda b, i: (b * n_q + i, 0)),
        scratch_shapes=[pltpu.VMEM((A_KV * tq, LANE), F32),
                        pltpu.VMEM((A_KV, rows4, 1), F32),
                        pltpu.VMEM((A_KV, rows4, 1), F32),
                        pltpu.VMEM((A_KV, rows4, A_DH), F32),
                        pltpu.VMEM((A_KV, rows4, A_DH), F32)],
        compiler_params=_params(2), name="nsa_prompt_attention",
    )(q, gate_pre, ck, cv, sel_rows, win_rows, ov, ex, b_gate_row)


DEC_ROWS = 8
CMP_PAGES_PER_STEP = 16
SEL_PAGES_PER_STEP = 16
R16_PER_PAGE = PAGE_SIZE // CMP_STRIDE


def _dec_cmp_body(pt_ref, *refs, past_len, n_blk_pad):
    del pt_ref
    pages = refs[:CMP_PAGES_PER_STEP]
    (w1_ref, b1_ref, w2_ref, q_ref, ov_ref, ocmp_ref, msel_ref,
     ck_s, cv_s, carry_s, x_s) = refs[CMP_PAGES_PER_STEP:]
    s = pl.program_id(1)
    rows = CMP_PAGES_PER_STEP * R16_PER_PAGE

    @pl.when(s == 0)
    def _():
        carry_s[...] = jnp.zeros_like(carry_s)

    eye = _iota((PAGE_SIZE, PAGE_SIZE), 0) == _iota((PAGE_SIZE, PAGE_SIZE), 1)
    eye = jnp.where(eye, 1.0, 0.0).astype(BF16)
    n_lt = ROW_W // LANE
    for k, p in enumerate(pages):
        x_t = _nt(eye, p[0].astype(BF16))
        for j in range(n_lt):
            x_s[j, k * PAGE_SIZE:(k + 1) * PAGE_SIZE, :] = x_t[:, j * LANE:(j + 1) * LANE]

    rid = _iota((rows, 1), 0)
    lt_per_c = GRP_W // LANE
    for c, dst in ((0, ck_s), (1, cv_s)):
        first = None
        second = None
        for l in range(CMP_STRIDE):
            x = jnp.concatenate(
                [x_s[c * lt_per_c + j, pl.ds(l, rows, stride=CMP_STRIDE), :] for j in range(lt_per_c)],
                axis=1).astype(BF16)
            f = _nn(x, w1_ref[c, l])
            sc = _nn(x, w1_ref[c, CMP_STRIDE + l])
            first = f if first is None else first + f
            second = sc if second is None else second + sc
        prev = carry_s[c, SUBLANE - 1:SUBLANE, :]
        shifted = jnp.where(rid == 0, prev, pltpu.roll(first, 1, 0))
        hid = _gelu(shifted + second + b1_ref[c])
        dst[pl.ds(pl.multiple_of(s * rows, rows), rows), :] = _nn(hid.astype(BF16), w2_ref[c])
        carry_s[c] = first[rows - SUBLANE:rows, :]

    @pl.when(s == pl.num_programs(1) - 1)
    def _():
        n_rows = ck_s.shape[0]
        q = q_ref[0]
        t_col = _iota((DEC_ROWS, 1), 0)
        pos = past_len + t_col
        pos4 = jnp.concatenate([pos] * A_REP, axis=0)
        m_lane = _iota((1, n_rows), 1)
        valid = (m_lane >= 1) & ((m_lane - 1) * CMP_STRIDE + (CMP_LEN - 1) <= pos4)
        imps = []
        for g in range(A_KV):
            gs = slice(g * A_DH, (g + 1) * A_DH)
            qg = _stack_heads(q, g).astype(BF16)
            p = _masked_softmax(_nt(qg, ck_s[:, gs].astype(BF16)) * ATTN_SCALE, valid)
            o = _nn(p.astype(BF16), cv_s[:, gs].astype(BF16))
            for r in range(A_REP):
                h = g * A_REP + r
                ocmp_ref[0, :, h * A_DH:(h + 1) * A_DH] = o[r * DEC_ROWS:(r + 1) * DEC_ROWS]
            p_sum = p[0:DEC_ROWS]
            for r in range(1, A_REP):
                p_sum = p_sum + p[r * DEC_ROWS:(r + 1) * DEC_ROWS]
            imps.append(_nn(p_sum, ov_ref[...], HI))
        imp = jnp.concatenate(imps, axis=0)
        lane = _iota((A_KV * DEC_ROWS, n_blk_pad), 1)
        score = _block_scores(imp, pos4, lane)
        mask = _topk_mask(score, lane.astype(F32), N_SEL)
        for j in range(n_blk_pad // LANE):
            msel_ref[0, j] = mask[:, j * LANE:(j + 1) * LANE]


def nsa_decode_compress(cache_t, page_table, w1bd, b1t, w2bd, q8, past_len, n_blk_pad):
    dbsz, n_pages = page_table.shape
    assert n_pages % CMP_PAGES_PER_STEP == 0
    n_steps = n_pages // CMP_PAGES_PER_STEP
    n_rows = n_pages * R16_PER_PAGE
    n_tiles = n_blk_pad // LANE
    ov = _overlap_matrix(n_rows, n_blk_pad, 1)

    def page_spec(k):
        return pl.BlockSpec((1, ROW_W, PAGE_SIZE),
                            lambda b, s, pt: (pt[b, s * CMP_PAGES_PER_STEP + k], 0, 0))

    grid_spec = pltpu.PrefetchScalarGridSpec(
        num_scalar_prefetch=1,
        grid=(dbsz, n_steps),
        in_specs=[page_spec(k) for k in range(CMP_PAGES_PER_STEP)] + [
            pl.BlockSpec(w1bd.shape, lambda b, s, pt: (0, 0, 0, 0)),
            pl.BlockSpec(b1t.shape, lambda b, s, pt: (0, 0, 0)),
            pl.BlockSpec(w2bd.shape, lambda b, s, pt: (0, 0, 0)),
            pl.BlockSpec((1, DEC_ROWS, A_QW), lambda b, s, pt: (b, 0, 0)),
            pl.BlockSpec(ov.shape, lambda b, s, pt: (0, 0))],
        out_specs=(pl.BlockSpec((1, DEC_ROWS, A_QW), lambda b, s, pt: (b, 0, 0)),
                   pl.BlockSpec((1, n_tiles, A_KV * DEC_ROWS, LANE), lambda b, s, pt: (b, 0, 0, 0))),
        scratch_shapes=[pltpu.VMEM((n_rows, GRP_W), F32), pltpu.VMEM((n_rows, GRP_W), F32),
                        pltpu.VMEM((2, SUBLANE, GRP_W), F32),
                        pltpu.VMEM((ROW_W // LANE, CMP_PAGES_PER_STEP * PAGE_SIZE, LANE), F32)])
    body = functools.partial(_dec_cmp_body, past_len=past_len, n_blk_pad=n_blk_pad)
    return pl.pallas_call(
        body,
        out_shape=(jax.ShapeDtypeStruct((dbsz, DEC_ROWS, A_QW), F32),
                   jax.ShapeDtypeStruct((dbsz, n_tiles, A_KV * DEC_ROWS, LANE), F32)),
        grid_spec=grid_spec,
        compiler_params=_params(2), name="nsa_decode_compress",
    )(page_table, *([cache_t] * CMP_PAGES_PER_STEP), w1bd, b1t, w2bd, q8, ov)


def _dec_sel_body(pt_ref, *refs, past_len, t_new):
    del pt_ref
    pages = refs[:SEL_PAGES_PER_STEP]
    (q_ref, msel_ref, ocmp_ref, snew_ref, wpast_ref, wnew_ref, gate_ref, bg_ref,
     o_ref, m_s, l_s, acc_s) = refs[SEL_PAGES_PER_STEP:]
    s = pl.program_id(1)
    keys = SEL_PAGES_PER_STEP * PAGE_SIZE
    blks = keys // SEL_BLK
    steps_per_tile = LANE // blks

    @pl.when(s == 0)
    def _():
        m_s[...] = jnp.full_like(m_s, NEG_INF)
        l_s[...] = jnp.zeros_like(l_s)
        acc_s[...] = jnp.zeros_like(acc_s)

    q = q_ref[0]
    tile = s // steps_per_tile
    first_blk = (s - tile * steps_per_tile) * blks
    expand = (_iota((LANE, keys), 0)
              == first_blk + lax.shift_right_arithmetic(_iota((LANE, keys), 1), SEL_BLK_LOG2))
    expand = jnp.where(expand, 1.0, 0.0).astype(BF16)
    m_keys = _nn(msel_ref[0, tile].astype(BF16), expand)

    for g in range(A_KV):
        gs = slice(g * A_DH, (g + 1) * A_DH)
        vs_ = slice(GRP_W + g * A_DH, GRP_W + (g + 1) * A_DH)
        qg = _stack_heads(q, g).astype(BF16)
        k_t = jnp.concatenate([p[0, gs, :] for p in pages], axis=1).astype(BF16)
        v_t = jnp.concatenate([p[0, vs_, :] for p in pages], axis=1).astype(BF16)
        valid = jnp.concatenate([m_keys[g * DEC_ROWS:(g + 1) * DEC_ROWS]] * A_REP, axis=0) > 0.5
        _online_softmax_step(m_s, l_s, acc_s, g, _nn(qg, k_t) * ATTN_SCALE, valid,
                             lambda e, v_t=v_t: _nt(e, v_t))

    @pl.when(s == pl.num_programs(1) - 1)
    def _():
        t_col = _iota((DEC_ROWS, 1), 0)
        t4 = jnp.concatenate([t_col] * A_REP, axis=0)
        t_key = _iota((1, DEC_ROWS), 1)
        valid_new = (t_key <= t4) & (t_key < t_new)
        gate = jax.nn.sigmoid(gate_ref[0] + bg_ref[...])
        n_win = wpast_ref.shape[2]
        kpos_w = past_len - n_win + _iota((1, n_win), 1)
        diff_w = (past_len + t4) - kpos_w
        valid_wp = (diff_w >= 0) & (diff_w <= WINDOW)
        new_blk = past_len // SEL_BLK
        new_tile, new_lane = new_blk // LANE, new_blk % LANE
        for g in range(A_KV):
            gs = slice(g * A_DH, (g + 1) * A_DH)
            vs_ = slice(GRP_W + g * A_DH, GRP_W + (g + 1) * A_DH)
            qg = _stack_heads(q, g).astype(BF16)
            mnew = msel_ref[0, new_tile, g * DEC_ROWS:(g + 1) * DEC_ROWS, new_lane:new_lane + 1]
            mnew4 = jnp.concatenate([mnew] * A_REP, axis=0)
            v_new = snew_ref[0, :, vs_].astype(BF16)
            _online_softmax_step(m_s, l_s, acc_s, g,
                                 _nt(qg, snew_ref[0, :, gs].astype(BF16)) * ATTN_SCALE,
                                 valid_new & (mnew4 > 0.5), lambda e, v_new=v_new: _nn(e, v_new))
            o_sel = acc_s[g] / jnp.maximum(l_s[g], 1e-30)

            s_p = jnp.where(valid_wp, _nn(qg, wpast_ref[0, gs, :].astype(BF16)) * ATTN_SCALE, NEG_INF)
            s_n = jnp.where(valid_new, _nt(qg, wnew_ref[0, :, gs].astype(BF16)) * ATTN_SCALE, NEG_INF)
            mx = jnp.maximum(jnp.max(s_p, axis=1, keepdims=True), jnp.max(s_n, axis=1, keepdims=True))
            mx = jnp.where(mx == NEG_INF, 0.0, mx)
            e_p = jnp.exp(s_p - mx)
            e_n = jnp.exp(s_n - mx)
            den = jnp.maximum(jnp.sum(e_p, axis=1, keepdims=True)
                              + jnp.sum(e_n, axis=1, keepdims=True), 1e-30)
            o_win = (_nt(e_p.astype(BF16), wpast_ref[0, vs_, :].astype(BF16))
                     + _nn(e_n.astype(BF16), wnew_ref[0, :, vs_].astype(BF16))) / den
            o_cmp = _stack_heads(ocmp_ref[0], g)
            _combine_heads(o_ref.at[0], gate, o_cmp, o_sel, o_win, g, DEC_ROWS)


def nsa_decode_attention(cache_t, page_table, q8, msel, ocmp, sel_new, win_past_t, win_new,
                         gate8, b_gate_row, past_len, t_new):
    dbsz, n_pages = page_table.shape
    keys = SEL_PAGES_PER_STEP * PAGE_SIZE
    assert n_pages % SEL_PAGES_PER_STEP == 0 and past_len % SEL_BLK == 0
    assert LANE % (keys // SEL_BLK) == 0
    n_steps = n_pages // SEL_PAGES_PER_STEP
    n_win = win_past_t.shape[2]

    def page_spec(k):
        return pl.BlockSpec((1, ROW_W, PAGE_SIZE),
                            lambda b, s, pt: (pt[b, s * SEL_PAGES_PER_STEP + k], 0, 0))

    per_b = lambda shp: pl.BlockSpec((1,) + shp, lambda b, s, pt: (b,) + (0,) * len(shp))
    grid_spec = pltpu.PrefetchScalarGridSpec(
        num_scalar_prefetch=1,
        grid=(dbsz, n_steps),
        in_specs=[page_spec(k) for k in range(SEL_PAGES_PER_STEP)] + [
            per_b((DEC_ROWS, A_QW)), per_b(msel.shape[1:]), per_b((DEC_ROWS, A_QW)),
            per_b((DEC_ROWS, ROW_W)), per_b((ROW_W, n_win)), per_b((DEC_ROWS, ROW_W)),
            per_b((DEC_ROWS, LANE)), pl.BlockSpec((1, LANE), lambda b, s, pt: (0, 0))],
        out_specs=per_b((DEC_ROWS, A_QW)),
        scratch_shapes=[pltpu.VMEM((A_KV, A_REP * DEC_ROWS, 1), F32),
                        pltpu.VMEM((A_KV, A_REP * DEC_ROWS, 1), F32),
                        pltpu.VMEM((A_KV, A_REP * DEC_ROWS, A_DH), F32)])
    body = functools.partial(_dec_sel_body, past_len=past_len, t_new=t_new)
    return pl.pallas_call(
        body,
        out_shape=jax.ShapeDtypeStruct((dbsz, DEC_ROWS, A_QW), F32),
        grid_spec=grid_spec,
        compiler_params=_params(2), name="nsa_decode_attention",
    )(page_table, *([cache_t] * SEL_PAGES_PER_STEP), q8, msel, ocmp, sel_new, win_past_t,
      win_new, gate8, b_gate_row)


def tokens_minor(rows):
    n, t = rows.shape[:2]
    return jnp.transpose(rows, (0, 2, 3, 4, 1)).reshape(n, ROW_W, t)


PROMPT_TM = 256
PEER_TM = 512
PEER_TE = 1024
NSA_TQ = 128


def kernel(x_prompt, x_sample, cache_cmp_kv, cache_sel_kv, state_win_kv, state_C, state_n, state_m,
           page_table, norm_mix, norm_ffn, norm_final, mlstm_w_in, mlstm_b_i, mlstm_b_f,
           mlstm_head_norm, mlstm_w_out, nsa_w_in, nsa_b_gate, nsa_cmp_w1, nsa_cmp_b1, nsa_cmp_w2,
           nsa_w_out, peer_w_q, peer_sub_keys, peer_u, peer_v):
    bsz, t_len, d = x_prompt.shape
    dbsz, t_s, _ = x_sample.shape
    n_pages = page_table.shape[1]
    past_len = n_pages * PAGE_SIZE
    assert norm_mix.shape[0] == 2 and d == D_MODEL and t_s <= DEC_ROWS // 2
    assert t_len % M_CHUNK == 0 and t_len % NSA_TQ == 0 and t_len >= WINDOW

    xp = x_prompt.reshape(bsz * t_len, d)
    reps = DEC_ROWS // t_s
    xs = jnp.concatenate([x_sample] * reps, axis=1).reshape(dbsz * DEC_ROWS, d)
    n_s = dbsz * DEC_ROWS

    def peer(x, layer, tm):
        keys = peer_sub_keys[layer].astype(BF16)
        return peer_ffn_residual(x, norm_ffn[layer], peer_w_q[layer].astype(BF16),
                                 keys[:, 0], keys[:, 1], peer_u[layer].astype(BF16),
                                 peer_v[layer].T.astype(BF16), tm, PEER_TE)

    w_in = jnp.pad(mlstm_w_in[0], ((0, 0), (0, M_PROJ_PAD - M_PROJ))).astype(BF16)
    w_out = mlstm_w_out[0].astype(BF16)
    gate_bias = jnp.pad(jnp.concatenate([mlstm_b_i[0], mlstm_b_f[0]]),
                        (0, LANE - 2 * M_HEADS)).reshape(1, LANE)
    proj_p = rms_matmul(xp, norm_mix[0], w_in, PROMPT_TM).reshape(bsz, t_len, M_PROJ_PAD)
    proj_s = rms_matmul(xs, norm_mix[0], w_in, n_s).reshape(dbsz, DEC_ROWS, M_PROJ_PAD)
    hid_p, c_p, n_p, m_p = mlstm_scan(
        proj_p, jnp.zeros((bsz, M_HEADS, M_DK, M_DV), F32), jnp.zeros((bsz, M_HEADS, M_DK), F32),
        jnp.zeros((bsz, M_HEADS), F32), gate_bias, mlstm_head_norm[0], M_CHUNK, M_CHUNK - 1)
    hid_s, c_s, n_st, m_st = mlstm_scan(
        proj_s, state_C[0], state_n[0], state_m[0], gate_bias, mlstm_head_norm[0],
        DEC_ROWS, t_s - 1)
    xp = matmul_residual(hid_p.reshape(bsz * t_len, d), w_out, xp, PROMPT_TM)
    xs = matmul_residual(hid_s.reshape(n_s, d), w_out, xs, n_s)
    xp = peer(xp, 0, PEER_TM)
    xs = peer(xs, 0, n_s)

    w_in_a = jnp.pad(nsa_w_in[0], ((0, 0), (0, A_PROJ_PAD - A_PROJ))).astype(BF16)
    w_out_a = nsa_w_out[0].astype(BF16)
    b_gate_row = jnp.pad(nsa_b_gate[0].reshape(-1), (0, LANE - 3 * A_HEADS)).reshape(1, LANE)
    w1bd, b1t, w2bd = _compress_weights(nsa_cmp_w1[0], nsa_cmp_b1[0], nsa_cmp_w2[0])
    tabs_p = rope_tables(jnp.arange(t_len, dtype=jnp.int32))
    pos_s = past_len + jnp.arange(DEC_ROWS, dtype=jnp.int32)
    tabs_s = tuple(jnp.tile(a, (dbsz, 1)) for a in rope_tables(pos_s))

    q_p, cmp_p, sel_p, win_p, gate_p = nsa_project(xp, norm_mix[1], w_in_a, tabs_p, PROMPT_TM)
    q_s, cmp_s, sel_s, win_s, gate_s = nsa_project(xs, norm_mix[1], w_in_a, tabs_s, n_s)

    ck, cv = compress_prompt(cmp_p.reshape(bsz, t_len // CMP_STRIDE, CMP_STRIDE * ROW_W),
                             w1bd, b1t, w2bd)
    o_p = nsa_prompt_attention(q_p, gate_p, ck, cv, sel_p, win_p, b_gate_row, bsz, t_len, NSA_TQ)
    xp = matmul_residual(o_p, w_out_a, xp, PROMPT_TM)

    n_blk = -(-(past_len + t_s) // SEL_BLK)
    n_blk_pad = -(-n_blk // LANE) * LANE
    q8 = q_s.reshape(dbsz, DEC_ROWS, A_QW)
    o_cmp, msel = nsa_decode_compress(tokens_minor(cache_cmp_kv[0]), page_table, w1bd, b1t, w2bd,
                                      q8, past_len, n_blk_pad)
    o_s = nsa_decode_attention(
        tokens_minor(cache_sel_kv[0]), page_table, q8, msel, o_cmp,
        sel_s.reshape(dbsz, DEC_ROWS, ROW_W), tokens_minor(state_win_kv[0]),
        win_s.reshape(dbsz, DEC_ROWS, ROW_W), gate_s.reshape(dbsz, DEC_ROWS, LANE), b_gate_row,
        past_len, t_s)
    xs = matmul_residual(o_s.reshape(n_s, d), w_out_a, xs, n_s)
    xp = peer(xp, 1, PEER_TM)
    xs = peer(xs, 1, n_s)

    y_p = final_norm(xp, norm_final, PROMPT_TM).reshape(bsz, t_len, d)
    y_s = final_norm(xs, norm_final, n_s).reshape(dbsz, DEC_ROWS, d)[:, :t_s]

    row_shape = (2, A_KV, A_DH)
    kv_p = lambda a: a.reshape((1, bsz, t_len) + row_shape)
    kv_s = lambda a: a.reshape((dbsz, DEC_ROWS) + row_shape)[None, :, :t_s]
    win_rows = min(WINDOW, t_len)
    win_buf_p = win_p.reshape((bsz, t_len) + row_shape)[None, :, t_len - win_rows:]
    win_new_s = win_s.reshape((dbsz, DEC_ROWS) + row_shape)[:, :t_s]
    win_buf_s = jnp.concatenate([state_win_kv[0], win_new_s], axis=1)[None, :, t_s:]
    return (y_p, y_s,
            kv_p(cmp_p), kv_p(sel_p), win_buf_p,
            c_p[None], n_p[None], m_p[None],
            kv_s(cmp_s), kv_s(sel_s), win_buf_s,
            c_s[None], n_st[None], m_st[None])
```

```python
import functools

import numpy as np
import jax
import jax.numpy as jnp
from jax import lax
from jax.experimental import pallas as pl
from jax.experimental.pallas import tpu as pltpu

F32 = jnp.float32
BF16 = jnp.bfloat16
HI = lax.Precision.HIGHEST

V7X_VMEM_BYTES = 64 * 1024 * 1024
VMEM_LIMIT = V7X_VMEM_BYTES - 8 * 1024 * 1024
LANE = 128
SUBLANE = 8

D_MODEL = 1024
NORM_EPS = 1e-6

M_HEADS = 8
M_DK = 64
M_DV = 128
M_CHUNK = 64
M_PROJ = 2 * M_HEADS * M_DK + 2 * M_HEADS * M_DV + 2 * M_HEADS
M_PROJ_PAD = 3200

A_HEADS = 16
A_KV = 4
A_REP = 4
A_DH = 64
ROT_DIM = 16
ROPE_THETA = 500000.0
CMP_STRIDE = 16
CMP_LEN = 32
SEL_BLK = 64
SEL_BLK_LOG2 = 6
N_SEL = 16
WINDOW = 512
A_QW = 1024
A_KVW = 1536
A_PROJ = A_QW + A_KVW + 3 * A_HEADS
A_PROJ_PAD = 2688
ATTN_SCALE = A_DH ** -0.5
GRP_W = A_KV * A_DH
ROW_W = 2 * GRP_W
PAGE_SIZE = 128

P_HEADS = 8
P_NKEYS = 128
P_EXPERTS = P_NKEYS * P_NKEYS
P_QDIM = 256
P_TOPK = 16

NEG_INF = float("-inf")
LOG2E = 1.4426950408889634
MASKED = -(2.0 ** 100)


def _params(n_axes):
    return pltpu.CompilerParams(dimension_semantics=("arbitrary",) * n_axes,
                                vmem_limit_bytes=VMEM_LIMIT)


def _nn(a, b, precision=None):
    return jnp.dot(a, b, preferred_element_type=F32, precision=precision)


def _nt(a, b, precision=None):
    return lax.dot_general(a, b, (((1,), (1,)), ((), ())), preferred_element_type=F32,
                           precision=precision)


def _tn(a, b, precision=None):
    return lax.dot_general(a, b, (((0,), (0,)), ((), ())), preferred_element_type=F32,
                           precision=precision)


def _gelu(x):
    return 0.5 * x * (1.0 + lax.erf(x * (2.0 ** -0.5)))


def _rms(x, g):
    return x * lax.rsqrt(jnp.mean(x * x, axis=-1, keepdims=True) + NORM_EPS) * g


def _masked_softmax(s, valid):
    s = jnp.where(valid, s, NEG_INF)
    m = jnp.max(s, axis=-1, keepdims=True)
    m = jnp.where(m == NEG_INF, 0.0, m)
    e = jnp.exp(s - m)
    return e / jnp.maximum(jnp.sum(e, axis=-1, keepdims=True), 1e-30)


def _iota(shape, dim):
    return lax.broadcasted_iota(jnp.int32, shape, dim)


def _rms_mm_body(x_ref, g_ref, w_ref, o_ref):
    o_ref[...] = _nn(_rms(x_ref[...], g_ref[...]).astype(BF16), w_ref[...])


def rms_matmul(x, g, w, tm):
    n, d = x.shape
    nout = w.shape[1]
    return pl.pallas_call(
        _rms_mm_body,
        out_shape=jax.ShapeDtypeStruct((n, nout), F32),
        grid=(n // tm,),
        in_specs=[pl.BlockSpec((tm, d), lambda i: (i, 0)),
                  pl.BlockSpec((1, d), lambda i: (0, 0)),
                  pl.BlockSpec((d, nout), lambda i: (0, 0))],
        out_specs=pl.BlockSpec((tm, nout), lambda i: (i, 0)),
        compiler_params=_params(1), name="rms_matmul",
    )(x, g.reshape(1, d), w)


def _mm_res_body(a_ref, w_ref, x_ref, o_ref):
    o_ref[...] = x_ref[...] + _nn(a_ref[...].astype(BF16), w_ref[...])


def matmul_residual(a, w, x, tm):
    n, k = a.shape
    d = w.shape[1]
    return pl.pallas_call(
        _mm_res_body,
        out_shape=jax.ShapeDtypeStruct((n, d), F32),
        grid=(n // tm,),
        in_specs=[pl.BlockSpec((tm, k), lambda i: (i, 0)),
                  pl.BlockSpec((k, d), lambda i: (0, 0)),
                  pl.BlockSpec((tm, d), lambda i: (i, 0))],
        out_specs=pl.BlockSpec((tm, d), lambda i: (i, 0)),
        compiler_params=_params(1), name="matmul_residual",
    )(a, w, x)


def _final_norm_body(x_ref, g_ref, o_ref):
    o_ref[...] = _rms(x_ref[...], g_ref[...])


def final_norm(x, g, tm):
    n, d = x.shape
    return pl.pallas_call(
        _final_norm_body,
        out_shape=jax.ShapeDtypeStruct((n, d), F32),
        grid=(n // tm,),
        in_specs=[pl.BlockSpec((tm, d), lambda i: (i, 0)),
                  pl.BlockSpec((1, d), lambda i: (0, 0))],
        out_specs=pl.BlockSpec((tm, d), lambda i: (i, 0)),
        compiler_params=_params(1), name="final_norm",
    )(x, g.reshape(1, d))


def _nsa_proj_body(x_ref, g_ref, w_ref, c_ref, sa_ref, sb_ref,
                   q_ref, cmp_ref, sel_ref, win_ref, gate_ref):
    y = _nn(_rms(x_ref[...], g_ref[...]).astype(BF16), w_ref[...])
    c = c_ref[...]
    sa = sa_ref[...]
    sb = sb_ref[...]

    def rot(z):
        return (z * c + pltpu.roll(z, GRP_W - ROT_DIM // 2, 1) * sa
                + pltpu.roll(z, ROT_DIM // 2, 1) * sb)

    for j in range(A_QW // GRP_W):
        q_ref[:, j * GRP_W:(j + 1) * GRP_W] = rot(y[:, j * GRP_W:(j + 1) * GRP_W])
    for ref, base in ((cmp_ref, A_QW), (sel_ref, A_QW + ROW_W), (win_ref, A_QW + 2 * ROW_W)):
        ref[:, 0:GRP_W] = rot(y[:, base:base + GRP_W])
        ref[:, GRP_W:ROW_W] = y[:, base + GRP_W:base + ROW_W]
    gate_ref[...] = y[:, A_QW + A_KVW:A_PROJ_PAD]


def nsa_project(x, g, w, tabs, tm):
    n, d = x.shape
    n_tab = tabs[0].shape[0] // tm
    tab_spec = pl.BlockSpec((tm, GRP_W), lambda i: (i % n_tab, 0))
    row = lambda w_: pl.BlockSpec((tm, w_), lambda i: (i, 0))
    return pl.pallas_call(
        _nsa_proj_body,
        out_shape=(jax.ShapeDtypeStruct((n, A_QW), F32),
                   jax.ShapeDtypeStruct((n, ROW_W), F32),
                   jax.ShapeDtypeStruct((n, ROW_W), F32),
                   jax.ShapeDtypeStruct((n, ROW_W), F32),
                   jax.ShapeDtypeStruct((n, LANE), F32)),
        grid=(n // tm,),
        in_specs=[row(d), pl.BlockSpec((1, d), lambda i: (0, 0)),
                  pl.BlockSpec((d, A_PROJ_PAD), lambda i: (0, 0)),
                  tab_spec, tab_spec, tab_spec],
        out_specs=(row(A_QW), row(ROW_W), row(ROW_W), row(ROW_W), row(LANE)),
        compiler_params=_params(1), name="nsa_project",
    )(x, g.reshape(1, d), w, *tabs)


def rope_tables(pos):
    half = ROT_DIM // 2
    inv_freq = ROPE_THETA ** (-(jnp.arange(half, dtype=F32) * (2.0 / ROT_DIM)))
    ang = pos.astype(F32)[:, None] * inv_freq[None, :]
    cos, sin = jnp.cos(ang), jnp.sin(ang)
    t = pos.shape[0]
    one = jnp.ones((t, A_DH - ROT_DIM), F32)
    zero = jnp.zeros((t, A_DH - ROT_DIM), F32)
    z8 = jnp.zeros((t, half), F32)
    c = jnp.concatenate([cos, cos, one], axis=1)
    sa = jnp.concatenate([-sin, z8, zero], axis=1)
    sb = jnp.concatenate([z8, sin, zero], axis=1)
    return tuple(jnp.tile(a, (1, A_KV)) for a in (c, sa, sb))


def _mlstm_body(q_ref, k_ref, v_ref, og_ref, gt_ref, c0_ref, n0_ref, m0_ref, gb_ref, hn_ref,
                hid_ref, c_out, n_out, m_out, c_s, n_s, m_s, *, chunk, t_last, nb):
    ci = pl.program_id(1)

    @pl.when(ci == 0)
    def _():
        c_s[...] = c0_ref[...]
        n_s[...] = n0_ref[...]
        m_s[...] = m0_ref[...]

    L = chunk
    row = _iota((L, L), 0)
    col = _iota((L, L), 1)
    causal = row >= col
    tril = causal.astype(F32)
    eye8 = (_iota((SUBLANE, LANE), 0) == _iota((SUBLANE, LANE), 1)).astype(F32)
    rcol = _iota((L, 1), 0)
    lane1 = _iota((1, LANE), 1)

    units = [(bi, h) for bi in range(nb) for h in range(M_HEADS)]
    gates, b_all, g_rows = [], [], []
    for bi in range(nb):
        gt = gt_ref[bi] + gb_ref[...]
        log_f = jnp.minimum(gt, 0.0) - jnp.log1p(jnp.exp(-jnp.abs(gt)))
        ba = _nn(tril, log_f, HI)
        g_mat = gt - pltpu.roll(ba, LANE - M_HEADS, 1)
        gates.append(gt)
        b_all.append(ba)
        g_rows.append(_nt(eye8, g_mat, HI))

    qs, ks, vs, cs, ns = {}, {}, {}, {}, {}
    s_raw, q_c = {}, {}
    for u in units:
        bi, h = u
        qs[u] = q_ref[bi, :, h * M_DK:(h + 1) * M_DK]
        ks[u] = k_ref[bi, :, h * M_DK:(h + 1) * M_DK] * (M_DK ** -0.5)
        vs[u] = v_ref[bi, :, h * M_DV:(h + 1) * M_DV]
        cs[u] = c_s[bi, h]
        ns[u] = n_s[bi, h:h + 1, :]
        s_raw[u] = _nt(qs[u].astype(BF16), ks[u].astype(BF16))
        q_c[u] = _nn(qs[u].astype(BF16), cs[u].astype(BF16))

    b_col, a_col, m_t, w_d, w_a = {}, {}, {}, {}, {}
    for u in units:
        bi, h = u
        b_col[u] = b_all[bi][:, M_HEADS + h:M_HEADS + h + 1]
        a_col[u] = b_col[u] + m_s[bi][:, h:h + 1]
        dm = jnp.where(causal, b_col[u] + g_rows[bi][h:h + 1, :], NEG_INF)
        m_t[u] = jnp.maximum(a_col[u], jnp.max(dm, axis=1, keepdims=True))
        w_d[u] = jnp.exp(dm - m_t[u])
        w_a[u] = jnp.exp(a_col[u] - m_t[u])

    for u in units:
        bi, h = u
        s = s_raw[u] * w_d[u]
        num = w_a[u] * q_c[u] + _nn(s.astype(BF16), vs[u].astype(BF16))
        den = (w_a[u] * jnp.sum(qs[u] * ns[u], axis=1, keepdims=True)
               + jnp.sum(s, axis=1, keepdims=True))
        h_out = num / jnp.maximum(jnp.abs(den), jnp.exp(-m_t[u]))
        hid = h_out * lax.rsqrt(jnp.mean(h_out * h_out, axis=1, keepdims=True) + NORM_EPS)
        hid = hid * hn_ref[:, h * M_DV:(h + 1) * M_DV]
        hid_ref[bi, :, h * M_DV:(h + 1) * M_DV] = (
            jax.nn.sigmoid(og_ref[bi, :, h * M_DV:(h + 1) * M_DV]) * hid)

    m_rows = [m_s[bi] for bi in range(nb)]
    for u in units:
        bi, h = u
        m_new = m_t[u][t_last:t_last + 1, :]
        b_last = b_col[u][t_last:t_last + 1, :]
        d_last = jnp.where(rcol <= t_last, b_last - b_col[u] + gates[bi][:, h:h + 1], NEG_INF)
        w_last = jnp.exp(d_last - m_new)
        decay = jnp.exp(a_col[u][t_last:t_last + 1, :] - m_new)
        kw = ks[u] * w_last
        c_s[bi, h] = decay * cs[u] + _tn(kw, vs[u], HI)
        n_s[bi, h:h + 1, :] = decay * ns[u] + jnp.sum(kw, axis=0, keepdims=True)
        m_rows[bi] = jnp.where(lane1 == h, m_new, m_rows[bi])
    for bi in range(nb):
        m_s[bi] = m_rows[bi]

    @pl.when(ci == pl.num_programs(1) - 1)
    def _():
        c_out[...] = c_s[...]
        n_out[...] = n_s[...]
        m_out[...] = m_s[...]


MLSTM_SEQ_PER_STEP = 2


def mlstm_scan(proj, c0, n0, m0, gate_bias, head_norm, chunk, t_last):
    b, t, _ = proj.shape
    nb = MLSTM_SEQ_PER_STEP
    assert b % nb == 0
    n_chunks = t // chunk
    hd = M_HEADS * M_DK
    vd = M_HEADS * M_DV
    m0p = jnp.pad(m0, ((0, 0), (0, LANE - M_HEADS))).reshape(b, 1, LANE)
    body = functools.partial(_mlstm_body, chunk=chunk, t_last=t_last, nb=nb)
    hid, c_t, n_t, m_t = pl.pallas_call(
        body,
        out_shape=(jax.ShapeDtypeStruct((b, t, vd), F32),
                   jax.ShapeDtypeStruct((b, M_HEADS, M_DK, M_DV), F32),
                   jax.ShapeDtypeStruct((b, M_HEADS, M_DK), F32),
                   jax.ShapeDtypeStruct((b, 1, LANE), F32)),
        grid=(b // nb, n_chunks),
        in_specs=[pl.BlockSpec((nb, chunk, hd), lambda i, c: (i, c, 0)),
                  pl.BlockSpec((nb, chunk, hd), lambda i, c: (i, c, 1)),
                  pl.BlockSpec((nb, chunk, vd), lambda i, c: (i, c, 1)),
                  pl.BlockSpec((nb, chunk, vd), lambda i, c: (i, c, 2)),
                  pl.BlockSpec((nb, chunk, LANE), lambda i, c: (i, c, (2 * hd + 2 * vd) // LANE)),
                  pl.BlockSpec((nb, M_HEADS, M_DK, M_DV), lambda i, c: (i, 0, 0, 0)),
                  pl.BlockSpec((nb, M_HEADS, M_DK), lambda i, c: (i, 0, 0)),
                  pl.BlockSpec((nb, 1, LANE), lambda i, c: (i, 0, 0)),
                  pl.BlockSpec((1, LANE), lambda i, c: (0, 0)),
                  pl.BlockSpec((1, vd), lambda i, c: (0, 0))],
        out_specs=(pl.BlockSpec((nb, chunk, vd), lambda i, c: (i, c, 0)),
                   pl.BlockSpec((nb, M_HEADS, M_DK, M_DV), lambda i, c: (i, 0, 0, 0)),
                   pl.BlockSpec((nb, M_HEADS, M_DK), lambda i, c: (i, 0, 0)),
                   pl.BlockSpec((nb, 1, LANE), lambda i, c: (i, 0, 0))),
        scratch_shapes=[pltpu.VMEM((nb, M_HEADS, M_DK, M_DV), F32),
                        pltpu.VMEM((nb, M_HEADS, M_DK), F32),
                        pltpu.VMEM((nb, 1, LANE), F32)],
        compiler_params=_params(2), name="mlstm_scan",
    )(proj, proj, proj, proj, proj, c0, n0, m0p, gate_bias, head_norm.reshape(1, vd))
    return hid, c_t, n_t, m_t[:, 0, :M_HEADS]


N_CAND = (2 + SUBLANE) * SUBLANE
RANK_OUT = float(P_TOPK)


def _top_rows(xs, dsts, k, with_rank):
    xs = list(xs)
    ranks = [jnp.full(x.shape, RANK_OUT, F32) if w else None for x, w in zip(xs, with_rank)]
    for r in range(k):
        for i, dst in enumerate(dsts):
            m = jnp.max(xs[i], axis=1, keepdims=True)
            dst[:, r:r + 1, :] = m
            hit = xs[i] == m
            if ranks[i] is not None:
                ranks[i] = jnp.where(hit, float(r), ranks[i])
            xs[i] = jnp.where(hit, NEG_INF, xs[i])
    return ranks


def _peer_select_body(qp_ref, k1_ref, k2_ref, r2_ref, e2_ref, l_ref, c_ref,
                      s1_s, s2_s, v1_s, v2_s, cand_s, vals_s):
    for h in range(P_HEADS):
        q1 = qp_ref[:, h * P_QDIM:h * P_QDIM + P_QDIM // 2].astype(BF16)
        q2 = qp_ref[:, h * P_QDIM + P_QDIM // 2:(h + 1) * P_QDIM].astype(BF16)
        s1_s[h] = _nt(k1_ref[h], q1)
        s2_s[h] = _nt(k2_ref[h], q2)
    s1 = s1_s[...]
    s2 = s2_s[...]
    _, rank2 = _top_rows((s1, s2), (v1_s, v2_s), P_TOPK, (False, True))
    v2_lo = v2_s[:, 0:SUBLANE, :]
    cand_s[:, 0:SUBLANE, :] = v2_lo + v1_s[:, 0:1, :]
    cand_s[:, SUBLANE:2 * SUBLANE, :] = v2_s[:, SUBLANE:2 * SUBLANE, :] + v1_s[:, 0:1, :]
    cand_s[:, 2 * SUBLANE:3 * SUBLANE, :] = v1_s[:, SUBLANE:2 * SUBLANE, :] + v2_s[:, 0:1, :]
    for r in range(1, SUBLANE):
        cand_s[:, (2 + r) * SUBLANE:(3 + r) * SUBLANE, :] = v2_lo + v1_s[:, r:r + 1, :]
    _top_rows((cand_s[...],), (vals_s,), P_TOPK, (False,))
    top = vals_s[:, 0:1, :]
    tau = vals_s[:, P_TOPK - 1:P_TOPK, :]
    z = jnp.sum(jnp.exp(vals_s[...] - top), axis=1, keepdims=True)
    v1 = v1_s[...]
    kept = jnp.zeros(v1.shape, F32)
    for r in range(P_TOPK):
        kept = kept + jnp.where(v2_s[:, r:r + 1, :] + v1 >= tau, 1.0, 0.0)
    n_keep = jnp.zeros(s1.shape, F32)
    for r in range(P_TOPK):
        n_keep = jnp.where(s1 == v1_s[:, r:r + 1, :], kept[:, r:r + 1, :], n_keep)
    r2_ref[...] = rank2.astype(BF16)
    e2_ref[...] = jnp.exp(s2 - v2_s[:, 0:1, :]).astype(BF16)
    l_ref[...] = n_keep
    c_ref[...] = jnp.exp(s1 - v1_s[:, 0:1, :]) / z


def peer_select(qp, k1, k2, tm):
    n = qp.shape[0]
    spec = pl.BlockSpec((P_HEADS, P_NKEYS, tm), lambda i: (0, 0, i))
    key_spec = pl.BlockSpec((P_HEADS, P_NKEYS, P_QDIM // 2), lambda i: (0, 0, 0))
    shp = lambda dt: jax.ShapeDtypeStruct((P_HEADS, P_NKEYS, n), dt)
    return pl.pallas_call(
        _peer_select_body,
        out_shape=(shp(BF16), shp(BF16), shp(F32), shp(F32)),
        grid=(n // tm,),
        in_specs=[pl.BlockSpec((tm, P_HEADS * P_QDIM), lambda i: (i, 0)), key_spec, key_spec],
        out_specs=(spec, spec, spec, spec),
        scratch_shapes=[pltpu.VMEM((P_HEADS, P_NKEYS, tm), F32), pltpu.VMEM((P_HEADS, P_NKEYS, tm), F32),
                        pltpu.VMEM((P_HEADS, P_TOPK, tm), F32), pltpu.VMEM((P_HEADS, P_TOPK, tm), F32),
                        pltpu.VMEM((P_HEADS, N_CAND, tm), F32), pltpu.VMEM((P_HEADS, P_TOPK, tm), F32)],
        compiler_params=_params(1), name="peer_select",
    )(qp, k1, k2)


def _peer_main_body(x_ref, g_ref, u_ref, vt_prev_ref, vt_last_ref, r2_ref, e2_ref, l_ref, c_ref,
                    o_ref, hn_s, acc_s, p_s, *, a_tile):
    t = pl.program_id(1)
    slot = lax.rem(t, 2)

    @pl.when(t == 0)
    def _():
        hn_s[...] = _rms(x_ref[...], g_ref[...]).astype(BF16)
        acc_s[...] = jnp.zeros_like(acc_s)
        p_s[1] = jnp.zeros(p_s.shape[1:], p_s.dtype)

    scores = _nt(u_ref[...], hn_s[...])
    acc_s[...] += _nn(vt_prev_ref[...], p_s[1 - slot])
    act = _gelu(scores).astype(BF16)
    tm = act.shape[1]
    zero = jnp.zeros((P_NKEYS, tm), BF16)
    parts = []
    for aa in range(a_tile):
        w = None
        for h in range(P_HEADS):
            keep = jnp.broadcast_to(l_ref[h, aa:aa + 1, :], (P_NKEYS, tm)).astype(BF16)
            coef = jnp.broadcast_to(c_ref[h, aa:aa + 1, :], (P_NKEYS, tm)).astype(BF16)
            term = jnp.where(r2_ref[h] < keep, e2_ref[h], zero) * coef
            w = term if w is None else w + term
        parts.append(w * act[aa * P_NKEYS:(aa + 1) * P_NKEYS])
    p_s[slot] = jnp.concatenate(parts, axis=0)

    @pl.when(t == pl.num_programs(1) - 1)
    def _():
        o_ref[...] = x_ref[...] + (acc_s[...] + _nn(vt_last_ref[...], p_s[slot])).T


def peer_main(x, g, u, vt, sel, tm, te):
    n, d = x.shape
    r2, e2, n_keep, coef = sel
    a_tile = te // P_NKEYS
    n_tiles = P_EXPERTS // te
    assert a_tile % SUBLANE == 0
    full_spec = pl.BlockSpec((P_HEADS, P_NKEYS, tm), lambda i, t: (0, 0, i))
    row_spec = pl.BlockSpec((P_HEADS, a_tile, tm), lambda i, t: (0, t, i))
    body = functools.partial(_peer_main_body, a_tile=a_tile)
    return pl.pallas_call(
        body,
        out_shape=jax.ShapeDtypeStruct((n, d), F32),
        grid=(n // tm, n_tiles),
        in_specs=[pl.BlockSpec((tm, d), lambda i, t: (i, 0)),
                  pl.BlockSpec((1, d), lambda i, t: (0, 0)),
                  pl.BlockSpec((te, d), lambda i, t: (t, 0)),
                  pl.BlockSpec((d, te), lambda i, t: (0, jnp.maximum(t - 1, 0))),
                  pl.BlockSpec((d, te), lambda i, t: (0, n_tiles - 1)),
                  full_spec, full_spec, row_spec, row_spec],
        out_specs=pl.BlockSpec((tm, d), lambda i, t: (i, 0)),
        scratch_shapes=[pltpu.VMEM((tm, d), BF16), pltpu.VMEM((d, tm), F32),
                        pltpu.VMEM((2, te, tm), BF16)],
        compiler_params=_params(2), name="peer_main",
    )(x, g.reshape(1, d), u, vt, vt, r2, e2, n_keep, coef)


def peer_ffn_residual(x, g, w_q, k1, k2, u, vt, tm, te):
    qp = rms_matmul(x, g, w_q, min(tm, 256))
    sel = peer_select(qp, k1, k2, tm)
    return peer_main(x, g, u, vt, sel, tm, te)


def _topk_mask(score, lane_f, k):
    sel = jnp.zeros(score.shape, F32)
    for _ in range(k):
        m = jnp.max(score, axis=1, keepdims=True)
        idx = jnp.min(jnp.where(score == m, lane_f, 1e9), axis=1, keepdims=True)
        hit = lane_f == idx
        sel = jnp.where(hit, 1.0, sel)
        score = jnp.where(hit, NEG_INF, score)
    return sel


def _block_scores(imp, pos_col, lane):
    cur = lax.shift_right_arithmetic(pos_col, SEL_BLK_LOG2)
    forced = (lane == 0) | (lane == cur) | (lane == cur - 1)
    return jnp.where(forced, jnp.inf, jnp.where(lane <= cur, imp, NEG_INF))


def _stack_heads(q, g):
    return jnp.concatenate(
        [q[:, (g * A_REP + r) * A_DH:(g * A_REP + r + 1) * A_DH] for r in range(A_REP)], axis=0)


def _combine_heads(o_ref, gate, o_cmp, o_sel, o_win, g, rows):
    for r in range(A_REP):
        h = g * A_REP + r
        rs = slice(r * rows, (r + 1) * rows)
        o_ref[:, h * A_DH:(h + 1) * A_DH] = (gate[:, 3 * h:3 * h + 1] * o_cmp[rs]
                                             + gate[:, 3 * h + 1:3 * h + 2] * o_sel[rs]
                                             + gate[:, 3 * h + 2:3 * h + 3] * o_win[rs])


def _online_softmax_step(m_ref, l_ref, acc_ref, g, sc, valid, pv):
    sc = jnp.where(valid, sc, NEG_INF)
    m_old = m_ref[g]
    m_new = jnp.maximum(m_old, jnp.max(sc, axis=1, keepdims=True))
    m_safe = jnp.where(m_new == NEG_INF, 0.0, m_new)
    alpha = jnp.exp(m_old - m_safe)
    e = jnp.exp(sc - m_safe)
    l_ref[g] = alpha * l_ref[g] + jnp.sum(e, axis=1, keepdims=True)
    acc_ref[g] = alpha * acc_ref[g] + pv(e.astype(BF16))
    m_ref[g] = m_new


def _compress_weights(w1, b1, w2):
    eye = jnp.eye(A_KV, dtype=F32)
    w1bd = jnp.einsum("cldh,gk->clgdkh", w1, eye).reshape(2, CMP_LEN, GRP_W, GRP_W).astype(BF16)
    w1cat = jnp.concatenate([w1bd[:, :CMP_STRIDE], w1bd[:, CMP_STRIDE:]], axis=-1)
    w2bd = jnp.einsum("chd,gk->cghkd", w2, eye).reshape(2, GRP_W, GRP_W).astype(BF16)
    b1t = jnp.tile(b1, (1, A_KV)).reshape(2, 1, GRP_W)
    return w1cat, b1t, w2bd


def _split3(x):
    hi = x.astype(BF16)
    r1 = x - hi.astype(F32)
    mid = r1.astype(BF16)
    lo = (r1 - mid.astype(F32)).astype(BF16)
    return hi, mid, lo


def _block_importance(p_sum, ov_ref):
    hi, mid, lo = _split3(p_sum)
    ov = ov_ref[...]
    return _nn(hi, ov) + _nn(mid, ov) + _nn(lo, ov)


def _overlap_matrix(n_rows, n_cols, shift):
    c_start = (np.arange(n_rows) - shift) * CMP_STRIDE
    s_start = np.arange(n_cols) * SEL_BLK
    ov = np.clip(np.minimum(c_start[:, None] + CMP_LEN, s_start[None, :] + SEL_BLK)
                 - np.maximum(c_start[:, None], s_start[None, :]), 0, None) / CMP_STRIDE
    ov[c_start < 0] = 0.0
    return jnp.asarray(ov, BF16)


def _cmp_prompt_body(r_ref, w1_ref, b1_ref, w2_ref, ck_ref, cv_ref):
    for c, out_ref in ((0, ck_ref), (1, cv_ref)):
        both = None
        for l in range(CMP_STRIDE):
            lo = l * ROW_W + c * GRP_W
            y = _nn(r_ref[0, :, lo:lo + GRP_W].astype(BF16), w1_ref[c, l])
            both = y if both is None else both + y
        first, second = both[:, :GRP_W], both[:, GRP_W:]
        n = first.shape[0]
        hid = _gelu(first + pltpu.roll(second, n - 1, 0) + b1_ref[c])
        out_ref[0] = _nn(hid.astype(BF16), w2_ref[c])


def compress_prompt(rows16, w1bd, b1t, w2bd):
    b, n, w = rows16.shape
    out = jax.ShapeDtypeStruct((b, n, GRP_W), F32)
    return pl.pallas_call(
        _cmp_prompt_body,
        out_shape=(out, out),
        grid=(b,),
        in_specs=[pl.BlockSpec((1, n, w), lambda i: (i, 0, 0)),
                  pl.BlockSpec(w1bd.shape, lambda i: (0, 0, 0, 0)),
                  pl.BlockSpec(b1t.shape, lambda i: (0, 0, 0)),
                  pl.BlockSpec(w2bd.shape, lambda i: (0, 0, 0))],
        out_specs=(pl.BlockSpec((1, n, GRP_W), lambda i: (i, 0, 0)),
                   pl.BlockSpec((1, n, GRP_W), lambda i: (i, 0, 0))),
        compiler_params=_params(1), name="compress_prompt",
    )(rows16, w1bd, b1t, w2bd)


NSA_KEY_CHUNK = 512


def _nsa_prompt_body(q_ref, gate_ref, ck_ref, cv_ref, sel_ref, win_ref, ov_ref, ex_ref, bg_ref,
                     o_ref, mask_s, m_s, l_s, acc_s, ocmp_s, *, tq, t_len):
    i = pl.program_id(1)
    q0 = i * tq
    q = q_ref[...] * (ATTN_SCALE * LOG2E)
    pos = q0 + _iota((tq, 1), 0)
    pos4 = jnp.concatenate([pos] * A_REP, axis=0)
    n_cmp_rows = ck_ref.shape[1]
    n_lane = _iota((1, n_cmp_rows), 1)
    valid_cmp = (n_lane * CMP_STRIDE + (CMP_LEN - 1) <= pos) & (n_lane < n_cmp_rows - 1)
    bias_cmp = jnp.where(valid_cmp, 0.0, NEG_INF)
    qgs = [_stack_heads(q, g).astype(BF16) for g in range(A_KV)]

    def add_bias(s, bias):
        return jnp.concatenate([s[r * tq:(r + 1) * tq] + bias for r in range(A_REP)], axis=0)

    imps = []
    for g in range(A_KV):
        gs = slice(g * A_DH, (g + 1) * A_DH)
        s_c = add_bias(_nt(qgs[g], ck_ref[0, :, gs].astype(BF16)), bias_cmp)
        m_c = jnp.max(s_c, axis=1, keepdims=True)
        e_c = jnp.exp2(s_c - jnp.where(m_c == NEG_INF, 0.0, m_c))
        p_cmp = e_c / jnp.maximum(jnp.sum(e_c, axis=1, keepdims=True), 1e-30)
        ocmp_s[g] = _nn(p_cmp.astype(BF16), cv_ref[0, :, gs].astype(BF16))
        p_sum = p_cmp[0:tq]
        for r in range(1, A_REP):
            p_sum = p_sum + p_cmp[r * tq:(r + 1) * tq]
        imps.append(_block_importance(p_sum, ov_ref))
    blk_lane = _iota((A_KV * tq, LANE), 1)
    score = _block_scores(jnp.concatenate(imps, axis=0), pos4, blk_lane)
    few_blocks = q0 + tq <= N_SEL * SEL_BLK

    @pl.when(few_blocks)
    def _():
        mask_s[...] = jnp.where(score == NEG_INF, 0.0, 1.0)

    @pl.when(jnp.logical_not(few_blocks))
    def _():
        mask_s[...] = _topk_mask(score, blk_lane.astype(F32), N_SEL)

    m_s[...] = jnp.full_like(m_s, NEG_INF)
    l_s[...] = jnp.zeros_like(l_s)
    acc_s[...] = jnp.zeros_like(acc_s)
    kc = NSA_KEY_CHUNK
    for c in range(t_len // kc):
        @pl.when(c * kc < q0 + tq)
        def _(c=c):
            key_lane = c * kc + _iota((1, kc), 1)
            causal_bias = jnp.where(key_lane <= pos, 0.0, MASKED)
            for g in range(A_KV):
                blk_bias = _nn((mask_s[g * tq:(g + 1) * tq, :] - 1.0).astype(BF16),
                               ex_ref[:, c * kc:(c + 1) * kc])
                k = sel_ref[c * kc:(c + 1) * kc, g * A_DH:(g + 1) * A_DH].astype(BF16)
                v = sel_ref[c * kc:(c + 1) * kc,
                            GRP_W + g * A_DH:GRP_W + (g + 1) * A_DH].astype(BF16)
                sc = add_bias(_nt(qgs[g], k), blk_bias + causal_bias)
                m_old = m_s[g]
                m_new = jnp.maximum(m_old, jnp.max(sc, axis=1, keepdims=True))
                alpha = jnp.exp2(m_old - m_new)
                e = jnp.exp2(sc - m_new)
                l_s[g] = alpha * l_s[g] + jnp.sum(e, axis=1, keepdims=True)
                acc_s[g] = alpha * acc_s[g] + _nn(e.astype(BF16), v)
                m_s[g] = m_new

    gate = jax.nn.sigmoid(gate_ref[...] + bg_ref[...])
    band = WINDOW + tq
    w_start = pl.multiple_of(jnp.maximum(q0 - WINDOW, 0), tq)
    kpos_w = w_start + _iota((1, band), 1)
    diff_w = pos - kpos_w
    bias_w = jnp.where((diff_w >= 0) & (diff_w <= WINDOW), 0.0, MASKED)
    for g in range(A_KV):
        gs = slice(g * A_DH, (g + 1) * A_DH)
        vs_ = slice(GRP_W + g * A_DH, GRP_W + (g + 1) * A_DH)
        kw = win_ref[pl.ds(w_start, band), gs].astype(BF16)
        vw = win_ref[pl.ds(w_start, band), vs_].astype(BF16)
        s_w = add_bias(_nt(qgs[g], kw), bias_w)
        e_w = jnp.exp2(s_w - jnp.max(s_w, axis=1, keepdims=True))
        o_win = _nn(e_w.astype(BF16), vw) / jnp.sum(e_w, axis=1, keepdims=True)
        o_sel = acc_s[g] / jnp.maximum(l_s[g], 1e-30)
        _combine_heads(o_ref, gate, ocmp_s[g], o_sel, o_win, g, tq)


def nsa_prompt_attention(q, gate_pre, ck, cv, sel_rows, win_rows, b_gate_row, bsz, t_len, tq):
    n = q.shape[0]
    n_q = t_len // tq
    n_blk = t_len // SEL_BLK
    assert t_len % NSA_KEY_CHUNK == 0 and n_blk <= LANE
    ov = jnp.pad(_overlap_matrix(ck.shape[1], n_blk, 0), ((0, 0), (0, LANE - n_blk)))
    ex = (np.arange(LANE)[:, None] == (np.arange(t_len)[None, :] // SEL_BLK))
    ex = jnp.asarray(ex * -MASKED, BF16)
    rows4 = A_REP * tq
    body = functools.partial(_nsa_prompt_body, tq=tq, t_len=t_len)
    return pl.pallas_call(
        body,
        out_shape=jax.ShapeDtypeStruct((n, A_QW), F32),
        grid=(bsz, n_q),
        in_specs=[pl.BlockSpec((tq, A_QW), lambda b, i: (b * n_q + i, 0)),
                  pl.BlockSpec((tq, LANE), lambda b, i: (b * n_q + i, 0)),
                  pl.BlockSpec((1,) + ck.shape[1:], lambda b, i: (b, 0, 0)),
                  pl.BlockSpec((1,) + cv.shape[1:], lambda b, i: (b, 0, 0)),
                  pl.BlockSpec((t_len, ROW_W), lambda b, i: (b, 0)),
                  pl.BlockSpec((t_len, ROW_W), lambda b, i: (b, 0)),
                  pl.BlockSpec(ov.shape, lambda b, i: (0, 0)),
                  pl.BlockSpec(ex.shape, lambda b, i: (0, 0)),
                  pl.BlockSpec((1, LANE), lambda b, i: (0, 0))],
        out_specs=pl.BlockSpec((tq, A_QW), lambda b, i: (b * n_q + i, 0)),
        scratch_shapes=[pltpu.VMEM((A_KV * tq, LANE), F32),
                        pltpu.VMEM((A_KV, rows4, 1), F32),
                        pltpu.VMEM((A_KV, rows4, 1), F32),
                        pltpu.VMEM((A_KV, rows4, A_DH), F32),
                        pltpu.VMEM((A_KV, rows4, A_DH), F32)],
        compiler_params=_params(2), name="nsa_prompt_attention",
    )(q, gate_pre, ck, cv, sel_rows, win_rows, ov, ex, b_gate_row)


DEC_ROWS = 8
CMP_PAGES_PER_STEP = 32
SEL_PAGES_PER_STEP = 32
R16_PER_PAGE = PAGE_SIZE // CMP_STRIDE


def _dec_cmp_body(pt_ref, *refs, past_len, n_blk_pad):
    del pt_ref
    pages = refs[:CMP_PAGES_PER_STEP]
    (w1_ref, b1_ref, w2_ref, q_ref, ov_ref, ocmp_ref, msel_ref,
     ck_s, cv_s, carry_s, x_s) = refs[CMP_PAGES_PER_STEP:]
    s = pl.program_id(1)
    rows = CMP_PAGES_PER_STEP * R16_PER_PAGE

    @pl.when(s == 0)
    def _():
        carry_s[...] = jnp.zeros_like(carry_s)

    out_row = _iota((PAGE_SIZE, PAGE_SIZE), 0)
    src_tok = (CMP_STRIDE * (out_row & (R16_PER_PAGE - 1))
               + lax.shift_right_logical(out_row, R16_PER_PAGE.bit_length() - 1))
    perm = jnp.where(_iota((PAGE_SIZE, PAGE_SIZE), 1) == src_tok, 1.0, 0.0).astype(BF16)
    for k, p in enumerate(pages):
        x_t = _nt(perm, p[0].astype(BF16))
        for l in range(CMP_STRIDE):
            x_s[l, k * R16_PER_PAGE:(k + 1) * R16_PER_PAGE, :] = (
                x_t[l * R16_PER_PAGE:(l + 1) * R16_PER_PAGE, :])

    rid = _iota((rows, 1), 0)
    for c, dst in ((0, ck_s), (1, cv_s)):
        both = None
        for l in range(CMP_STRIDE):
            x = x_s[l, :, c * GRP_W:(c + 1) * GRP_W].astype(BF16)
            y = _nn(x, w1_ref[c, l])
            both = y if both is None else both + y
        first, second = both[:, :GRP_W], both[:, GRP_W:]
        prev = carry_s[c, SUBLANE - 1:SUBLANE, :]
        shifted = jnp.where(rid == 0, prev, pltpu.roll(first, 1, 0))
        hid = _gelu(shifted + second + b1_ref[c])
        dst[pl.ds(pl.multiple_of(s * rows, rows), rows), :] = _nn(hid.astype(BF16), w2_ref[c])
        carry_s[c] = first[rows - SUBLANE:rows, :]

    @pl.when(s == pl.num_programs(1) - 1)
    def _():
        n_rows = ck_s.shape[0]
        q = q_ref[0]
        t_col = _iota((DEC_ROWS, 1), 0)
        pos = past_len + t_col
        pos4 = jnp.concatenate([pos] * A_REP, axis=0)
        m_lane = _iota((1, n_rows), 1)
        valid = (m_lane >= 1) & ((m_lane - 1) * CMP_STRIDE + (CMP_LEN - 1) <= pos4)
        imps = []
        for g in range(A_KV):
            gs = slice(g * A_DH, (g + 1) * A_DH)
            qg = _stack_heads(q, g).astype(BF16)
            p = _masked_softmax(_nt(qg, ck_s[:, gs].astype(BF16)) * ATTN_SCALE, valid)
            o = _nn(p.astype(BF16), cv_s[:, gs].astype(BF16))
            for r in range(A_REP):
                h = g * A_REP + r
                ocmp_ref[0, :, h * A_DH:(h + 1) * A_DH] = o[r * DEC_ROWS:(r + 1) * DEC_ROWS]
            p_sum = p[0:DEC_ROWS]
            for r in range(1, A_REP):
                p_sum = p_sum + p[r * DEC_ROWS:(r + 1) * DEC_ROWS]
            imps.append(_block_importance(p_sum, ov_ref))
        imp = jnp.concatenate(imps, axis=0)
        lane = _iota((A_KV * DEC_ROWS, n_blk_pad), 1)
        score = _block_scores(imp, pos4, lane)
        mask = _topk_mask(score, lane.astype(F32), N_SEL)
        for j in range(n_blk_pad // LANE):
            msel_ref[0, j] = mask[:, j * LANE:(j + 1) * LANE]


def nsa_decode_compress(cache_t, page_table, w1bd, b1t, w2bd, q8, past_len, n_blk_pad):
    dbsz, n_pages = page_table.shape
    assert n_pages % CMP_PAGES_PER_STEP == 0
    n_steps = n_pages // CMP_PAGES_PER_STEP
    n_rows = n_pages * R16_PER_PAGE
    n_tiles = n_blk_pad // LANE
    ov = _overlap_matrix(n_rows, n_blk_pad, 1)

    def page_spec(k):
        return pl.BlockSpec((1, ROW_W, PAGE_SIZE),
                            lambda b, s, pt: (pt[b, s * CMP_PAGES_PER_STEP + k], 0, 0))

    grid_spec = pltpu.PrefetchScalarGridSpec(
        num_scalar_prefetch=1,
        grid=(dbsz, n_steps),
        in_specs=[page_spec(k) for k in range(CMP_PAGES_PER_STEP)] + [
            pl.BlockSpec(w1bd.shape, lambda b, s, pt: (0, 0, 0, 0)),
            pl.BlockSpec(b1t.shape, lambda b, s, pt: (0, 0, 0)),
            pl.BlockSpec(w2bd.shape, lambda b, s, pt: (0, 0, 0)),
            pl.BlockSpec((1, DEC_ROWS, A_QW), lambda b, s, pt: (b, 0, 0)),
            pl.BlockSpec(ov.shape, lambda b, s, pt: (0, 0))],
        out_specs=(pl.BlockSpec((1, DEC_ROWS, A_QW), lambda b, s, pt: (b, 0, 0)),
                   pl.BlockSpec((1, n_tiles, A_KV * DEC_ROWS, LANE), lambda b, s, pt: (b, 0, 0, 0))),
        scratch_shapes=[pltpu.VMEM((n_rows, GRP_W), F32), pltpu.VMEM((n_rows, GRP_W), F32),
                        pltpu.VMEM((2, SUBLANE, GRP_W), F32),
                        pltpu.VMEM((CMP_STRIDE, CMP_PAGES_PER_STEP * R16_PER_PAGE, ROW_W), F32)])
    body = functools.partial(_dec_cmp_body, past_len=past_len, n_blk_pad=n_blk_pad)
    return pl.pallas_call(
        body,
        out_shape=(jax.ShapeDtypeStruct((dbsz, DEC_ROWS, A_QW), F32),
                   jax.ShapeDtypeStruct((dbsz, n_tiles, A_KV * DEC_ROWS, LANE), F32)),
        grid_spec=grid_spec,
        compiler_params=_params(2), name="nsa_decode_compress",
    )(page_table, *([cache_t] * CMP_PAGES_PER_STEP), w1bd, b1t, w2bd, q8, ov)


def _dec_sel_body(pt_ref, *refs, past_len, t_new):
    del pt_ref
    pages = refs[:SEL_PAGES_PER_STEP]
    (q_ref, msel_ref, ocmp_ref, snew_ref, wpast_ref, wnew_ref, gate_ref, bg_ref,
     o_ref, m_s, l_s, acc_s) = refs[SEL_PAGES_PER_STEP:]
    s = pl.program_id(1)
    keys = SEL_PAGES_PER_STEP * PAGE_SIZE
    blks = keys // SEL_BLK
    steps_per_tile = LANE // blks

    @pl.when(s == 0)
    def _():
        m_s[...] = jnp.full_like(m_s, NEG_INF)
        l_s[...] = jnp.zeros_like(l_s)
        acc_s[...] = jnp.zeros_like(acc_s)

    q = q_ref[0]
    tile = s // steps_per_tile
    first_blk = (s - tile * steps_per_tile) * blks
    expand = (_iota((LANE, keys), 0)
              == first_blk + lax.shift_right_arithmetic(_iota((LANE, keys), 1), SEL_BLK_LOG2))
    expand = jnp.where(expand, 1.0, 0.0).astype(BF16)
    m_keys = _nn(msel_ref[0, tile].astype(BF16), expand)

    for g in range(A_KV):
        gs = slice(g * A_DH, (g + 1) * A_DH)
        vs_ = slice(GRP_W + g * A_DH, GRP_W + (g + 1) * A_DH)
        qg = _stack_heads(q, g).astype(BF16)
        k_t = jnp.concatenate([p[0, gs, :] for p in pages], axis=1).astype(BF16)
        v_t = jnp.concatenate([p[0, vs_, :] for p in pages], axis=1).astype(BF16)
        valid = jnp.concatenate([m_keys[g * DEC_ROWS:(g + 1) * DEC_ROWS]] * A_REP, axis=0) > 0.5
        _online_softmax_step(m_s, l_s, acc_s, g, _nn(qg, k_t) * ATTN_SCALE, valid,
                             lambda e, v_t=v_t: _nt(e, v_t))

    @pl.when(s == pl.num_programs(1) - 1)
    def _():
        t_col = _iota((DEC_ROWS, 1), 0)
        t4 = jnp.concatenate([t_col] * A_REP, axis=0)
        t_key = _iota((1, DEC_ROWS), 1)
        valid_new = (t_key <= t4) & (t_key < t_new)
        gate = jax.nn.sigmoid(gate_ref[0] + bg_ref[...])
        n_win = wpast_ref.shape[2]
        kpos_w = past_len - n_win + _iota((1, n_win), 1)
        diff_w = (past_len + t4) - kpos_w
        valid_wp = (diff_w >= 0) & (diff_w <= WINDOW)
        new_blk = past_len // SEL_BLK
        new_tile, new_lane = new_blk // LANE, new_blk % LANE
        for g in range(A_KV):
            gs = slice(g * A_DH, (g + 1) * A_DH)
            vs_ = slice(GRP_W + g * A_DH, GRP_W + (g + 1) * A_DH)
            qg = _stack_heads(q, g).astype(BF16)
            mnew = msel_ref[0, new_tile, g * DEC_ROWS:(g + 1) * DEC_ROWS, new_lane:new_lane + 1]
            mnew4 = jnp.concatenate([mnew] * A_REP, axis=0)
            v_new = snew_ref[0, :, vs_].astype(BF16)
            _online_softmax_step(m_s, l_s, acc_s, g,
                                 _nt(qg, snew_ref[0, :, gs].astype(BF16)) * ATTN_SCALE,
                                 valid_new & (mnew4 > 0.5), lambda e, v_new=v_new: _nn(e, v_new))
            o_sel = acc_s[g] / jnp.maximum(l_s[g], 1e-30)

            s_p = jnp.where(valid_wp, _nn(qg, wpast_ref[0, gs, :].astype(BF16)) * ATTN_SCALE, NEG_INF)
            s_n = jnp.where(valid_new, _nt(qg, wnew_ref[0, :, gs].astype(BF16)) * ATTN_SCALE, NEG_INF)
            mx = jnp.maximum(jnp.max(s_p, axis=1, keepdims=True), jnp.max(s_n, axis=1, keepdims=True))
            mx = jnp.where(mx == NEG_INF, 0.0, mx)
            e_p = jnp.exp(s_p - mx)
            e_n = jnp.exp(s_n - mx)
            den = jnp.maximum(jnp.sum(e_p, axis=1, keepdims=True)
                              + jnp.sum(e_n, axis=1, keepdims=True), 1e-30)
            o_win = (_nt(e_p.astype(BF16), wpast_ref[0, vs_, :].astype(BF16))
                     + _nn(e_n.astype(BF16), wnew_ref[0, :, vs_].astype(BF16))) / den
            o_cmp = _stack_heads(ocmp_ref[0], g)
            _combine_heads(o_ref.at[0], gate, o_cmp, o_sel, o_win, g, DEC_ROWS)


def nsa_decode_attention(cache_t, page_table, q8, msel, ocmp, sel_new, win_past_t, win_new,
                         gate8, b_gate_row, past_len, t_new):
    dbsz, n_pages = page_table.shape
    keys = SEL_PAGES_PER_STEP * PAGE_SIZE
    assert n_pages % SEL_PAGES_PER_STEP == 0 and past_len % SEL_BLK == 0
    assert LANE % (keys // SEL_BLK) == 0
    n_steps = n_pages // SEL_PAGES_PER_STEP
    n_win = win_past_t.shape[2]

    def page_spec(k):
        return pl.BlockSpec((1, ROW_W, PAGE_SIZE),
                            lambda b, s, pt: (pt[b, s * SEL_PAGES_PER_STEP + k], 0, 0))

    per_b = lambda shp: pl.BlockSpec((1,) + shp, lambda b, s, pt: (b,) + (0,) * len(shp))
    grid_spec = pltpu.PrefetchScalarGridSpec(
        num_scalar_prefetch=1,
        grid=(dbsz, n_steps),
        in_specs=[page_spec(k) for k in range(SEL_PAGES_PER_STEP)] + [
            per_b((DEC_ROWS, A_QW)), per_b(msel.shape[1:]), per_b((DEC_ROWS, A_QW)),
            per_b((DEC_ROWS, ROW_W)), per_b((ROW_W, n_win)), per_b((DEC_ROWS, ROW_W)),
            per_b((DEC_ROWS, LANE)), pl.BlockSpec((1, LANE), lambda b, s, pt: (0, 0))],
        out_specs=per_b((DEC_ROWS, A_QW)),
        scratch_shapes=[pltpu.VMEM((A_KV, A_REP * DEC_ROWS, 1), F32),
                        pltpu.VMEM((A_KV, A_REP * DEC_ROWS, 1), F32),
                        pltpu.VMEM((A_KV, A_REP * DEC_ROWS, A_DH), F32)])
    body = functools.partial(_dec_sel_body, past_len=past_len, t_new=t_new)
    return pl.pallas_call(
        body,
        out_shape=jax.ShapeDtypeStruct((dbsz, DEC_ROWS, A_QW), F32),
        grid_spec=grid_spec,
        compiler_params=_params(2), name="nsa_decode_attention",
    )(page_table, *([cache_t] * SEL_PAGES_PER_STEP), q8, msel, ocmp, sel_new, win_past_t,
      win_new, gate8, b_gate_row)


def tokens_minor(rows):
    n, t = rows.shape[:2]
    return jnp.transpose(rows, (0, 2, 3, 4, 1)).reshape(n, ROW_W, t)


PROMPT_TM = 256
PEER_TM = 512
PEER_TE = 1024
NSA_TQ = 128


def kernel(x_prompt, x_sample, cache_cmp_kv, cache_sel_kv, state_win_kv, state_C, state_n, state_m,
           page_table, norm_mix, norm_ffn, norm_final, mlstm_w_in, mlstm_b_i, mlstm_b_f,
           mlstm_head_norm, mlstm_w_out, nsa_w_in, nsa_b_gate, nsa_cmp_w1, nsa_cmp_b1, nsa_cmp_w2,
           nsa_w_out, peer_w_q, peer_sub_keys, peer_u, peer_v):
    bsz, t_len, d = x_prompt.shape
    dbsz, t_s, _ = x_sample.shape
    n_pages = page_table.shape[1]
    past_len = n_pages * PAGE_SIZE
    assert norm_mix.shape[0] == 2 and d == D_MODEL and t_s <= DEC_ROWS // 2
    assert t_len % M_CHUNK == 0 and t_len % NSA_TQ == 0 and t_len >= WINDOW

    xp = x_prompt.reshape(bsz * t_len, d)
    reps = DEC_ROWS // t_s
    xs = jnp.concatenate([x_sample] * reps, axis=1).reshape(dbsz * DEC_ROWS, d)
    n_s = dbsz * DEC_ROWS

    def peer(x, layer, tm):
        keys = peer_sub_keys[layer].astype(BF16)
        return peer_ffn_residual(x, norm_ffn[layer], peer_w_q[layer].astype(BF16),
                                 keys[:, 0], keys[:, 1], peer_u[layer].astype(BF16),
                                 peer_v[layer].T.astype(BF16), tm, PEER_TE)

    w_in = jnp.pad(mlstm_w_in[0], ((0, 0), (0, M_PROJ_PAD - M_PROJ))).astype(BF16)
    w_out = mlstm_w_out[0].astype(BF16)
    gate_bias = jnp.pad(jnp.concatenate([mlstm_b_i[0], mlstm_b_f[0]]),
                        (0, LANE - 2 * M_HEADS)).reshape(1, LANE)
    proj_p = rms_matmul(xp, norm_mix[0], w_in, PROMPT_TM).reshape(bsz, t_len, M_PROJ_PAD)
    proj_s = rms_matmul(xs, norm_mix[0], w_in, n_s).reshape(dbsz, DEC_ROWS, M_PROJ_PAD)
    hid_p, c_p, n_p, m_p = mlstm_scan(
        proj_p, jnp.zeros((bsz, M_HEADS, M_DK, M_DV), F32), jnp.zeros((bsz, M_HEADS, M_DK), F32),
        jnp.zeros((bsz, M_HEADS), F32), gate_bias, mlstm_head_norm[0], M_CHUNK, M_CHUNK - 1)
    hid_s, c_s, n_st, m_st = mlstm_scan(
        proj_s, state_C[0], state_n[0], state_m[0], gate_bias, mlstm_head_norm[0],
        DEC_ROWS, t_s - 1)
    xp = matmul_residual(hid_p.reshape(bsz * t_len, d), w_out, xp, PROMPT_TM)
    xs = matmul_residual(hid_s.reshape(n_s, d), w_out, xs, n_s)
    xp = peer(xp, 0, PEER_TM)
    xs = peer(xs, 0, n_s)

    w_in_a = jnp.pad(nsa_w_in[0], ((0, 0), (0, A_PROJ_PAD - A_PROJ))).astype(BF16)
    w_out_a = nsa_w_out[0].astype(BF16)
    b_gate_row = jnp.pad(nsa_b_gate[0].reshape(-1), (0, LANE - 3 * A_HEADS)).reshape(1, LANE)
    w1bd, b1t, w2bd = _compress_weights(nsa_cmp_w1[0], nsa_cmp_b1[0], nsa_cmp_w2[0])
    tabs_p = rope_tables(jnp.arange(t_len, dtype=jnp.int32))
    pos_s = past_len + jnp.arange(DEC_ROWS, dtype=jnp.int32)
    tabs_s = tuple(jnp.tile(a, (dbsz, 1)) for a in rope_tables(pos_s))

    q_p, cmp_p, sel_p, win_p, gate_p = nsa_project(xp, norm_mix[1], w_in_a, tabs_p, PROMPT_TM)
    q_s, cmp_s, sel_s, win_s, gate_s = nsa_project(xs, norm_mix[1], w_in_a, tabs_s, n_s)

    ck, cv = compress_prompt(cmp_p.reshape(bsz, t_len // CMP_STRIDE, CMP_STRIDE * ROW_W),
                             w1bd, b1t, w2bd)
    o_p = nsa_prompt_attention(q_p, gate_p, ck, cv, sel_p, win_p, b_gate_row, bsz, t_len, NSA_TQ)
    xp = matmul_residual(o_p, w_out_a, xp, PROMPT_TM)

    n_blk = -(-(past_len + t_s) // SEL_BLK)
    n_blk_pad = -(-n_blk // LANE) * LANE
    q8 = q_s.reshape(dbsz, DEC_ROWS, A_QW)
    o_cmp, msel = nsa_decode_compress(tokens_minor(cache_cmp_kv[0]), page_table, w1bd, b1t, w2bd,
                                      q8, past_len, n_blk_pad)
    o_s = nsa_decode_attention(
        tokens_minor(cache_sel_kv[0]), page_table, q8, msel, o_cmp,
        sel_s.reshape(dbsz, DEC_ROWS, ROW_W), tokens_minor(state_win_kv[0]),
        win_s.reshape(dbsz, DEC_ROWS, ROW_W), gate_s.reshape(dbsz, DEC_ROWS, LANE), b_gate_row,
        past_len, t_s)
    xs = matmul_residual(o_s.reshape(n_s, d), w_out_a, xs, n_s)
    xp = peer(xp, 1, PEER_TM)
    xs = peer(xs, 1, n_s)

    y_p = final_norm(xp, norm_final, PROMPT_TM).reshape(bsz, t_len, d)
    y_s = final_norm(xs, norm_final, n_s).reshape(dbsz, DEC_ROWS, d)[:, :t_s]

    row_shape = (2, A_KV, A_DH)
    kv_p = lambda a: a.reshape((1, bsz, t_len) + row_shape)
    kv_s = lambda a: a.reshape((dbsz, DEC_ROWS) + row_shape)[None, :, :t_s]
    win_rows = min(WINDOW, t_len)
    win_buf_p = win_p.reshape((bsz, t_len) + row_shape)[None, :, t_len - win_rows:]
    win_new_s = win_s.reshape((dbsz, DEC_ROWS) + row_shape)[:, :t_s]
    win_buf_s = jnp.concatenate([state_win_kv[0], win_new_s], axis=1)[None, :, t_s:]
    return (y_p, y_s,
            kv_p(cmp_p), kv_p(sel_p), win_buf_p,
            c_p[None], n_p[None], m_p[None],
            kv_s(cmp_s), kv_s(sel_s), win_buf_s,
            c_s[None], n_st[None], m_st[None])
```

```python
import functools

import numpy as np
import jax
import jax.numpy as jnp
from jax import lax
from jax.experimental import pallas as pl
from jax.experimental.pallas import tpu as pltpu

F32 = jnp.float32
BF16 = jnp.bfloat16
HI = lax.Precision.HIGHEST

V7X_VMEM_BYTES = 64 * 1024 * 1024
VMEM_LIMIT = V7X_VMEM_BYTES - 8 * 1024 * 1024
LANE = 128
SUBLANE = 8

D_MODEL = 1024
NORM_EPS = 1e-6

M_HEADS = 8
M_DK = 64
M_DV = 128
M_CHUNK = 64
M_PROJ = 2 * M_HEADS * M_DK + 2 * M_HEADS * M_DV + 2 * M_HEADS
M_PROJ_PAD = 3200

A_HEADS = 16
A_KV = 4
A_REP = 4
A_DH = 64
ROT_DIM = 16
ROPE_THETA = 500000.0
CMP_STRIDE = 16
CMP_LEN = 32
SEL_BLK = 64
SEL_BLK_LOG2 = 6
N_SEL = 16
WINDOW = 512
A_QW = 1024
A_KVW = 1536
A_PROJ = A_QW + A_KVW + 3 * A_HEADS
A_PROJ_PAD = 2688
ATTN_SCALE = A_DH ** -0.5
GRP_W = A_KV * A_DH
ROW_W = 2 * GRP_W
PAGE_SIZE = 128

P_HEADS = 8
P_NKEYS = 128
P_EXPERTS = P_NKEYS * P_NKEYS
P_QDIM = 256
P_TOPK = 16

NEG_INF = float("-inf")
LOG2E = 1.4426950408889634
MASKED = -(2.0 ** 100)


def _params(n_axes):
    return pltpu.CompilerParams(dimension_semantics=("arbitrary",) * n_axes,
                                vmem_limit_bytes=VMEM_LIMIT)


def _nn(a, b, precision=None):
    return jnp.dot(a, b, preferred_element_type=F32, precision=precision)


def _nt(a, b, precision=None):
    return lax.dot_general(a, b, (((1,), (1,)), ((), ())), preferred_element_type=F32,
                           precision=precision)


def _tn(a, b, precision=None):
    return lax.dot_general(a, b, (((0,), (0,)), ((), ())), preferred_element_type=F32,
                           precision=precision)


def _gelu(x):
    return 0.5 * x * (1.0 + lax.erf(x * (2.0 ** -0.5)))


def _rms(x, g):
    return x * lax.rsqrt(jnp.mean(x * x, axis=-1, keepdims=True) + NORM_EPS) * g


def _masked_softmax(s, valid):
    s = jnp.where(valid, s, NEG_INF)
    m = jnp.max(s, axis=-1, keepdims=True)
    m = jnp.where(m == NEG_INF, 0.0, m)
    e = jnp.exp(s - m)
    return e / jnp.maximum(jnp.sum(e, axis=-1, keepdims=True), 1e-30)


def _iota(shape, dim):
    return lax.broadcasted_iota(jnp.int32, shape, dim)


def _rms_mm_body(x_ref, g_ref, w_ref, o_ref):
    o_ref[...] = _nn(_rms(x_ref[...], g_ref[...]).astype(BF16), w_ref[...])


def rms_matmul(x, g, w, tm):
    n, d = x.shape
    nout = w.shape[1]
    return pl.pallas_call(
        _rms_mm_body,
        out_shape=jax.ShapeDtypeStruct((n, nout), F32),
        grid=(n // tm,),
        in_specs=[pl.BlockSpec((tm, d), lambda i: (i, 0)),
                  pl.BlockSpec((1, d), lambda i: (0, 0)),
                  pl.BlockSpec((d, nout), lambda i: (0, 0))],
        out_specs=pl.BlockSpec((tm, nout), lambda i: (i, 0)),
        compiler_params=_params(1), name="rms_matmul",
    )(x, g.reshape(1, d), w)


def _mm_res_body(a_ref, w_ref, x_ref, o_ref):
    o_ref[...] = x_ref[...] + _nn(a_ref[...].astype(BF16), w_ref[...])


def matmul_residual(a, w, x, tm):
    n, k = a.shape
    d = w.shape[1]
    return pl.pallas_call(
        _mm_res_body,
        out_shape=jax.ShapeDtypeStruct((n, d), F32),
        grid=(n // tm,),
        in_specs=[pl.BlockSpec((tm, k), lambda i: (i, 0)),
                  pl.BlockSpec((k, d), lambda i: (0, 0)),
                  pl.BlockSpec((tm, d), lambda i: (i, 0))],
        out_specs=pl.BlockSpec((tm, d), lambda i: (i, 0)),
        compiler_params=_params(1), name="matmul_residual",
    )(a, w, x)


def _nsa_proj_body(x_ref, g_ref, w_ref, c_ref, sa_ref, sb_ref,
                   q_ref, cmp_ref, sel_ref, win_ref, gate_ref):
    y = _nn(_rms(x_ref[...], g_ref[...]).astype(BF16), w_ref[...])
    c = c_ref[...]
    sa = sa_ref[...]
    sb = sb_ref[...]

    def rot(z):
        return (z * c + pltpu.roll(z, GRP_W - ROT_DIM // 2, 1) * sa
                + pltpu.roll(z, ROT_DIM // 2, 1) * sb)

    for j in range(A_QW // GRP_W):
        q_ref[:, j * GRP_W:(j + 1) * GRP_W] = rot(y[:, j * GRP_W:(j + 1) * GRP_W])
    for ref, base in ((cmp_ref, A_QW), (sel_ref, A_QW + ROW_W), (win_ref, A_QW + 2 * ROW_W)):
        ref[:, 0:GRP_W] = rot(y[:, base:base + GRP_W])
        ref[:, GRP_W:ROW_W] = y[:, base + GRP_W:base + ROW_W]
    gate_ref[...] = y[:, A_QW + A_KVW:A_PROJ_PAD]


def nsa_project(x, g, w, tabs, tm):
    n, d = x.shape
    n_tab = tabs[0].shape[0] // tm
    tab_spec = pl.BlockSpec((tm, GRP_W), lambda i: (i % n_tab, 0))
    row = lambda w_: pl.BlockSpec((tm, w_), lambda i: (i, 0))
    return pl.pallas_call(
        _nsa_proj_body,
        out_shape=(jax.ShapeDtypeStruct((n, A_QW), F32),
                   jax.ShapeDtypeStruct((n, ROW_W), F32),
                   jax.ShapeDtypeStruct((n, ROW_W), F32),
                   jax.ShapeDtypeStruct((n, ROW_W), F32),
                   jax.ShapeDtypeStruct((n, LANE), F32)),
        grid=(n // tm,),
        in_specs=[row(d), pl.BlockSpec((1, d), lambda i: (0, 0)),
                  pl.BlockSpec((d, A_PROJ_PAD), lambda i: (0, 0)),
                  tab_spec, tab_spec, tab_spec],
        out_specs=(row(A_QW), row(ROW_W), row(ROW_W), row(ROW_W), row(LANE)),
        compiler_params=_params(1), name="nsa_project",
    )(x, g.reshape(1, d), w, *tabs)


def rope_tables(pos):
    half = ROT_DIM // 2
    inv_freq = ROPE_THETA ** (-(jnp.arange(half, dtype=F32) * (2.0 / ROT_DIM)))
    ang = pos.astype(F32)[:, None] * inv_freq[None, :]
    cos, sin = jnp.cos(ang), jnp.sin(ang)
    t = pos.shape[0]
    one = jnp.ones((t, A_DH - ROT_DIM), F32)
    zero = jnp.zeros((t, A_DH - ROT_DIM), F32)
    z8 = jnp.zeros((t, half), F32)
    c = jnp.concatenate([cos, cos, one], axis=1)
    sa = jnp.concatenate([-sin, z8, zero], axis=1)
    sb = jnp.concatenate([z8, sin, zero], axis=1)
    return tuple(jnp.tile(a, (1, A_KV)) for a in (c, sa, sb))


def _mlstm_body(q_ref, k_ref, v_ref, og_ref, gt_ref, c0_ref, n0_ref, m0_ref, gb_ref, hn_ref,
                hid_ref, c_out, n_out, m_out, c_s, n_s, m_s, *, chunk, t_last, nb):
    ci = pl.program_id(1)

    @pl.when(ci == 0)
    def _():
        c_s[...] = c0_ref[...]
        n_s[...] = n0_ref[...]
        m_s[...] = m0_ref[...]

    L = chunk
    row = _iota((L, L), 0)
    col = _iota((L, L), 1)
    causal = row >= col
    tril = causal.astype(F32)
    eye8 = (_iota((SUBLANE, LANE), 0) == _iota((SUBLANE, LANE), 1)).astype(F32)
    rcol = _iota((L, 1), 0)
    lane1 = _iota((1, LANE), 1)

    units = [(bi, h) for bi in range(nb) for h in range(M_HEADS)]
    gates, b_all, g_rows = [], [], []
    for bi in range(nb):
        gt = gt_ref[bi] + gb_ref[...]
        log_f = jnp.minimum(gt, 0.0) - jnp.log1p(jnp.exp(-jnp.abs(gt)))
        ba = _nn(tril, log_f, HI)
        g_mat = gt - pltpu.roll(ba, LANE - M_HEADS, 1)
        gates.append(gt)
        b_all.append(ba)
        g_rows.append(_nt(eye8, g_mat, HI))

    qs, ks, vs, cs, ns = {}, {}, {}, {}, {}
    s_raw, q_c = {}, {}
    for u in units:
        bi, h = u
        qs[u] = q_ref[bi, :, h * M_DK:(h + 1) * M_DK]
        ks[u] = k_ref[bi, :, h * M_DK:(h + 1) * M_DK] * (M_DK ** -0.5)
        vs[u] = v_ref[bi, :, h * M_DV:(h + 1) * M_DV]
        cs[u] = c_s[bi, h]
        ns[u] = n_s[bi, h:h + 1, :]
        s_raw[u] = _nt(qs[u].astype(BF16), ks[u].astype(BF16))
        q_c[u] = _nn(qs[u].astype(BF16), cs[u].astype(BF16))

    b_col, a_col, m_t, w_d, w_a = {}, {}, {}, {}, {}
    for u in units:
        bi, h = u
        b_col[u] = b_all[bi][:, M_HEADS + h:M_HEADS + h + 1]
        a_col[u] = b_col[u] + m_s[bi][:, h:h + 1]
        dm = jnp.where(causal, b_col[u] + g_rows[bi][h:h + 1, :], NEG_INF)
        m_t[u] = jnp.maximum(a_col[u], jnp.max(dm, axis=1, keepdims=True))
        w_d[u] = jnp.exp(dm - m_t[u])
        w_a[u] = jnp.exp(a_col[u] - m_t[u])

    for u in units:
        bi, h = u
        s = s_raw[u] * w_d[u]
        num = w_a[u] * q_c[u] + _nn(s.astype(BF16), vs[u].astype(BF16))
        den = (w_a[u] * jnp.sum(qs[u] * ns[u], axis=1, keepdims=True)
               + jnp.sum(s, axis=1, keepdims=True))
        h_out = num / jnp.maximum(jnp.abs(den), jnp.exp(-m_t[u]))
        hid = h_out * lax.rsqrt(jnp.mean(h_out * h_out, axis=1, keepdims=True) + NORM_EPS)
        hid = hid * hn_ref[:, h * M_DV:(h + 1) * M_DV]
        hid_ref[bi, :, h * M_DV:(h + 1) * M_DV] = (
            jax.nn.sigmoid(og_ref[bi, :, h * M_DV:(h + 1) * M_DV]) * hid)

    m_rows = [m_s[bi] for bi in range(nb)]
    for u in units:
        bi, h = u
        m_new = m_t[u][t_last:t_last + 1, :]
        b_last = b_col[u][t_last:t_last + 1, :]
        d_last = jnp.where(rcol <= t_last, b_last - b_col[u] + gates[bi][:, h:h + 1], NEG_INF)
        w_last = jnp.exp(d_last - m_new)
        decay = jnp.exp(a_col[u][t_last:t_last + 1, :] - m_new)
        kw = ks[u] * w_last
        c_s[bi, h] = decay * cs[u] + _tn(kw, vs[u], HI)
        n_s[bi, h:h + 1, :] = decay * ns[u] + jnp.sum(kw, axis=0, keepdims=True)
        m_rows[bi] = jnp.where(lane1 == h, m_new, m_rows[bi])
    for bi in range(nb):
        m_s[bi] = m_rows[bi]

    @pl.when(ci == pl.num_programs(1) - 1)
    def _():
        c_out[...] = c_s[...]
        n_out[...] = n_s[...]
        m_out[...] = m_s[...]


MLSTM_SEQ_PER_STEP = 2


def mlstm_scan(proj, c0, n0, m0, gate_bias, head_norm, chunk, t_last):
    b, t, _ = proj.shape
    nb = MLSTM_SEQ_PER_STEP
    assert b % nb == 0
    n_chunks = t // chunk
    hd = M_HEADS * M_DK
    vd = M_HEADS * M_DV
    m0p = jnp.pad(m0, ((0, 0), (0, LANE - M_HEADS))).reshape(b, 1, LANE)
    body = functools.partial(_mlstm_body, chunk=chunk, t_last=t_last, nb=nb)
    hid, c_t, n_t, m_t = pl.pallas_call(
        body,
        out_shape=(jax.ShapeDtypeStruct((b, t, vd), F32),
                   jax.ShapeDtypeStruct((b, M_HEADS, M_DK, M_DV), F32),
                   jax.ShapeDtypeStruct((b, M_HEADS, M_DK), F32),
                   jax.ShapeDtypeStruct((b, 1, LANE), F32)),
        grid=(b // nb, n_chunks),
        in_specs=[pl.BlockSpec((nb, chunk, hd), lambda i, c: (i, c, 0)),
                  pl.BlockSpec((nb, chunk, hd), lambda i, c: (i, c, 1)),
                  pl.BlockSpec((nb, chunk, vd), lambda i, c: (i, c, 1)),
                  pl.BlockSpec((nb, chunk, vd), lambda i, c: (i, c, 2)),
                  pl.BlockSpec((nb, chunk, LANE), lambda i, c: (i, c, (2 * hd + 2 * vd) // LANE)),
                  pl.BlockSpec((nb, M_HEADS, M_DK, M_DV), lambda i, c: (i, 0, 0, 0)),
                  pl.BlockSpec((nb, M_HEADS, M_DK), lambda i, c: (i, 0, 0)),
                  pl.BlockSpec((nb, 1, LANE), lambda i, c: (i, 0, 0)),
                  pl.BlockSpec((1, LANE), lambda i, c: (0, 0)),
                  pl.BlockSpec((1, vd), lambda i, c: (0, 0))],
        out_specs=(pl.BlockSpec((nb, chunk, vd), lambda i, c: (i, c, 0)),
                   pl.BlockSpec((nb, M_HEADS, M_DK, M_DV), lambda i, c: (i, 0, 0, 0)),
                   pl.BlockSpec((nb, M_HEADS, M_DK), lambda i, c: (i, 0, 0)),
                   pl.BlockSpec((nb, 1, LANE), lambda i, c: (i, 0, 0))),
        scratch_shapes=[pltpu.VMEM((nb, M_HEADS, M_DK, M_DV), F32),
                        pltpu.VMEM((nb, M_HEADS, M_DK), F32),
                        pltpu.VMEM((nb, 1, LANE), F32)],
        compiler_params=_params(2), name="mlstm_scan",
    )(proj, proj, proj, proj, proj, c0, n0, m0p, gate_bias, head_norm.reshape(1, vd))
    return hid, c_t, n_t, m_t[:, 0, :M_HEADS]


N_CAND = (2 + SUBLANE) * SUBLANE
RANK_OUT = float(P_TOPK)


def _top_rows(xs, dsts, k, with_rank):
    xs = list(xs)
    ranks = [jnp.full(x.shape, RANK_OUT, F32) if w else None for x, w in zip(xs, with_rank)]
    for r in range(k):
        for i, dst in enumerate(dsts):
            m = jnp.max(xs[i], axis=1, keepdims=True)
            dst[:, r:r + 1, :] = m
            hit = xs[i] == m
            if ranks[i] is not None:
                ranks[i] = jnp.where(hit, float(r), ranks[i])
            xs[i] = jnp.where(hit, NEG_INF, xs[i])
    return ranks


def _peer_select_body(x_ref, g_ref, wq_ref, k1_ref, k2_ref, r2_ref, e2_ref, l_ref, c_ref,
                      s1_s, s2_s, v1_s, v2_s, cand_s, vals_s):
    qp = _nn(_rms(x_ref[...], g_ref[...]).astype(BF16), wq_ref[...])
    for h in range(P_HEADS):
        q1 = qp[:, h * P_QDIM:h * P_QDIM + P_QDIM // 2].astype(BF16)
        q2 = qp[:, h * P_QDIM + P_QDIM // 2:(h + 1) * P_QDIM].astype(BF16)
        s1_s[h] = _nt(k1_ref[h], q1)
        s2_s[h] = _nt(k2_ref[h], q2)
    s1 = s1_s[...]
    s2 = s2_s[...]
    _, rank2 = _top_rows((s1, s2), (v1_s, v2_s), P_TOPK, (False, True))
    v2_lo = v2_s[:, 0:SUBLANE, :]
    cand_s[:, 0:SUBLANE, :] = v2_lo + v1_s[:, 0:1, :]
    cand_s[:, SUBLANE:2 * SUBLANE, :] = v2_s[:, SUBLANE:2 * SUBLANE, :] + v1_s[:, 0:1, :]
    cand_s[:, 2 * SUBLANE:3 * SUBLANE, :] = v1_s[:, SUBLANE:2 * SUBLANE, :] + v2_s[:, 0:1, :]
    for r in range(1, SUBLANE):
        cand_s[:, (2 + r) * SUBLANE:(3 + r) * SUBLANE, :] = v2_lo + v1_s[:, r:r + 1, :]
    _top_rows((cand_s[...],), (vals_s,), P_TOPK, (False,))
    top = vals_s[:, 0:1, :]
    tau = vals_s[:, P_TOPK - 1:P_TOPK, :]
    z = jnp.sum(jnp.exp(vals_s[...] - top), axis=1, keepdims=True)
    v1 = v1_s[...]
    kept = jnp.zeros(v1.shape, F32)
    for r in range(P_TOPK):
        kept = kept + jnp.where(v2_s[:, r:r + 1, :] + v1 >= tau, 1.0, 0.0)
    n_keep = jnp.zeros(s1.shape, F32)
    for r in range(P_TOPK):
        n_keep = jnp.where(s1 == v1_s[:, r:r + 1, :], kept[:, r:r + 1, :], n_keep)
    r2_ref[...] = rank2.astype(BF16)
    e2_ref[...] = jnp.exp(s2 - v2_s[:, 0:1, :]).astype(BF16)
    l_ref[...] = n_keep
    c_ref[...] = jnp.exp(s1 - v1_s[:, 0:1, :]) / z


def peer_select(x, g, w_q, k1, k2, tm):
    n, d = x.shape
    spec = pl.BlockSpec((P_HEADS, P_NKEYS, tm), lambda i: (0, 0, i))
    key_spec = pl.BlockSpec((P_HEADS, P_NKEYS, P_QDIM // 2), lambda i: (0, 0, 0))
    shp = lambda dt: jax.ShapeDtypeStruct((P_HEADS, P_NKEYS, n), dt)
    return pl.pallas_call(
        _peer_select_body,
        out_shape=(shp(BF16), shp(BF16), shp(F32), shp(F32)),
        grid=(n // tm,),
        in_specs=[pl.BlockSpec((tm, d), lambda i: (i, 0)),
                  pl.BlockSpec((1, d), lambda i: (0, 0)),
                  pl.BlockSpec((d, P_HEADS * P_QDIM), lambda i: (0, 0)),
                  key_spec, key_spec],
        out_specs=(spec, spec, spec, spec),
        scratch_shapes=[pltpu.VMEM((P_HEADS, P_NKEYS, tm), F32), pltpu.VMEM((P_HEADS, P_NKEYS, tm), F32),
                        pltpu.VMEM((P_HEADS, P_TOPK, tm), F32), pltpu.VMEM((P_HEADS, P_TOPK, tm), F32),
                        pltpu.VMEM((P_HEADS, N_CAND, tm), F32), pltpu.VMEM((P_HEADS, P_TOPK, tm), F32)],
        compiler_params=_params(1), name="peer_select",
    )(x, g.reshape(1, d), w_q, k1, k2)


def _peer_main_body(x_ref, g_ref, og_ref, u_ref, vt_ref, r2_ref, e2_ref, l_ref, c_ref,
                    o_ref, hn_s, acc_s, *, a_tile, norm_out):
    t = pl.program_id(1)

    @pl.when(t == 0)
    def _():
        hn_s[...] = _rms(x_ref[...], g_ref[...]).astype(BF16)
        acc_s[...] = jnp.zeros_like(acc_s)

    act = _gelu(_nt(u_ref[...], hn_s[...])).astype(BF16)
    tm = act.shape[1]
    zero = jnp.zeros((P_NKEYS, tm), BF16)
    parts = []
    for aa in range(a_tile):
        w = None
        for h in range(P_HEADS):
            keep = jnp.broadcast_to(l_ref[h, aa:aa + 1, :], (P_NKEYS, tm)).astype(BF16)
            coef = jnp.broadcast_to(c_ref[h, aa:aa + 1, :], (P_NKEYS, tm)).astype(BF16)
            term = jnp.where(r2_ref[h] < keep, e2_ref[h], zero) * coef
            w = term if w is None else w + term
        parts.append(w * act[aa * P_NKEYS:(aa + 1) * P_NKEYS])
    acc_s[...] += _nn(vt_ref[...], jnp.concatenate(parts, axis=0))

    @pl.when(t == pl.num_programs(1) - 1)
    def _():
        y = x_ref[...] + acc_s[...].T
        o_ref[...] = _rms(y, og_ref[...]) if norm_out else y


def peer_main(x, g, u, vt, sel, tm, te, out_norm=None):
    n, d = x.shape
    og = g if out_norm is None else out_norm
    r2, e2, n_keep, coef = sel
    a_tile = te // P_NKEYS
    n_tiles = P_EXPERTS // te
    assert a_tile % SUBLANE == 0
    full_spec = pl.BlockSpec((P_HEADS, P_NKEYS, tm), lambda i, t: (0, 0, i))
    row_spec = pl.BlockSpec((P_HEADS, a_tile, tm), lambda i, t: (0, t, i))
    body = functools.partial(_peer_main_body, a_tile=a_tile, norm_out=out_norm is not None)
    return pl.pallas_call(
        body,
        out_shape=jax.ShapeDtypeStruct((n, d), F32),
        grid=(n // tm, n_tiles),
        in_specs=[pl.BlockSpec((tm, d), lambda i, t: (i, 0)),
                  pl.BlockSpec((1, d), lambda i, t: (0, 0)),
                  pl.BlockSpec((1, d), lambda i, t: (0, 0)),
                  pl.BlockSpec((te, d), lambda i, t: (t, 0)),
                  pl.BlockSpec((d, te), lambda i, t: (0, t)),
                  full_spec, full_spec, row_spec, row_spec],
        out_specs=pl.BlockSpec((tm, d), lambda i, t: (i, 0)),
        scratch_shapes=[pltpu.VMEM((tm, d), BF16), pltpu.VMEM((d, tm), F32)],
        compiler_params=_params(2), name="peer_main",
    )(x, g.reshape(1, d), og.reshape(1, d), u, vt, r2, e2, n_keep, coef)


def peer_ffn_residual(x, g, w_q, k1, k2, u, vt, tm, te, out_norm=None):
    sel = peer_select(x, g, w_q, k1, k2, tm)
    return peer_main(x, g, u, vt, sel, tm, te, out_norm)


def _topk_mask(score, lane_f, k):
    sel = jnp.zeros(score.shape, F32)
    for _ in range(k):
        m = jnp.max(score, axis=1, keepdims=True)
        idx = jnp.min(jnp.where(score == m, lane_f, 1e9), axis=1, keepdims=True)
        hit = lane_f == idx
        sel = jnp.where(hit, 1.0, sel)
        score = jnp.where(hit, NEG_INF, score)
    return sel


def _block_scores(imp, pos_col, lane):
    cur = lax.shift_right_arithmetic(pos_col, SEL_BLK_LOG2)
    forced = (lane == 0) | (lane == cur) | (lane == cur - 1)
    return jnp.where(forced, jnp.inf, jnp.where(lane <= cur, imp, NEG_INF))


def _stack_heads(q, g):
    return jnp.concatenate(
        [q[:, (g * A_REP + r) * A_DH:(g * A_REP + r + 1) * A_DH] for r in range(A_REP)], axis=0)


def _combine_heads(o_ref, gate, o_cmp, o_sel, o_win, g, rows):
    for r in range(A_REP):
        h = g * A_REP + r
        rs = slice(r * rows, (r + 1) * rows)
        o_ref[:, h * A_DH:(h + 1) * A_DH] = (gate[:, 3 * h:3 * h + 1] * o_cmp[rs]
                                             + gate[:, 3 * h + 1:3 * h + 2] * o_sel[rs]
                                             + gate[:, 3 * h + 2:3 * h + 3] * o_win[rs])


def _online_softmax_step(m_ref, l_ref, acc_ref, g, sc, valid, pv):
    sc = jnp.where(valid, sc, NEG_INF)
    m_old = m_ref[g]
    m_new = jnp.maximum(m_old, jnp.max(sc, axis=1, keepdims=True))
    m_safe = jnp.where(m_new == NEG_INF, 0.0, m_new)
    alpha = jnp.exp(m_old - m_safe)
    e = jnp.exp(sc - m_safe)
    l_ref[g] = alpha * l_ref[g] + jnp.sum(e, axis=1, keepdims=True)
    acc_ref[g] = alpha * acc_ref[g] + pv(e.astype(BF16))
    m_ref[g] = m_new


def _compress_weights(w1, b1, w2):
    eye = jnp.eye(A_KV, dtype=F32)
    w1bd = jnp.einsum("cldh,gk->clgdkh", w1, eye).reshape(2, CMP_LEN, GRP_W, GRP_W).astype(BF16)
    w1cat = jnp.concatenate([w1bd[:, :CMP_STRIDE], w1bd[:, CMP_STRIDE:]], axis=-1)
    w2bd = jnp.einsum("chd,gk->cghkd", w2, eye).reshape(2, GRP_W, GRP_W).astype(BF16)
    b1t = jnp.tile(b1, (1, A_KV)).reshape(2, 1, GRP_W)
    return w1cat, b1t, w2bd


def _split3(x):
    hi = x.astype(BF16)
    r1 = x - hi.astype(F32)
    mid = r1.astype(BF16)
    lo = (r1 - mid.astype(F32)).astype(BF16)
    return hi, mid, lo


def _block_importance(p_sum, ov_ref):
    hi, mid, lo = _split3(p_sum)
    ov = ov_ref[...]
    return _nn(hi, ov) + _nn(mid, ov) + _nn(lo, ov)


def _overlap_matrix(n_rows, n_cols, shift):
    c_start = (np.arange(n_rows) - shift) * CMP_STRIDE
    s_start = np.arange(n_cols) * SEL_BLK
    ov = np.clip(np.minimum(c_start[:, None] + CMP_LEN, s_start[None, :] + SEL_BLK)
                 - np.maximum(c_start[:, None], s_start[None, :]), 0, None) / CMP_STRIDE
    ov[c_start < 0] = 0.0
    return jnp.asarray(ov, BF16)


def _cmp_prompt_body(r_ref, w1_ref, b1_ref, w2_ref, ck_ref, cv_ref):
    for c, out_ref in ((0, ck_ref), (1, cv_ref)):
        both = None
        for l in range(CMP_STRIDE):
            lo = l * ROW_W + c * GRP_W
            y = _nn(r_ref[0, :, lo:lo + GRP_W].astype(BF16), w1_ref[c, l])
            both = y if both is None else both + y
        first, second = both[:, :GRP_W], both[:, GRP_W:]
        n = first.shape[0]
        hid = _gelu(first + pltpu.roll(second, n - 1, 0) + b1_ref[c])
        out_ref[0] = _nn(hid.astype(BF16), w2_ref[c])


def compress_prompt(rows16, w1bd, b1t, w2bd):
    b, n, w = rows16.shape
    out = jax.ShapeDtypeStruct((b, n, GRP_W), F32)
    return pl.pallas_call(
        _cmp_prompt_body,
        out_shape=(out, out),
        grid=(b,),
        in_specs=[pl.BlockSpec((1, n, w), lambda i: (i, 0, 0)),
                  pl.BlockSpec(w1bd.shape, lambda i: (0, 0, 0, 0)),
                  pl.BlockSpec(b1t.shape, lambda i: (0, 0, 0)),
                  pl.BlockSpec(w2bd.shape, lambda i: (0, 0, 0))],
        out_specs=(pl.BlockSpec((1, n, GRP_W), lambda i: (i, 0, 0)),
                   pl.BlockSpec((1, n, GRP_W), lambda i: (i, 0, 0))),
        compiler_params=_params(1), name="compress_prompt",
    )(rows16, w1bd, b1t, w2bd)


NSA_KEY_CHUNK = 512


def _nsa_prompt_body(q_ref, gate_ref, ck_ref, cv_ref, sel_ref, win_ref, ov_ref, ex_ref, bg_ref,
                     o_ref, mask_s, m_s, l_s, acc_s, ocmp_s, *, tq, t_len):
    i = pl.program_id(1)
    q0 = i * tq
    q = q_ref[...] * (ATTN_SCALE * LOG2E)
    pos = q0 + _iota((tq, 1), 0)
    pos4 = jnp.concatenate([pos] * A_REP, axis=0)
    n_cmp_rows = ck_ref.shape[1]
    n_lane = _iota((1, n_cmp_rows), 1)
    valid_cmp = (n_lane * CMP_STRIDE + (CMP_LEN - 1) <= pos) & (n_lane < n_cmp_rows - 1)
    bias_cmp = jnp.where(valid_cmp, 0.0, NEG_INF)
    qgs = [_stack_heads(q, g).astype(BF16) for g in range(A_KV)]

    def add_bias(s, bias):
        return jnp.concatenate([s[r * tq:(r + 1) * tq] + bias for r in range(A_REP)], axis=0)

    imps = []
    for g in range(A_KV):
        gs = slice(g * A_DH, (g + 1) * A_DH)
        s_c = add_bias(_nt(qgs[g], ck_ref[0, :, gs].astype(BF16)), bias_cmp)
        m_c = jnp.max(s_c, axis=1, keepdims=True)
        e_c = jnp.exp2(s_c - jnp.where(m_c == NEG_INF, 0.0, m_c))
        p_cmp = e_c / jnp.maximum(jnp.sum(e_c, axis=1, keepdims=True), 1e-30)
        ocmp_s[g] = _nn(p_cmp.astype(BF16), cv_ref[0, :, gs].astype(BF16))
        p_sum = p_cmp[0:tq]
        for r in range(1, A_REP):
            p_sum = p_sum + p_cmp[r * tq:(r + 1) * tq]
        imps.append(_block_importance(p_sum, ov_ref))
    blk_lane = _iota((A_KV * tq, LANE), 1)
    score = _block_scores(jnp.concatenate(imps, axis=0), pos4, blk_lane)
    few_blocks = q0 + tq <= N_SEL * SEL_BLK

    @pl.when(few_blocks)
    def _():
        mask_s[...] = jnp.where(score == NEG_INF, 0.0, 1.0)

    @pl.when(jnp.logical_not(few_blocks))
    def _():
        mask_s[...] = _topk_mask(score, blk_lane.astype(F32), N_SEL)

    m_s[...] = jnp.full_like(m_s, NEG_INF)
    l_s[...] = jnp.zeros_like(l_s)
    acc_s[...] = jnp.zeros_like(acc_s)
    kc = NSA_KEY_CHUNK
    for c in range(t_len // kc):
        @pl.when(c * kc < q0 + tq)
        def _(c=c):
            key_lane = c * kc + _iota((1, kc), 1)
            causal_bias = jnp.where(key_lane <= pos, 0.0, MASKED)
            for g in range(A_KV):
                blk_bias = _nn((mask_s[g * tq:(g + 1) * tq, :] - 1.0).astype(BF16),
                               ex_ref[:, c * kc:(c + 1) * kc])
                k = sel_ref[c * kc:(c + 1) * kc, g * A_DH:(g + 1) * A_DH].astype(BF16)
                v = sel_ref[c * kc:(c + 1) * kc,
                            GRP_W + g * A_DH:GRP_W + (g + 1) * A_DH].astype(BF16)
                sc = add_bias(_nt(qgs[g], k), blk_bias + causal_bias)
                m_old = m_s[g]
                m_new = jnp.maximum(m_old, jnp.max(sc, axis=1, keepdims=True))
                alpha = jnp.exp2(m_old - m_new)
                e = jnp.exp2(sc - m_new)
                l_s[g] = alpha * l_s[g] + jnp.sum(e, axis=1, keepdims=True)
                acc_s[g] = alpha * acc_s[g] + _nn(e.astype(BF16), v)
                m_s[g] = m_new

    gate = jax.nn.sigmoid(gate_ref[...] + bg_ref[...])
    band = WINDOW + tq
    w_start = pl.multiple_of(jnp.maximum(q0 - WINDOW, 0), tq)
    kpos_w = w_start + _iota((1, band), 1)
    diff_w = pos - kpos_w
    bias_w = jnp.where((diff_w >= 0) & (diff_w <= WINDOW), 0.0, MASKED)
    for g in range(A_KV):
        gs = slice(g * A_DH, (g + 1) * A_DH)
        vs_ = slice(GRP_W + g * A_DH, GRP_W + (g + 1) * A_DH)
        kw = win_ref[pl.ds(w_start, band), gs].astype(BF16)
        vw = win_ref[pl.ds(w_start, band), vs_].astype(BF16)
        s_w = add_bias(_nt(qgs[g], kw), bias_w)
        e_w = jnp.exp2(s_w - jnp.max(s_w, axis=1, keepdims=True))
        o_win = _nn(e_w.astype(BF16), vw) / jnp.sum(e_w, axis=1, keepdims=True)
        o_sel = acc_s[g] / jnp.maximum(l_s[g], 1e-30)
        _combine_heads(o_ref, gate, ocmp_s[g], o_sel, o_win, g, tq)


def nsa_prompt_attention(q, gate_pre, ck, cv, sel_rows, win_rows, b_gate_row, bsz, t_len, tq):
    n = q.shape[0]
    n_q = t_len // tq
    n_blk = t_len // SEL_BLK
    assert t_len % NSA_KEY_CHUNK == 0 and n_blk <= LANE
    ov = jnp.pad(_overlap_matrix(ck.shape[1], n_blk, 0), ((0, 0), (0, LANE - n_blk)))
    ex = (np.arange(LANE)[:, None] == (np.arange(t_len)[None, :] // SEL_BLK))
    ex = jnp.asarray(ex * -MASKED, BF16)
    rows4 = A_REP * tq
    body = functools.partial(_nsa_prompt_body, tq=tq, t_len=t_len)
    return pl.pallas_call(
        body,
        out_shape=jax.ShapeDtypeStruct((n, A_QW), F32),
        grid=(bsz, n_q),
        in_specs=[pl.BlockSpec((tq, A_QW), lambda b, i: (b * n_q + i, 0)),
                  pl.BlockSpec((tq, LANE), lambda b, i: (b * n_q + i, 0)),
                  pl.BlockSpec((1,) + ck.shape[1:], lambda b, i: (b, 0, 0)),
                  pl.BlockSpec((1,) + cv.shape[1:], lambda b, i: (b, 0, 0)),
                  pl.BlockSpec((t_len, ROW_W), lambda b, i: (b, 0)),
                  pl.BlockSpec((t_len, ROW_W), lambda b, i: (b, 0)),
                  pl.BlockSpec(ov.shape, lambda b, i: (0, 0)),
                  pl.BlockSpec(ex.shape, lambda b, i: (0, 0)),
                  pl.BlockSpec((1, LANE), lambda b, i: (0, 0))],
        out_specs=pl.BlockSpec((tq, A_QW), lambda b, i: (b * n_q + i, 0)),
        scratch_shapes=[pltpu.VMEM((A_KV * tq, LANE), F32),
                        pltpu.VMEM((A_KV, rows4, 1), F32),
                        pltpu.VMEM((A_KV, rows4, 1), F32),
                        pltpu.VMEM((A_KV, rows4, A_DH), F32),
                        pltpu.VMEM((A_KV, rows4, A_DH), F32)],
        compiler_params=_params(2), name="nsa_prompt_attention",
    )(q, gate_pre, ck, cv, sel_rows, win_rows, ov, ex, b_gate_row)


DEC_ROWS = 8
CMP_PAGES_PER_STEP = 32
SEL_PAGES_PER_STEP = 32
R16_PER_PAGE = PAGE_SIZE // CMP_STRIDE


def _dec_cmp_body(pt_ref, *refs, past_len, n_blk_pad):
    del pt_ref
    pages = refs[:CMP_PAGES_PER_STEP]
    (w1_ref, b1_ref, w2_ref, q_ref, ov_ref, ocmp_ref, msel_ref,
     ck_s, cv_s, carry_s, x_s) = refs[CMP_PAGES_PER_STEP:]
    s = pl.program_id(1)
    rows = CMP_PAGES_PER_STEP * R16_PER_PAGE

    @pl.when(s == 0)
    def _():
        carry_s[...] = jnp.zeros_like(carry_s)

    out_row = _iota((PAGE_SIZE, PAGE_SIZE), 0)
    src_tok = (CMP_STRIDE * (out_row & (R16_PER_PAGE - 1))
               + lax.shift_right_logical(out_row, R16_PER_PAGE.bit_length() - 1))
    perm = jnp.where(_iota((PAGE_SIZE, PAGE_SIZE), 1) == src_tok, 1.0, 0.0).astype(BF16)
    for k, p in enumerate(pages):
        x_t = _nt(perm, p[0].astype(BF16))
        for l in range(CMP_STRIDE):
            x_s[l, k * R16_PER_PAGE:(k + 1) * R16_PER_PAGE, :] = (
                x_t[l * R16_PER_PAGE:(l + 1) * R16_PER_PAGE, :])

    rid = _iota((rows, 1), 0)
    for c, dst in ((0, ck_s), (1, cv_s)):
        both = None
        for l in range(CMP_STRIDE):
            x = x_s[l, :, c * GRP_W:(c + 1) * GRP_W].astype(BF16)
            y = _nn(x, w1_ref[c, l])
            both = y if both is None else both + y
        first, second = both[:, :GRP_W], both[:, GRP_W:]
        prev = carry_s[c, SUBLANE - 1:SUBLANE, :]
        shifted = jnp.where(rid == 0, prev, pltpu.roll(first, 1, 0))
        hid = _gelu(shifted + second + b1_ref[c])
        dst[pl.ds(pl.multiple_of(s * rows, rows), rows), :] = _nn(hid.astype(BF16), w2_ref[c])
        carry_s[c] = first[rows - SUBLANE:rows, :]

    @pl.when(s == pl.num_programs(1) - 1)
    def _():
        n_rows = ck_s.shape[0]
        q = q_ref[0]
        t_col = _iota((DEC_ROWS, 1), 0)
        pos = past_len + t_col
        pos4 = jnp.concatenate([pos] * A_REP, axis=0)
        m_lane = _iota((1, n_rows), 1)
        valid = (m_lane >= 1) & ((m_lane - 1) * CMP_STRIDE + (CMP_LEN - 1) <= pos4)
        imps = []
        for g in range(A_KV):
            gs = slice(g * A_DH, (g + 1) * A_DH)
            qg = _stack_heads(q, g).astype(BF16)
            p = _masked_softmax(_nt(qg, ck_s[:, gs].astype(BF16)) * ATTN_SCALE, valid)
            o = _nn(p.astype(BF16), cv_s[:, gs].astype(BF16))
            for r in range(A_REP):
                h = g * A_REP + r
                ocmp_ref[0, :, h * A_DH:(h + 1) * A_DH] = o[r * DEC_ROWS:(r + 1) * DEC_ROWS]
            p_sum = p[0:DEC_ROWS]
            for r in range(1, A_REP):
                p_sum = p_sum + p[r * DEC_ROWS:(r + 1) * DEC_ROWS]
            imps.append(_block_importance(p_sum, ov_ref))
        imp = jnp.concatenate(imps, axis=0)
        lane = _iota((A_KV * DEC_ROWS, n_blk_pad), 1)
        score = _block_scores(imp, pos4, lane)
        mask = _topk_mask(score, lane.astype(F32), N_SEL)
        for j in range(n_blk_pad // LANE):
            msel_ref[0, j] = mask[:, j * LANE:(j + 1) * LANE]


def nsa_decode_compress(cache_t, page_table, w1bd, b1t, w2bd, q8, past_len, n_blk_pad):
    dbsz, n_pages = page_table.shape
    assert n_pages % CMP_PAGES_PER_STEP == 0
    n_steps = n_pages // CMP_PAGES_PER_STEP
    n_rows = n_pages * R16_PER_PAGE
    n_tiles = n_blk_pad // LANE
    ov = _overlap_matrix(n_rows, n_blk_pad, 1)

    def page_spec(k):
        return pl.BlockSpec((1, ROW_W, PAGE_SIZE),
                            lambda b, s, pt: (pt[b, s * CMP_PAGES_PER_STEP + k], 0, 0))

    grid_spec = pltpu.PrefetchScalarGridSpec(
        num_scalar_prefetch=1,
        grid=(dbsz, n_steps),
        in_specs=[page_spec(k) for k in range(CMP_PAGES_PER_STEP)] + [
            pl.BlockSpec(w1bd.shape, lambda b, s, pt: (0, 0, 0, 0)),
            pl.BlockSpec(b1t.shape, lambda b, s, pt: (0, 0, 0)),
            pl.BlockSpec(w2bd.shape, lambda b, s, pt: (0, 0, 0)),
            pl.BlockSpec((1, DEC_ROWS, A_QW), lambda b, s, pt: (b, 0, 0)),
            pl.BlockSpec(ov.shape, lambda b, s, pt: (0, 0))],
        out_specs=(pl.BlockSpec((1, DEC_ROWS, A_QW), lambda b, s, pt: (b, 0, 0)),
                   pl.BlockSpec((1, n_tiles, A_KV * DEC_ROWS, LANE), lambda b, s, pt: (b, 0, 0, 0))),
        scratch_shapes=[pltpu.VMEM((n_rows, GRP_W), F32), pltpu.VMEM((n_rows, GRP_W), F32),
                        pltpu.VMEM((2, SUBLANE, GRP_W), F32),
                        pltpu.VMEM((CMP_STRIDE, CMP_PAGES_PER_STEP * R16_PER_PAGE, ROW_W), F32)])
    body = functools.partial(_dec_cmp_body, past_len=past_len, n_blk_pad=n_blk_pad)
    return pl.pallas_call(
        body,
        out_shape=(jax.ShapeDtypeStruct((dbsz, DEC_ROWS, A_QW), F32),
                   jax.ShapeDtypeStruct((dbsz, n_tiles, A_KV * DEC_ROWS, LANE), F32)),
        grid_spec=grid_spec,
        compiler_params=_params(2), name="nsa_decode_compress",
    )(page_table, *([cache_t] * CMP_PAGES_PER_STEP), w1bd, b1t, w2bd, q8, ov)


def _dec_sel_body(pt_ref, *refs, past_len, t_new):
    del pt_ref
    pages = refs[:SEL_PAGES_PER_STEP]
    (q_ref, msel_ref, ocmp_ref, snew_ref, wpast_ref, wnew_ref, gate_ref, bg_ref,
     o_ref, m_s, l_s, acc_s) = refs[SEL_PAGES_PER_STEP:]
    s = pl.program_id(1)
    keys = SEL_PAGES_PER_STEP * PAGE_SIZE
    blks = keys // SEL_BLK
    steps_per_tile = LANE // blks

    @pl.when(s == 0)
    def _():
        m_s[...] = jnp.full_like(m_s, NEG_INF)
        l_s[...] = jnp.zeros_like(l_s)
        acc_s[...] = jnp.zeros_like(acc_s)

    q = q_ref[0]
    tile = s // steps_per_tile
    first_blk = (s - tile * steps_per_tile) * blks
    expand = (_iota((LANE, keys), 0)
              == first_blk + lax.shift_right_arithmetic(_iota((LANE, keys), 1), SEL_BLK_LOG2))
    expand = jnp.where(expand, 1.0, 0.0).astype(BF16)
    m_keys = _nn(msel_ref[0, tile].astype(BF16), expand)

    for g in range(A_KV):
        gs = slice(g * A_DH, (g + 1) * A_DH)
        vs_ = slice(GRP_W + g * A_DH, GRP_W + (g + 1) * A_DH)
        qg = _stack_heads(q, g).astype(BF16)
        k_t = jnp.concatenate([p[0, gs, :] for p in pages], axis=1).astype(BF16)
        v_t = jnp.concatenate([p[0, vs_, :] for p in pages], axis=1).astype(BF16)
        valid = jnp.concatenate([m_keys[g * DEC_ROWS:(g + 1) * DEC_ROWS]] * A_REP, axis=0) > 0.5
        _online_softmax_step(m_s, l_s, acc_s, g, _nn(qg, k_t) * ATTN_SCALE, valid,
                             lambda e, v_t=v_t: _nt(e, v_t))

    @pl.when(s == pl.num_programs(1) - 1)
    def _():
        t_col = _iota((DEC_ROWS, 1), 0)
        t4 = jnp.concatenate([t_col] * A_REP, axis=0)
        t_key = _iota((1, DEC_ROWS), 1)
        valid_new = (t_key <= t4) & (t_key < t_new)
        gate = jax.nn.sigmoid(gate_ref[0] + bg_ref[...])
        n_win = wpast_ref.shape[2]
        kpos_w = past_len - n_win + _iota((1, n_win), 1)
        diff_w = (past_len + t4) - kpos_w
        valid_wp = (diff_w >= 0) & (diff_w <= WINDOW)
        new_blk = past_len // SEL_BLK
        new_tile, new_lane = new_blk // LANE, new_blk % LANE
        for g in range(A_KV):
            gs = slice(g * A_DH, (g + 1) * A_DH)
            vs_ = slice(GRP_W + g * A_DH, GRP_W + (g + 1) * A_DH)
            qg = _stack_heads(q, g).astype(BF16)
            mnew = msel_ref[0, new_tile, g * DEC_ROWS:(g + 1) * DEC_ROWS, new_lane:new_lane + 1]
            mnew4 = jnp.concatenate([mnew] * A_REP, axis=0)
            v_new = snew_ref[0, :, vs_].astype(BF16)
            _online_softmax_step(m_s, l_s, acc_s, g,
                                 _nt(qg, snew_ref[0, :, gs].astype(BF16)) * ATTN_SCALE,
                                 valid_new & (mnew4 > 0.5), lambda e, v_new=v_new: _nn(e, v_new))
            o_sel = acc_s[g] / jnp.maximum(l_s[g], 1e-30)

            s_p = jnp.where(valid_wp, _nn(qg, wpast_ref[0, gs, :].astype(BF16)) * ATTN_SCALE, NEG_INF)
            s_n = jnp.where(valid_new, _nt(qg, wnew_ref[0, :, gs].astype(BF16)) * ATTN_SCALE, NEG_INF)
            mx = jnp.maximum(jnp.max(s_p, axis=1, keepdims=True), jnp.max(s_n, axis=1, keepdims=True))
            mx = jnp.where(mx == NEG_INF, 0.0, mx)
            e_p = jnp.exp(s_p - mx)
            e_n = jnp.exp(s_n - mx)
            den = jnp.maximum(jnp.sum(e_p, axis=1, keepdims=True)
                              + jnp.sum(e_n, axis=1, keepdims=True), 1e-30)
            o_win = (_nt(e_p.astype(BF16), wpast_ref[0, vs_, :].astype(BF16))
                     + _nn(e_n.astype(BF16), wnew_ref[0, :, vs_].astype(BF16))) / den
            o_cmp = _stack_heads(ocmp_ref[0], g)
            _combine_heads(o_ref.at[0], gate, o_cmp, o_sel, o_win, g, DEC_ROWS)


def nsa_decode_attention(cache_t, page_table, q8, msel, ocmp, sel_new, win_past_t, win_new,
                         gate8, b_gate_row, past_len, t_new):
    dbsz, n_pages = page_table.shape
    keys = SEL_PAGES_PER_STEP * PAGE_SIZE
    assert n_pages % SEL_PAGES_PER_STEP == 0 and past_len % SEL_BLK == 0
    assert LANE % (keys // SEL_BLK) == 0
    n_steps = n_pages // SEL_PAGES_PER_STEP
    n_win = win_past_t.shape[2]

    def page_spec(k):
        return pl.BlockSpec((1, ROW_W, PAGE_SIZE),
                            lambda b, s, pt: (pt[b, s * SEL_PAGES_PER_STEP + k], 0, 0))

    per_b = lambda shp: pl.BlockSpec((1,) + shp, lambda b, s, pt: (b,) + (0,) * len(shp))
    grid_spec = pltpu.PrefetchScalarGridSpec(
        num_scalar_prefetch=1,
        grid=(dbsz, n_steps),
        in_specs=[page_spec(k) for k in range(SEL_PAGES_PER_STEP)] + [
            per_b((DEC_ROWS, A_QW)), per_b(msel.shape[1:]), per_b((DEC_ROWS, A_QW)),
            per_b((DEC_ROWS, ROW_W)), per_b((ROW_W, n_win)), per_b((DEC_ROWS, ROW_W)),
            per_b((DEC_ROWS, LANE)), pl.BlockSpec((1, LANE), lambda b, s, pt: (0, 0))],
        out_specs=per_b((DEC_ROWS, A_QW)),
        scratch_shapes=[pltpu.VMEM((A_KV, A_REP * DEC_ROWS, 1), F32),
                        pltpu.VMEM((A_KV, A_REP * DEC_ROWS, 1), F32),
                        pltpu.VMEM((A_KV, A_REP * DEC_ROWS, A_DH), F32)])
    body = functools.partial(_dec_sel_body, past_len=past_len, t_new=t_new)
    return pl.pallas_call(
        body,
        out_shape=jax.ShapeDtypeStruct((dbsz, DEC_ROWS, A_QW), F32),
        grid_spec=grid_spec,
        compiler_params=_params(2), name="nsa_decode_attention",
    )(page_table, *([cache_t] * SEL_PAGES_PER_STEP), q8, msel, ocmp, sel_new, win_past_t,
      win_new, gate8, b_gate_row)


def tokens_minor(rows):
    n, t = rows.shape[:2]
    return jnp.transpose(rows, (0, 2, 3, 4, 1)).reshape(n, ROW_W, t)


PROMPT_TM = 256
PEER_TM = 512
PEER_TE = 2048
NSA_TQ = 128


def kernel(x_prompt, x_sample, cache_cmp_kv, cache_sel_kv, state_win_kv, state_C, state_n, state_m,
           page_table, norm_mix, norm_ffn, norm_final, mlstm_w_in, mlstm_b_i, mlstm_b_f,
           mlstm_head_norm, mlstm_w_out, nsa_w_in, nsa_b_gate, nsa_cmp_w1, nsa_cmp_b1, nsa_cmp_w2,
           nsa_w_out, peer_w_q, peer_sub_keys, peer_u, peer_v):
    bsz, t_len, d = x_prompt.shape
    dbsz, t_s, _ = x_sample.shape
    n_pages = page_table.shape[1]
    past_len = n_pages * PAGE_SIZE
    assert norm_mix.shape[0] == 2 and d == D_MODEL and t_s <= DEC_ROWS // 2
    assert t_len % M_CHUNK == 0 and t_len % NSA_TQ == 0 and t_len >= WINDOW

    xp = x_prompt.reshape(bsz * t_len, d)
    reps = DEC_ROWS // t_s
    xs = jnp.concatenate([x_sample] * reps, axis=1).reshape(dbsz * DEC_ROWS, d)
    n_s = dbsz * DEC_ROWS

    def peer(x, layer, tm, out_norm=None):
        keys = peer_sub_keys[layer].astype(BF16)
        return peer_ffn_residual(x, norm_ffn[layer], peer_w_q[layer].astype(BF16),
                                 keys[:, 0], keys[:, 1], peer_u[layer].astype(BF16),
                                 peer_v[layer].T.astype(BF16), tm, PEER_TE, out_norm)

    w_in = jnp.pad(mlstm_w_in[0], ((0, 0), (0, M_PROJ_PAD - M_PROJ))).astype(BF16)
    w_out = mlstm_w_out[0].astype(BF16)
    gate_bias = jnp.pad(jnp.concatenate([mlstm_b_i[0], mlstm_b_f[0]]),
                        (0, LANE - 2 * M_HEADS)).reshape(1, LANE)
    proj_p = rms_matmul(xp, norm_mix[0], w_in, PROMPT_TM).reshape(bsz, t_len, M_PROJ_PAD)
    proj_s = rms_matmul(xs, norm_mix[0], w_in, n_s).reshape(dbsz, DEC_ROWS, M_PROJ_PAD)
    hid_p, c_p, n_p, m_p = mlstm_scan(
        proj_p, jnp.zeros((bsz, M_HEADS, M_DK, M_DV), F32), jnp.zeros((bsz, M_HEADS, M_DK), F32),
        jnp.zeros((bsz, M_HEADS), F32), gate_bias, mlstm_head_norm[0], M_CHUNK, M_CHUNK - 1)
    hid_s, c_s, n_st, m_st = mlstm_scan(
        proj_s, state_C[0], state_n[0], state_m[0], gate_bias, mlstm_head_norm[0],
        DEC_ROWS, t_s - 1)
    xp = matmul_residual(hid_p.reshape(bsz * t_len, d), w_out, xp, PROMPT_TM)
    xs = matmul_residual(hid_s.reshape(n_s, d), w_out, xs, n_s)
    xp = peer(xp, 0, PEER_TM)
    xs = peer(xs, 0, n_s)

    w_in_a = jnp.pad(nsa_w_in[0], ((0, 0), (0, A_PROJ_PAD - A_PROJ))).astype(BF16)
    w_out_a = nsa_w_out[0].astype(BF16)
    b_gate_row = jnp.pad(nsa_b_gate[0].reshape(-1), (0, LANE - 3 * A_HEADS)).reshape(1, LANE)
    w1bd, b1t, w2bd = _compress_weights(nsa_cmp_w1[0], nsa_cmp_b1[0], nsa_cmp_w2[0])
    tabs_p = rope_tables(jnp.arange(t_len, dtype=jnp.int32))
    pos_s = past_len + jnp.arange(DEC_ROWS, dtype=jnp.int32)
    tabs_s = tuple(jnp.tile(a, (dbsz, 1)) for a in rope_tables(pos_s))

    q_p, cmp_p, sel_p, win_p, gate_p = nsa_project(xp, norm_mix[1], w_in_a, tabs_p, PROMPT_TM)
    q_s, cmp_s, sel_s, win_s, gate_s = nsa_project(xs, norm_mix[1], w_in_a, tabs_s, n_s)

    ck, cv = compress_prompt(cmp_p.reshape(bsz, t_len // CMP_STRIDE, CMP_STRIDE * ROW_W),
                             w1bd, b1t, w2bd)
    o_p = nsa_prompt_attention(q_p, gate_p, ck, cv, sel_p, win_p, b_gate_row, bsz, t_len, NSA_TQ)
    xp = matmul_residual(o_p, w_out_a, xp, PROMPT_TM)

    n_blk = -(-(past_len + t_s) // SEL_BLK)
    n_blk_pad = -(-n_blk // LANE) * LANE
    q8 = q_s.reshape(dbsz, DEC_ROWS, A_QW)
    o_cmp, msel = nsa_decode_compress(tokens_minor(cache_cmp_kv[0]), page_table, w1bd, b1t, w2bd,
                                      q8, past_len, n_blk_pad)
    o_s = nsa_decode_attention(
        tokens_minor(cache_sel_kv[0]), page_table, q8, msel, o_cmp,
        sel_s.reshape(dbsz, DEC_ROWS, ROW_W), tokens_minor(state_win_kv[0]),
        win_s.reshape(dbsz, DEC_ROWS, ROW_W), gate_s.reshape(dbsz, DEC_ROWS, LANE), b_gate_row,
        past_len, t_s)
    xs = matmul_residual(o_s.reshape(n_s, d), w_out_a, xs, n_s)
    y_p = peer(xp, 1, PEER_TM, norm_final).reshape(bsz, t_len, d)
    y_s = peer(xs, 1, n_s, norm_final).reshape(dbsz, DEC_ROWS, d)[:, :t_s]

    row_shape = (2, A_KV, A_DH)
    kv_p = lambda a: a.reshape((1, bsz, t_len) + row_shape)
    kv_s = lambda a: a.reshape((dbsz, DEC_ROWS) + row_shape)[None, :, :t_s]
    win_rows = min(WINDOW, t_len)
    win_buf_p = win_p.reshape((bsz, t_len) + row_shape)[None, :, t_len - win_rows:]
    win_new_s = win_s.reshape((dbsz, DEC_ROWS) + row_shape)[:, :t_s]
    win_buf_s = jnp.concatenate([state_win_kv[0], win_new_s], axis=1)[None, :, t_s:]
    return (y_p, y_s,
            kv_p(cmp_p), kv_p(sel_p), win_buf_p,
            c_p[None], n_p[None], m_p[None],
            kv_s(cmp_s), kv_s(sel_s), win_buf_s,
            c_s[None], n_st[None], m_st[None])
```

```python
import functools

import numpy as np
import jax
import jax.numpy as jnp
from jax import lax
from jax.experimental import pallas as pl
from jax.experimental.pallas import tpu as pltpu

F32 = jnp.float32
BF16 = jnp.bfloat16
HI = lax.Precision.HIGHEST

V7X_VMEM_BYTES = 64 * 1024 * 1024
VMEM_LIMIT = V7X_VMEM_BYTES - 8 * 1024 * 1024
LANE = 128
SUBLANE = 8

D_MODEL = 1024
NORM_EPS = 1e-6

M_HEADS = 8
M_DK = 64
M_DV = 128
M_CHUNK = 64
M_PROJ = 2 * M_HEADS * M_DK + 2 * M_HEADS * M_DV + 2 * M_HEADS
M_PROJ_PAD = 3200

A_HEADS = 16
A_KV = 4
A_REP = 4
A_DH = 64
ROT_DIM = 16
ROPE_THETA = 500000.0
CMP_STRIDE = 16
CMP_LEN = 32
SEL_BLK = 64
SEL_BLK_LOG2 = 6
N_SEL = 16
WINDOW = 512
A_QW = 1024
A_KVW = 1536
A_PROJ = A_QW + A_KVW + 3 * A_HEADS
A_PROJ_PAD = 2688
ATTN_SCALE = A_DH ** -0.5
GRP_W = A_KV * A_DH
ROW_W = 2 * GRP_W
PAGE_SIZE = 128

P_HEADS = 8
P_NKEYS = 128
P_EXPERTS = P_NKEYS * P_NKEYS
P_QDIM = 256
P_TOPK = 16

NEG_INF = float("-inf")
LOG2E = 1.4426950408889634
MASKED = -(2.0 ** 100)


def _params(n_axes):
    return pltpu.CompilerParams(dimension_semantics=("arbitrary",) * n_axes,
                                vmem_limit_bytes=VMEM_LIMIT)


def _nn(a, b, precision=None):
    return jnp.dot(a, b, preferred_element_type=F32, precision=precision)


def _nt(a, b, precision=None):
    return lax.dot_general(a, b, (((1,), (1,)), ((), ())), preferred_element_type=F32,
                           precision=precision)


def _tn(a, b, precision=None):
    return lax.dot_general(a, b, (((0,), (0,)), ((), ())), preferred_element_type=F32,
                           precision=precision)


def _gelu(x):
    return 0.5 * x * (1.0 + lax.erf(x * (2.0 ** -0.5)))


def _rms(x, g):
    return x * lax.rsqrt(jnp.mean(x * x, axis=-1, keepdims=True) + NORM_EPS) * g


def _masked_softmax(s, valid):
    s = jnp.where(valid, s, NEG_INF)
    m = jnp.max(s, axis=-1, keepdims=True)
    m = jnp.where(m == NEG_INF, 0.0, m)
    e = jnp.exp(s - m)
    return e / jnp.maximum(jnp.sum(e, axis=-1, keepdims=True), 1e-30)


def _iota(shape, dim):
    return lax.broadcasted_iota(jnp.int32, shape, dim)


def _rms_mm_body(x_ref, g_ref, w_ref, o_ref):
    o_ref[...] = _nn(_rms(x_ref[...], g_ref[...]).astype(BF16), w_ref[...])


def rms_matmul(x, g, w, tm):
    n, d = x.shape
    nout = w.shape[1]
    return pl.pallas_call(
        _rms_mm_body,
        out_shape=jax.ShapeDtypeStruct((n, nout), F32),
        grid=(n // tm,),
        in_specs=[pl.BlockSpec((tm, d), lambda i: (i, 0)),
                  pl.BlockSpec((1, d), lambda i: (0, 0)),
                  pl.BlockSpec((d, nout), lambda i: (0, 0))],
        out_specs=pl.BlockSpec((tm, nout), lambda i: (i, 0)),
        compiler_params=_params(1), name="rms_matmul",
    )(x, g.reshape(1, d), w)


def _mm_res_body(a_ref, w_ref, x_ref, o_ref):
    o_ref[...] = x_ref[...] + _nn(a_ref[...].astype(BF16), w_ref[...])


def matmul_residual(a, w, x, tm):
    n, k = a.shape
    d = w.shape[1]
    return pl.pallas_call(
        _mm_res_body,
        out_shape=jax.ShapeDtypeStruct((n, d), F32),
        grid=(n // tm,),
        in_specs=[pl.BlockSpec((tm, k), lambda i: (i, 0)),
                  pl.BlockSpec((k, d), lambda i: (0, 0)),
                  pl.BlockSpec((tm, d), lambda i: (i, 0))],
        out_specs=pl.BlockSpec((tm, d), lambda i: (i, 0)),
        compiler_params=_params(1), name="matmul_residual",
    )(a, w, x)


def _nsa_proj_body(x_ref, g_ref, w_ref, c_ref, sa_ref, sb_ref,
                   q_ref, cmp_ref, sel_ref, win_ref, gate_ref):
    y = _nn(_rms(x_ref[...], g_ref[...]).astype(BF16), w_ref[...])
    c = c_ref[...]
    sa = sa_ref[...]
    sb = sb_ref[...]

    def rot(z):
        return (z * c + pltpu.roll(z, GRP_W - ROT_DIM // 2, 1) * sa
                + pltpu.roll(z, ROT_DIM // 2, 1) * sb)

    for j in range(A_QW // GRP_W):
        q_ref[:, j * GRP_W:(j + 1) * GRP_W] = rot(y[:, j * GRP_W:(j + 1) * GRP_W])
    for ref, base in ((cmp_ref, A_QW), (sel_ref, A_QW + ROW_W), (win_ref, A_QW + 2 * ROW_W)):
        ref[:, 0:GRP_W] = rot(y[:, base:base + GRP_W])
        ref[:, GRP_W:ROW_W] = y[:, base + GRP_W:base + ROW_W]
    gate_ref[...] = y[:, A_QW + A_KVW:A_PROJ_PAD]


def nsa_project(x, g, w, tabs, tm):
    n, d = x.shape
    n_tab = tabs[0].shape[0] // tm
    tab_spec = pl.BlockSpec((tm, GRP_W), lambda i: (i % n_tab, 0))
    row = lambda w_: pl.BlockSpec((tm, w_), lambda i: (i, 0))
    return pl.pallas_call(
        _nsa_proj_body,
        out_shape=(jax.ShapeDtypeStruct((n, A_QW), F32),
                   jax.ShapeDtypeStruct((n, ROW_W), F32),
                   jax.ShapeDtypeStruct((n, ROW_W), F32),
                   jax.ShapeDtypeStruct((n, ROW_W), F32),
                   jax.ShapeDtypeStruct((n, LANE), F32)),
        grid=(n // tm,),
        in_specs=[row(d), pl.BlockSpec((1, d), lambda i: (0, 0)),
                  pl.BlockSpec((d, A_PROJ_PAD), lambda i: (0, 0)),
                  tab_spec, tab_spec, tab_spec],
        out_specs=(row(A_QW), row(ROW_W), row(ROW_W), row(ROW_W), row(LANE)),
        compiler_params=_params(1), name="nsa_project",
    )(x, g.reshape(1, d), w, *tabs)


def rope_tables(pos):
    half = ROT_DIM // 2
    inv_freq = ROPE_THETA ** (-(jnp.arange(half, dtype=F32) * (2.0 / ROT_DIM)))
    ang = pos.astype(F32)[:, None] * inv_freq[None, :]
    cos, sin = jnp.cos(ang), jnp.sin(ang)
    t = pos.shape[0]
    one = jnp.ones((t, A_DH - ROT_DIM), F32)
    zero = jnp.zeros((t, A_DH - ROT_DIM), F32)
    z8 = jnp.zeros((t, half), F32)
    c = jnp.concatenate([cos, cos, one], axis=1)
    sa = jnp.concatenate([-sin, z8, zero], axis=1)
    sb = jnp.concatenate([z8, sin, zero], axis=1)
    return tuple(jnp.tile(a, (1, A_KV)) for a in (c, sa, sb))


def _mlstm_body(q_ref, k_ref, v_ref, og_ref, gt_ref, c0_ref, n0_ref, m0_ref, gb_ref, hn_ref,
                hid_ref, c_out, n_out, m_out, c_s, n_s, m_s, *, chunk, t_last, nb):
    ci = pl.program_id(1)

    @pl.when(ci == 0)
    def _():
        c_s[...] = c0_ref[...]
        n_s[...] = n0_ref[...]
        m_s[...] = m0_ref[...]

    L = chunk
    row = _iota((L, L), 0)
    col = _iota((L, L), 1)
    causal = row >= col
    tril = causal.astype(F32)
    eye8 = (_iota((SUBLANE, LANE), 0) == _iota((SUBLANE, LANE), 1)).astype(F32)
    rlane = _iota((L, LANE), 0)
    lane1 = _iota((1, LANE), 1)
    sel_row = _iota((LANE, LANE), 0)
    ones_l = jnp.ones((L, LANE), BF16)
    ones_k = jnp.ones((M_DK, LANE), BF16)
    ones_v = jnp.ones((M_DV, LANE), BF16)

    def replicate(parts, col_idx):
        sel = jnp.where(sel_row == col_idx, 1.0, 0.0).astype(BF16)
        return _nn(parts[0], sel) + _nn(parts[1], sel) + _nn(parts[2], sel)

    units = [(bi, h) for bi in range(nb) for h in range(M_HEADS)]
    b_parts, g_parts, m_parts, g_rows = [], [], [], []
    for bi in range(nb):
        gt = gt_ref[bi] + gb_ref[...]
        log_f = jnp.minimum(gt, 0.0) - jnp.log1p(jnp.exp(-jnp.abs(gt)))
        ba = _nn(tril, log_f, HI)
        g_mat = gt - pltpu.roll(ba, LANE - M_HEADS, 1)
        b_parts.append(_split3(ba))
        g_parts.append(_split3(g_mat))
        m_parts.append(_split3(jnp.broadcast_to(m_s[bi], (SUBLANE, LANE))))
        g_rows.append(_nt(eye8, g_mat, HI))

    qs, ks, vs, cs, ns = {}, {}, {}, {}, {}
    s_raw, q_c, b_rep, g_rep, m_prev = {}, {}, {}, {}, {}
    for u in units:
        bi, h = u
        qs[u] = q_ref[bi, :, h * M_DK:(h + 1) * M_DK]
        ks[u] = k_ref[bi, :, h * M_DK:(h + 1) * M_DK] * (M_DK ** -0.5)
        vs[u] = v_ref[bi, :, h * M_DV:(h + 1) * M_DV]
        cs[u] = c_s[bi, h]
        ns[u] = n_s[bi, h:h + 1, :]
        s_raw[u] = _nt(qs[u].astype(BF16), ks[u].astype(BF16))
        q_c[u] = _nn(qs[u].astype(BF16), cs[u].astype(BF16))
        b_rep[u] = replicate(b_parts[bi], M_HEADS + h)
        g_rep[u] = replicate(g_parts[bi], h)
        m_prev[u] = replicate(m_parts[bi], h)[0:1, :]

    a_rep, m_t, w_d, w_a = {}, {}, {}, {}
    for u in units:
        bi, h = u
        pm = g_rep[u]
        shift = 1
        while shift < L:
            pm = jnp.where(rlane >= shift, jnp.maximum(pm, pltpu.roll(pm, shift, 0)), pm)
            shift *= 2
        a_rep[u] = b_rep[u] + m_prev[u]
        m_t[u] = jnp.maximum(a_rep[u], b_rep[u] + pm)
        dm = jnp.where(causal, b_rep[u][:, :L] + g_rows[bi][h:h + 1, :], NEG_INF)
        w_d[u] = jnp.exp(dm - m_t[u][:, :L])
        w_a[u] = jnp.exp(a_rep[u] - m_t[u])

    for u in units:
        bi, h = u
        s = s_raw[u] * w_d[u]
        s_b = s.astype(BF16)
        num = w_a[u] * q_c[u] + _nn(s_b, vs[u].astype(BF16))
        den = w_a[u] * _nn((qs[u] * ns[u]).astype(BF16), ones_k) + _nn(s_b, ones_l)
        h_out = num / jnp.maximum(jnp.abs(den), jnp.exp(-m_t[u]))
        msq = _nn((h_out * h_out).astype(BF16), ones_v) * (1.0 / M_DV)
        hid = h_out * lax.rsqrt(msq + NORM_EPS) * hn_ref[:, h * M_DV:(h + 1) * M_DV]
        hid_ref[bi, :, h * M_DV:(h + 1) * M_DV] = (
            jax.nn.sigmoid(og_ref[bi, :, h * M_DV:(h + 1) * M_DV]) * hid)

    m_rows = [m_s[bi] for bi in range(nb)]
    for u in units:
        bi, h = u
        m_new = m_t[u][t_last:t_last + 1, :]
        d_last = jnp.where(rlane <= t_last, b_rep[u][t_last:t_last + 1, :] + g_rep[u], NEG_INF)
        w_last = jnp.exp(d_last - m_new)
        decay = jnp.exp(a_rep[u][t_last:t_last + 1, :] - m_new)
        kw = ks[u] * w_last[:, :M_DK]
        c_s[bi, h] = decay * cs[u] + _tn(kw, vs[u], HI)
        n_s[bi, h:h + 1, :] = decay[:, :M_DK] * ns[u] + jnp.sum(kw, axis=0, keepdims=True)
        m_rows[bi] = jnp.where(lane1 == h, m_new, m_rows[bi])
    for bi in range(nb):
        m_s[bi] = m_rows[bi]

    @pl.when(ci == pl.num_programs(1) - 1)
    def _():
        c_out[...] = c_s[...]
        n_out[...] = n_s[...]
        m_out[...] = m_s[...]


MLSTM_SEQ_PER_STEP = 4


def mlstm_scan(proj, c0, n0, m0, gate_bias, head_norm, chunk, t_last):
    b, t, _ = proj.shape
    nb = MLSTM_SEQ_PER_STEP
    assert b % nb == 0
    n_chunks = t // chunk
    hd = M_HEADS * M_DK
    vd = M_HEADS * M_DV
    m0p = jnp.pad(m0, ((0, 0), (0, LANE - M_HEADS))).reshape(b, 1, LANE)
    body = functools.partial(_mlstm_body, chunk=chunk, t_last=t_last, nb=nb)
    hid, c_t, n_t, m_t = pl.pallas_call(
        body,
        out_shape=(jax.ShapeDtypeStruct((b, t, vd), F32),
                   jax.ShapeDtypeStruct((b, M_HEADS, M_DK, M_DV), F32),
                   jax.ShapeDtypeStruct((b, M_HEADS, M_DK), F32),
                   jax.ShapeDtypeStruct((b, 1, LANE), F32)),
        grid=(b // nb, n_chunks),
        in_specs=[pl.BlockSpec((nb, chunk, hd), lambda i, c: (i, c, 0)),
                  pl.BlockSpec((nb, chunk, hd), lambda i, c: (i, c, 1)),
                  pl.BlockSpec((nb, chunk, vd), lambda i, c: (i, c, 1)),
                  pl.BlockSpec((nb, chunk, vd), lambda i, c: (i, c, 2)),
                  pl.BlockSpec((nb, chunk, LANE), lambda i, c: (i, c, (2 * hd + 2 * vd) // LANE)),
                  pl.BlockSpec((nb, M_HEADS, M_DK, M_DV), lambda i, c: (i, 0, 0, 0)),
                  pl.BlockSpec((nb, M_HEADS, M_DK), lambda i, c: (i, 0, 0)),
                  pl.BlockSpec((nb, 1, LANE), lambda i, c: (i, 0, 0)),
                  pl.BlockSpec((1, LANE), lambda i, c: (0, 0)),
                  pl.BlockSpec((1, vd), lambda i, c: (0, 0))],
        out_specs=(pl.BlockSpec((nb, chunk, vd), lambda i, c: (i, c, 0)),
                   pl.BlockSpec((nb, M_HEADS, M_DK, M_DV), lambda i, c: (i, 0, 0, 0)),
                   pl.BlockSpec((nb, M_HEADS, M_DK), lambda i, c: (i, 0, 0)),
                   pl.BlockSpec((nb, 1, LANE), lambda i, c: (i, 0, 0))),
        scratch_shapes=[pltpu.VMEM((nb, M_HEADS, M_DK, M_DV), F32),
                        pltpu.VMEM((nb, M_HEADS, M_DK), F32),
                        pltpu.VMEM((nb, 1, LANE), F32)],
        compiler_params=_params(2), name="mlstm_scan",
    )(proj, proj, proj, proj, proj, c0, n0, m0p, gate_bias, head_norm.reshape(1, vd))
    return hid, c_t, n_t, m_t[:, 0, :M_HEADS]


N_CAND = (2 + SUBLANE) * SUBLANE
RANK_OUT = float(P_TOPK)


def _top_rows(xs, dsts, k, with_rank):
    xs = list(xs)
    ranks = [jnp.full(x.shape, RANK_OUT, F32) if w else None for x, w in zip(xs, with_rank)]
    for r in range(k):
        for i, dst in enumerate(dsts):
            m = jnp.max(xs[i], axis=1, keepdims=True)
            dst[:, r:r + 1, :] = m
            hit = xs[i] == m
            if ranks[i] is not None:
                ranks[i] = jnp.where(hit, float(r), ranks[i])
            xs[i] = jnp.where(hit, NEG_INF, xs[i])
    return ranks


def _peer_select_body(x_ref, g_ref, wq_ref, k1_ref, k2_ref, r2_ref, e2_ref, l_ref, c_ref,
                      s1_s, s2_s, v1_s, v2_s, cand_s, vals_s):
    qp = _nn(_rms(x_ref[...], g_ref[...]).astype(BF16), wq_ref[...])
    for h in range(P_HEADS):
        q1 = qp[:, h * P_QDIM:h * P_QDIM + P_QDIM // 2].astype(BF16)
        q2 = qp[:, h * P_QDIM + P_QDIM // 2:(h + 1) * P_QDIM].astype(BF16)
        s1_s[h] = _nt(k1_ref[h], q1)
        s2_s[h] = _nt(k2_ref[h], q2)
    s1 = s1_s[...]
    s2 = s2_s[...]
    _, rank2 = _top_rows((s1, s2), (v1_s, v2_s), P_TOPK, (False, True))
    v2_lo = v2_s[:, 0:SUBLANE, :]
    cand_s[:, 0:SUBLANE, :] = v2_lo + v1_s[:, 0:1, :]
    cand_s[:, SUBLANE:2 * SUBLANE, :] = v2_s[:, SUBLANE:2 * SUBLANE, :] + v1_s[:, 0:1, :]
    cand_s[:, 2 * SUBLANE:3 * SUBLANE, :] = v1_s[:, SUBLANE:2 * SUBLANE, :] + v2_s[:, 0:1, :]
    for r in range(1, SUBLANE):
        cand_s[:, (2 + r) * SUBLANE:(3 + r) * SUBLANE, :] = v2_lo + v1_s[:, r:r + 1, :]
    _top_rows((cand_s[...],), (vals_s,), P_TOPK, (False,))
    top = vals_s[:, 0:1, :]
    tau = vals_s[:, P_TOPK - 1:P_TOPK, :]
    z = jnp.sum(jnp.exp(vals_s[...] - top), axis=1, keepdims=True)
    v1 = v1_s[...]
    kept = jnp.zeros(v1.shape, F32)
    for r in range(P_TOPK):
        kept = kept + jnp.where(v2_s[:, r:r + 1, :] + v1 >= tau, 1.0, 0.0)
    n_keep = jnp.zeros(s1.shape, F32)
    for r in range(P_TOPK):
        n_keep = jnp.where(s1 == v1_s[:, r:r + 1, :], kept[:, r:r + 1, :], n_keep)
    r2_ref[...] = rank2.astype(BF16)
    e2_ref[...] = jnp.exp(s2 - v2_s[:, 0:1, :]).astype(BF16)
    l_ref[...] = n_keep
    c_ref[...] = 0.5 * jnp.exp(s1 - v1_s[:, 0:1, :]) / z


def peer_select(x, g, w_q, k1, k2, tm):
    n, d = x.shape
    spec = pl.BlockSpec((P_HEADS, P_NKEYS, tm), lambda i: (0, 0, i))
    key_spec = pl.BlockSpec((P_HEADS, P_NKEYS, P_QDIM // 2), lambda i: (0, 0, 0))
    shp = lambda dt: jax.ShapeDtypeStruct((P_HEADS, P_NKEYS, n), dt)
    return pl.pallas_call(
        _peer_select_body,
        out_shape=(shp(BF16), shp(BF16), shp(F32), shp(F32)),
        grid=(n // tm,),
        in_specs=[pl.BlockSpec((tm, d), lambda i: (i, 0)),
                  pl.BlockSpec((1, d), lambda i: (0, 0)),
                  pl.BlockSpec((d, P_HEADS * P_QDIM), lambda i: (0, 0)),
                  key_spec, key_spec],
        out_specs=(spec, spec, spec, spec),
        scratch_shapes=[pltpu.VMEM((P_HEADS, P_NKEYS, tm), F32), pltpu.VMEM((P_HEADS, P_NKEYS, tm), F32),
                        pltpu.VMEM((P_HEADS, P_TOPK, tm), F32), pltpu.VMEM((P_HEADS, P_TOPK, tm), F32),
                        pltpu.VMEM((P_HEADS, N_CAND, tm), F32), pltpu.VMEM((P_HEADS, P_TOPK, tm), F32)],
        compiler_params=_params(1), name="peer_select",
    )(x, g.reshape(1, d), w_q, k1, k2)


def _peer_main_body(x_ref, g_ref, og_ref, u_ref, vt_ref, r2_ref, e2_ref, l_ref, c_ref,
                    o_ref, hn_s, acc_s, *, a_tile, norm_out):
    t = pl.program_id(1)

    @pl.when(t == 0)
    def _():
        hn_s[...] = _rms(x_ref[...], g_ref[...]).astype(BF16)
        acc_s[...] = jnp.zeros_like(acc_s)

    tm = hn_s.shape[0]
    tg = tm // PEER_TOKEN_GROUPS
    zero = jnp.zeros((P_NKEYS, tg), BF16)
    for grp in range(PEER_TOKEN_GROUPS):
        ts = slice(grp * tg, (grp + 1) * tg)
        sc = _nt(u_ref[...], hn_s[ts, :])
        act = (sc + sc * lax.erf(sc * (2.0 ** -0.5))).astype(BF16)
        parts = []
        for aa in range(a_tile):
            w = None
            for h in range(P_HEADS):
                keep = jnp.broadcast_to(l_ref[h, aa:aa + 1, ts], (P_NKEYS, tg)).astype(BF16)
                coef = jnp.broadcast_to(c_ref[h, aa:aa + 1, ts], (P_NKEYS, tg)).astype(BF16)
                term = jnp.where(r2_ref[h, :, ts] < keep, e2_ref[h, :, ts], zero) * coef
                w = term if w is None else w + term
            parts.append(w * act[aa * P_NKEYS:(aa + 1) * P_NKEYS])
        acc_s[:, ts] += _nn(vt_ref[...], jnp.concatenate(parts, axis=0))

    @pl.when(t == pl.num_programs(1) - 1)
    def _():
        y = x_ref[...] + acc_s[...].T
        o_ref[...] = _rms(y, og_ref[...]) if norm_out else y


def peer_main(x, g, u, vt, sel, tm, te, out_norm=None):
    n, d = x.shape
    og = g if out_norm is None else out_norm
    r2, e2, n_keep, coef = sel
    a_tile = te // P_NKEYS
    n_tiles = P_EXPERTS // te
    assert a_tile % SUBLANE == 0
    full_spec = pl.BlockSpec((P_HEADS, P_NKEYS, tm), lambda i, t: (0, 0, i))
    row_spec = pl.BlockSpec((P_HEADS, a_tile, tm), lambda i, t: (0, t, i))
    body = functools.partial(_peer_main_body, a_tile=a_tile, norm_out=out_norm is not None)
    return pl.pallas_call(
        body,
        out_shape=jax.ShapeDtypeStruct((n, d), F32),
        grid=(n // tm, n_tiles),
        in_specs=[pl.BlockSpec((tm, d), lambda i, t: (i, 0)),
                  pl.BlockSpec((1, d), lambda i, t: (0, 0)),
                  pl.BlockSpec((1, d), lambda i, t: (0, 0)),
                  pl.BlockSpec((te, d), lambda i, t: (t, 0)),
                  pl.BlockSpec((d, te), lambda i, t: (0, t)),
                  full_spec, full_spec, row_spec, row_spec],
        out_specs=pl.BlockSpec((tm, d), lambda i, t: (i, 0)),
        scratch_shapes=[pltpu.VMEM((tm, d), BF16), pltpu.VMEM((d, tm), F32)],
        compiler_params=_params(2), name="peer_main",
    )(x, g.reshape(1, d), og.reshape(1, d), u, vt, r2, e2, n_keep, coef)


def peer_ffn_residual(x, g, w_q, k1, k2, u, vt, tm, te, out_norm=None):
    sel = peer_select(x, g, w_q, k1, k2, tm)
    return peer_main(x, g, u, vt, sel, tm, te, out_norm)


def _topk_mask(score, lane_f, k):
    sel = jnp.zeros(score.shape, F32)
    for _ in range(k):
        m = jnp.max(score, axis=1, keepdims=True)
        idx = jnp.min(jnp.where(score == m, lane_f, 1e9), axis=1, keepdims=True)
        hit = lane_f == idx
        sel = jnp.where(hit, 1.0, sel)
        score = jnp.where(hit, NEG_INF, score)
    return sel


def _block_scores(imp, pos_col, lane):
    cur = lax.shift_right_arithmetic(pos_col, SEL_BLK_LOG2)
    forced = (lane == 0) | (lane == cur) | (lane == cur - 1)
    return jnp.where(forced, jnp.inf, jnp.where(lane <= cur, imp, NEG_INF))


def _stack_heads(q, g):
    return jnp.concatenate(
        [q[:, (g * A_REP + r) * A_DH:(g * A_REP + r + 1) * A_DH] for r in range(A_REP)], axis=0)


def _combine_heads(o_ref, gate, o_cmp, o_sel, o_win, g, rows):
    for r in range(A_REP):
        h = g * A_REP + r
        rs = slice(r * rows, (r + 1) * rows)
        o_ref[:, h * A_DH:(h + 1) * A_DH] = (gate[:, 3 * h:3 * h + 1] * o_cmp[rs]
                                             + gate[:, 3 * h + 1:3 * h + 2] * o_sel[rs]
                                             + gate[:, 3 * h + 2:3 * h + 3] * o_win[rs])


def _online_softmax_step(m_ref, l_ref, acc_ref, g, sc, valid, pv):
    sc = jnp.where(valid, sc, NEG_INF)
    m_old = m_ref[g]
    m_new = jnp.maximum(m_old, jnp.max(sc, axis=1, keepdims=True))
    m_safe = jnp.where(m_new == NEG_INF, 0.0, m_new)
    alpha = jnp.exp(m_old - m_safe)
    e = jnp.exp(sc - m_safe)
    l_ref[g] = alpha * l_ref[g] + jnp.sum(e, axis=1, keepdims=True)
    acc_ref[g] = alpha * acc_ref[g] + pv(e.astype(BF16))
    m_ref[g] = m_new


def _compress_weights(w1, b1, w2):
    eye = jnp.eye(A_KV, dtype=F32)
    w1bd = jnp.einsum("cldh,gk->clgdkh", w1, eye).reshape(2, CMP_LEN, GRP_W, GRP_W).astype(BF16)
    w1cat = jnp.concatenate([w1bd[:, :CMP_STRIDE], w1bd[:, CMP_STRIDE:]], axis=-1)
    w2bd = jnp.einsum("chd,gk->cghkd", w2, eye).reshape(2, GRP_W, GRP_W).astype(BF16)
    b1t = jnp.tile(b1, (1, A_KV)).reshape(2, 1, GRP_W)
    return w1cat, b1t, w2bd


def _split3(x):
    hi = x.astype(BF16)
    r1 = x - hi.astype(F32)
    mid = r1.astype(BF16)
    lo = (r1 - mid.astype(F32)).astype(BF16)
    return hi, mid, lo


def _block_importance(p_sum, ov_ref):
    hi, mid, lo = _split3(p_sum)
    ov = ov_ref[...]
    return _nn(hi, ov) + _nn(mid, ov) + _nn(lo, ov)


def _overlap_matrix(n_rows, n_cols, shift):
    c_start = (np.arange(n_rows) - shift) * CMP_STRIDE
    s_start = np.arange(n_cols) * SEL_BLK
    ov = np.clip(np.minimum(c_start[:, None] + CMP_LEN, s_start[None, :] + SEL_BLK)
                 - np.maximum(c_start[:, None], s_start[None, :]), 0, None) / CMP_STRIDE
    ov[c_start < 0] = 0.0
    return jnp.asarray(ov, BF16)


def _cmp_prompt_body(r_ref, w1_ref, b1_ref, w2_ref, ck_ref, cv_ref):
    for c, out_ref in ((0, ck_ref), (1, cv_ref)):
        both = None
        for l in range(CMP_STRIDE):
            lo = l * ROW_W + c * GRP_W
            y = _nn(r_ref[0, :, lo:lo + GRP_W].astype(BF16), w1_ref[c, l])
            both = y if both is None else both + y
        first, second = both[:, :GRP_W], both[:, GRP_W:]
        n = first.shape[0]
        hid = _gelu(first + pltpu.roll(second, n - 1, 0) + b1_ref[c])
        out_ref[0] = _nn(hid.astype(BF16), w2_ref[c])


def compress_prompt(rows16, w1bd, b1t, w2bd):
    b, n, w = rows16.shape
    out = jax.ShapeDtypeStruct((b, n, GRP_W), F32)
    return pl.pallas_call(
        _cmp_prompt_body,
        out_shape=(out, out),
        grid=(b,),
        in_specs=[pl.BlockSpec((1, n, w), lambda i: (i, 0, 0)),
                  pl.BlockSpec(w1bd.shape, lambda i: (0, 0, 0, 0)),
                  pl.BlockSpec(b1t.shape, lambda i: (0, 0, 0)),
                  pl.BlockSpec(w2bd.shape, lambda i: (0, 0, 0))],
        out_specs=(pl.BlockSpec((1, n, GRP_W), lambda i: (i, 0, 0)),
                   pl.BlockSpec((1, n, GRP_W), lambda i: (i, 0, 0))),
        compiler_params=_params(1), name="compress_prompt",
    )(rows16, w1bd, b1t, w2bd)


NSA_KEY_CHUNK = 512


def _nsa_prompt_body(q_ref, gate_ref, ck_ref, cv_ref, sel_ref, win_ref, ov_ref, ex_ref, bg_ref,
                     o_ref, mask_s, m_s, l_s, acc_s, ocmp_s, *, tq, t_len):
    i = pl.program_id(1)
    q0 = i * tq
    q = q_ref[...] * (ATTN_SCALE * LOG2E)
    pos = q0 + _iota((tq, 1), 0)
    pos4 = jnp.concatenate([pos] * A_REP, axis=0)
    n_cmp_rows = ck_ref.shape[1]
    n_lane = _iota((1, n_cmp_rows), 1)
    valid_cmp = (n_lane * CMP_STRIDE + (CMP_LEN - 1) <= pos) & (n_lane < n_cmp_rows - 1)
    bias_cmp = jnp.where(valid_cmp, 0.0, NEG_INF)
    qgs = [_stack_heads(q, g).astype(BF16) for g in range(A_KV)]

    def add_bias(s, bias):
        return jnp.concatenate([s[r * tq:(r + 1) * tq] + bias for r in range(A_REP)], axis=0)

    imps = []
    for g in range(A_KV):
        gs = slice(g * A_DH, (g + 1) * A_DH)
        s_c = add_bias(_nt(qgs[g], ck_ref[0, :, gs].astype(BF16)), bias_cmp)
        m_c = jnp.max(s_c, axis=1, keepdims=True)
        e_c = jnp.exp2(s_c - jnp.where(m_c == NEG_INF, 0.0, m_c))
        p_cmp = e_c / jnp.maximum(jnp.sum(e_c, axis=1, keepdims=True), 1e-30)
        ocmp_s[g] = _nn(p_cmp.astype(BF16), cv_ref[0, :, gs].astype(BF16))
        p_sum = p_cmp[0:tq]
        for r in range(1, A_REP):
            p_sum = p_sum + p_cmp[r * tq:(r + 1) * tq]
        imps.append(_block_importance(p_sum, ov_ref))
    blk_lane = _iota((A_KV * tq, LANE), 1)
    score = _block_scores(jnp.concatenate(imps, axis=0), pos4, blk_lane)
    few_blocks = q0 + tq <= N_SEL * SEL_BLK

    @pl.when(few_blocks)
    def _():
        mask_s[...] = jnp.where(score == NEG_INF, 0.0, 1.0)

    @pl.when(jnp.logical_not(few_blocks))
    def _():
        mask_s[...] = _topk_mask(score, blk_lane.astype(F32), N_SEL)

    m_s[...] = jnp.full_like(m_s, NEG_INF)
    l_s[...] = jnp.zeros_like(l_s)
    acc_s[...] = jnp.zeros_like(acc_s)
    kc = NSA_KEY_CHUNK
    for c in range(t_len // kc):
        @pl.when(c * kc < q0 + tq)
        def _(c=c):
            key_lane = c * kc + _iota((1, kc), 1)
            causal_bias = jnp.where(key_lane <= pos, 0.0, MASKED)
            for g in range(A_KV):
                blk_bias = _nn((mask_s[g * tq:(g + 1) * tq, :] - 1.0).astype(BF16),
                               ex_ref[:, c * kc:(c + 1) * kc])
                k = sel_ref[c * kc:(c + 1) * kc, g * A_DH:(g + 1) * A_DH].astype(BF16)
                v = sel_ref[c * kc:(c + 1) * kc,
                            GRP_W + g * A_DH:GRP_W + (g + 1) * A_DH].astype(BF16)
                sc = add_bias(_nt(qgs[g], k), blk_bias + causal_bias)
                m_old = m_s[g]
                m_new = jnp.maximum(m_old, jnp.max(sc, axis=1, keepdims=True))
                alpha = jnp.exp2(m_old - m_new)
                e = jnp.exp2(sc - m_new)
                l_s[g] = alpha * l_s[g] + jnp.sum(e, axis=1, keepdims=True)
                acc_s[g] = alpha * acc_s[g] + _nn(e.astype(BF16), v)
                m_s[g] = m_new

    gate = jax.nn.sigmoid(gate_ref[...] + bg_ref[...])
    band = WINDOW + tq
    w_start = pl.multiple_of(jnp.maximum(q0 - WINDOW, 0), tq)
    kpos_w = w_start + _iota((1, band), 1)
    diff_w = pos - kpos_w
    bias_w = jnp.where((diff_w >= 0) & (diff_w <= WINDOW), 0.0, MASKED)
    for g in range(A_KV):
        gs = slice(g * A_DH, (g + 1) * A_DH)
        vs_ = slice(GRP_W + g * A_DH, GRP_W + (g + 1) * A_DH)
        kw = win_ref[pl.ds(w_start, band), gs].astype(BF16)
        vw = win_ref[pl.ds(w_start, band), vs_].astype(BF16)
        s_w = add_bias(_nt(qgs[g], kw), bias_w)
        e_w = jnp.exp2(s_w - jnp.max(s_w, axis=1, keepdims=True))
        o_win = _nn(e_w.astype(BF16), vw) / jnp.sum(e_w, axis=1, keepdims=True)
        o_sel = acc_s[g] / jnp.maximum(l_s[g], 1e-30)
        _combine_heads(o_ref, gate, ocmp_s[g], o_sel, o_win, g, tq)


def nsa_prompt_attention(q, gate_pre, ck, cv, sel_rows, win_rows, b_gate_row, bsz, t_len, tq):
    n = q.shape[0]
    n_q = t_len // tq
    n_blk = t_len // SEL_BLK
    assert t_len % NSA_KEY_CHUNK == 0 and n_blk <= LANE
    ov = jnp.pad(_overlap_matrix(ck.shape[1], n_blk, 0), ((0, 0), (0, LANE - n_blk)))
    ex = (np.arange(LANE)[:, None] == (np.arange(t_len)[None, :] // SEL_BLK))
    ex = jnp.asarray(ex * -MASKED, BF16)
    rows4 = A_REP * tq
    body = functools.partial(_nsa_prompt_body, tq=tq, t_len=t_len)
    return pl.pallas_call(
        body,
        out_shape=jax.ShapeDtypeStruct((n, A_QW), F32),
        grid=(bsz, n_q),
        in_specs=[pl.BlockSpec((tq, A_QW), lambda b, i: (b * n_q + i, 0)),
                  pl.BlockSpec((tq, LANE), lambda b, i: (b * n_q + i, 0)),
                  pl.BlockSpec((1,) + ck.shape[1:], lambda b, i: (b, 0, 0)),
                  pl.BlockSpec((1,) + cv.shape[1:], lambda b, i: (b, 0, 0)),
                  pl.BlockSpec((t_len, ROW_W), lambda b, i: (b, 0)),
                  pl.BlockSpec((t_len, ROW_W), lambda b, i: (b, 0)),
                  pl.BlockSpec(ov.shape, lambda b, i: (0, 0)),
                  pl.BlockSpec(ex.shape, lambda b, i: (0, 0)),
                  pl.BlockSpec((1, LANE), lambda b, i: (0, 0))],
        out_specs=pl.BlockSpec((tq, A_QW), lambda b, i: (b * n_q + i, 0)),
        scratch_shapes=[pltpu.VMEM((A_KV * tq, LANE), F32),
                        pltpu.VMEM((A_KV, rows4, 1), F32),
                        pltpu.VMEM((A_KV, rows4, 1), F32),
                        pltpu.VMEM((A_KV, rows4, A_DH), F32),
                        pltpu.VMEM((A_KV, rows4, A_DH), F32)],
        compiler_params=_params(2), name="nsa_prompt_attention",
    )(q, gate_pre, ck, cv, sel_rows, win_rows, ov, ex, b_gate_row)


DEC_ROWS = 8
CMP_PAGES_PER_STEP = 32
SEL_PAGES_PER_STEP = 32
R16_PER_PAGE = PAGE_SIZE // CMP_STRIDE


def _dec_cmp_body(pt_ref, *refs, past_len, n_blk_pad):
    del pt_ref
    pages = refs[:CMP_PAGES_PER_STEP]
    (w1_ref, b1_ref, w2_ref, q_ref, ov_ref, ocmp_ref, msel_ref,
     ck_s, cv_s, carry_s, x_s) = refs[CMP_PAGES_PER_STEP:]
    s = pl.program_id(1)
    rows = CMP_PAGES_PER_STEP * R16_PER_PAGE

    @pl.when(s == 0)
    def _():
        carry_s[...] = jnp.zeros_like(carry_s)

    out_row = _iota((PAGE_SIZE, PAGE_SIZE), 0)
    src_tok = (CMP_STRIDE * (out_row & (R16_PER_PAGE - 1))
               + lax.shift_right_logical(out_row, R16_PER_PAGE.bit_length() - 1))
    perm = jnp.where(_iota((PAGE_SIZE, PAGE_SIZE), 1) == src_tok, 1.0, 0.0).astype(BF16)
    for k, p in enumerate(pages):
        x_t = _nt(perm, p[0].astype(BF16))
        for l in range(CMP_STRIDE):
            x_s[l, k * R16_PER_PAGE:(k + 1) * R16_PER_PAGE, :] = (
                x_t[l * R16_PER_PAGE:(l + 1) * R16_PER_PAGE, :])

    rid = _iota((rows, 1), 0)
    for c, dst in ((0, ck_s), (1, cv_s)):
        both = None
        for l in range(CMP_STRIDE):
            x = x_s[l, :, c * GRP_W:(c + 1) * GRP_W].astype(BF16)
            y = _nn(x, w1_ref[c, l])
            both = y if both is None else both + y
        first, second = both[:, :GRP_W], both[:, GRP_W:]
        prev = carry_s[c, SUBLANE - 1:SUBLANE, :]
        shifted = jnp.where(rid == 0, prev, pltpu.roll(first, 1, 0))
        hid = _gelu(shifted + second + b1_ref[c])
        dst[pl.ds(pl.multiple_of(s * rows, rows), rows), :] = _nn(hid.astype(BF16), w2_ref[c])
        carry_s[c] = first[rows - SUBLANE:rows, :]

    @pl.when(s == pl.num_programs(1) - 1)
    def _():
        n_rows = ck_s.shape[0]
        q = q_ref[0]
        t_col = _iota((DEC_ROWS, 1), 0)
        pos = past_len + t_col
        pos4 = jnp.concatenate([pos] * A_REP, axis=0)
        m_lane = _iota((1, n_rows), 1)
        valid = (m_lane >= 1) & ((m_lane - 1) * CMP_STRIDE + (CMP_LEN - 1) <= pos4)
        imps = []
        for g in range(A_KV):
            gs = slice(g * A_DH, (g + 1) * A_DH)
            qg = _stack_heads(q, g).astype(BF16)
            p = _masked_softmax(_nt(qg, ck_s[:, gs].astype(BF16)) * ATTN_SCALE, valid)
            o = _nn(p.astype(BF16), cv_s[:, gs].astype(BF16))
            for r in range(A_REP):
                h = g * A_REP + r
                ocmp_ref[0, :, h * A_DH:(h + 1) * A_DH] = o[r * DEC_ROWS:(r + 1) * DEC_ROWS]
            p_sum = p[0:DEC_ROWS]
            for r in range(1, A_REP):
                p_sum = p_sum + p[r * DEC_ROWS:(r + 1) * DEC_ROWS]
            imps.append(_block_importance(p_sum, ov_ref))
        imp = jnp.concatenate(imps, axis=0)
        lane = _iota((A_KV * DEC_ROWS, n_blk_pad), 1)
        score = _block_scores(imp, pos4, lane)
        mask = _topk_mask(score, lane.astype(F32), N_SEL)
        for j in range(n_blk_pad // LANE):
            msel_ref[0, j] = mask[:, j * LANE:(j + 1) * LANE]


def nsa_decode_compress(cache_t, page_table, w1bd, b1t, w2bd, q8, past_len, n_blk_pad):
    dbsz, n_pages = page_table.shape
    assert n_pages % CMP_PAGES_PER_STEP == 0
    n_steps = n_pages // CMP_PAGES_PER_STEP
    n_rows = n_pages * R16_PER_PAGE
    n_tiles = n_blk_pad // LANE
    ov = _overlap_matrix(n_rows, n_blk_pad, 1)

    def page_spec(k):
        return pl.BlockSpec((1, ROW_W, PAGE_SIZE),
                            lambda b, s, pt: (pt[b, s * CMP_PAGES_PER_STEP + k], 0, 0))

    grid_spec = pltpu.PrefetchScalarGridSpec(
        num_scalar_prefetch=1,
        grid=(dbsz, n_steps),
        in_specs=[page_spec(k) for k in range(CMP_PAGES_PER_STEP)] + [
            pl.BlockSpec(w1bd.shape, lambda b, s, pt: (0, 0, 0, 0)),
            pl.BlockSpec(b1t.shape, lambda b, s, pt: (0, 0, 0)),
            pl.BlockSpec(w2bd.shape, lambda b, s, pt: (0, 0, 0)),
            pl.BlockSpec((1, DEC_ROWS, A_QW), lambda b, s, pt: (b, 0, 0)),
            pl.BlockSpec(ov.shape, lambda b, s, pt: (0, 0))],
        out_specs=(pl.BlockSpec((1, DEC_ROWS, A_QW), lambda b, s, pt: (b, 0, 0)),
                   pl.BlockSpec((1, n_tiles, A_KV * DEC_ROWS, LANE), lambda b, s, pt: (b, 0, 0, 0))),
        scratch_shapes=[pltpu.VMEM((n_rows, GRP_W), F32), pltpu.VMEM((n_rows, GRP_W), F32),
                        pltpu.VMEM((2, SUBLANE, GRP_W), F32),
                        pltpu.VMEM((CMP_STRIDE, CMP_PAGES_PER_STEP * R16_PER_PAGE, ROW_W), F32)])
    body = functools.partial(_dec_cmp_body, past_len=past_len, n_blk_pad=n_blk_pad)
    return pl.pallas_call(
        body,
        out_shape=(jax.ShapeDtypeStruct((dbsz, DEC_ROWS, A_QW), F32),
                   jax.ShapeDtypeStruct((dbsz, n_tiles, A_KV * DEC_ROWS, LANE), F32)),
        grid_spec=grid_spec,
        compiler_params=_params(2), name="nsa_decode_compress",
    )(page_table, *([cache_t] * CMP_PAGES_PER_STEP), w1bd, b1t, w2bd, q8, ov)


def _dec_sel_body(pt_ref, *refs, past_len, t_new):
    del pt_ref
    pages = refs[:SEL_PAGES_PER_STEP]
    (q_ref, msel_ref, ocmp_ref, snew_ref, wpast_ref, wnew_ref, gate_ref, bg_ref,
     o_ref, m_s, l_s, acc_s) = refs[SEL_PAGES_PER_STEP:]
    s = pl.program_id(1)
    keys = SEL_PAGES_PER_STEP * PAGE_SIZE
    blks = keys // SEL_BLK
    steps_per_tile = LANE // blks

    @pl.when(s == 0)
    def _():
        m_s[...] = jnp.full_like(m_s, NEG_INF)
        l_s[...] = jnp.zeros_like(l_s)
        acc_s[...] = jnp.zeros_like(acc_s)

    q = q_ref[0]
    tile = s // steps_per_tile
    first_blk = (s - tile * steps_per_tile) * blks
    expand = (_iota((LANE, keys), 0)
              == first_blk + lax.shift_right_arithmetic(_iota((LANE, keys), 1), SEL_BLK_LOG2))
    expand = jnp.where(expand, 1.0, 0.0).astype(BF16)
    m_keys = _nn(msel_ref[0, tile].astype(BF16), expand)

    for g in range(A_KV):
        gs = slice(g * A_DH, (g + 1) * A_DH)
        vs_ = slice(GRP_W + g * A_DH, GRP_W + (g + 1) * A_DH)
        qg = _stack_heads(q, g).astype(BF16)
        k_t = jnp.concatenate([p[0, gs, :] for p in pages], axis=1).astype(BF16)
        v_t = jnp.concatenate([p[0, vs_, :] for p in pages], axis=1).astype(BF16)
        valid = jnp.concatenate([m_keys[g * DEC_ROWS:(g + 1) * DEC_ROWS]] * A_REP, axis=0) > 0.5
        _online_softmax_step(m_s, l_s, acc_s, g, _nn(qg, k_t) * ATTN_SCALE, valid,
                             lambda e, v_t=v_t: _nt(e, v_t))

    @pl.when(s == pl.num_programs(1) - 1)
    def _():
        t_col = _iota((DEC_ROWS, 1), 0)
        t4 = jnp.concatenate([t_col] * A_REP, axis=0)
        t_key = _iota((1, DEC_ROWS), 1)
        valid_new = (t_key <= t4) & (t_key < t_new)
        gate = jax.nn.sigmoid(gate_ref[0] + bg_ref[...])
        n_win = wpast_ref.shape[2]
        kpos_w = past_len - n_win + _iota((1, n_win), 1)
        diff_w = (past_len + t4) - kpos_w
        valid_wp = (diff_w >= 0) & (diff_w <= WINDOW)
        new_blk = past_len // SEL_BLK
        new_tile, new_lane = new_blk // LANE, new_blk % LANE
        for g in range(A_KV):
            gs = slice(g * A_DH, (g + 1) * A_DH)
            vs_ = slice(GRP_W + g * A_DH, GRP_W + (g + 1) * A_DH)
            qg = _stack_heads(q, g).astype(BF16)
            mnew = msel_ref[0, new_tile, g * DEC_ROWS:(g + 1) * DEC_ROWS, new_lane:new_lane + 1]
            mnew4 = jnp.concatenate([mnew] * A_REP, axis=0)
            v_new = snew_ref[0, :, vs_].astype(BF16)
            _online_softmax_step(m_s, l_s, acc_s, g,
                                 _nt(qg, snew_ref[0, :, gs].astype(BF16)) * ATTN_SCALE,
                                 valid_new & (mnew4 > 0.5), lambda e, v_new=v_new: _nn(e, v_new))
            o_sel = acc_s[g] / jnp.maximum(l_s[g], 1e-30)

            s_p = jnp.where(valid_wp, _nn(qg, wpast_ref[0, gs, :].astype(BF16)) * ATTN_SCALE, NEG_INF)
            s_n = jnp.where(valid_new, _nt(qg, wnew_ref[0, :, gs].astype(BF16)) * ATTN_SCALE, NEG_INF)
            mx = jnp.maximum(jnp.max(s_p, axis=1, keepdims=True), jnp.max(s_n, axis=1, keepdims=True))
            mx = jnp.where(mx == NEG_INF, 0.0, mx)
            e_p = jnp.exp(s_p - mx)
            e_n = jnp.exp(s_n - mx)
            den = jnp.maximum(jnp.sum(e_p, axis=1, keepdims=True)
                              + jnp.sum(e_n, axis=1, keepdims=True), 1e-30)
            o_win = (_nt(e_p.astype(BF16), wpast_ref[0, vs_, :].astype(BF16))
                     + _nn(e_n.astype(BF16), wnew_ref[0, :, vs_].astype(BF16))) / den
            o_cmp = _stack_heads(ocmp_ref[0], g)
            _combine_heads(o_ref.at[0], gate, o_cmp, o_sel, o_win, g, DEC_ROWS)


def nsa_decode_attention(cache_t, page_table, q8, msel, ocmp, sel_new, win_past_t, win_new,
                         gate8, b_gate_row, past_len, t_new):
    dbsz, n_pages = page_table.shape
    keys = SEL_PAGES_PER_STEP * PAGE_SIZE
    assert n_pages % SEL_PAGES_PER_STEP == 0 and past_len % SEL_BLK == 0
    assert LANE % (keys // SEL_BLK) == 0
    n_steps = n_pages // SEL_PAGES_PER_STEP
    n_win = win_past_t.shape[2]

    def page_spec(k):
        return pl.BlockSpec((1, ROW_W, PAGE_SIZE),
                            lambda b, s, pt: (pt[b, s * SEL_PAGES_PER_STEP + k], 0, 0))

    per_b = lambda shp: pl.BlockSpec((1,) + shp, lambda b, s, pt: (b,) + (0,) * len(shp))
    grid_spec = pltpu.PrefetchScalarGridSpec(
        num_scalar_prefetch=1,
        grid=(dbsz, n_steps),
        in_specs=[page_spec(k) for k in range(SEL_PAGES_PER_STEP)] + [
            per_b((DEC_ROWS, A_QW)), per_b(msel.shape[1:]), per_b((DEC_ROWS, A_QW)),
            per_b((DEC_ROWS, ROW_W)), per_b((ROW_W, n_win)), per_b((DEC_ROWS, ROW_W)),
            per_b((DEC_ROWS, LANE)), pl.BlockSpec((1, LANE), lambda b, s, pt: (0, 0))],
        out_specs=per_b((DEC_ROWS, A_QW)),
        scratch_shapes=[pltpu.VMEM((A_KV, A_REP * DEC_ROWS, 1), F32),
                        pltpu.VMEM((A_KV, A_REP * DEC_ROWS, 1), F32),
                        pltpu.VMEM((A_KV, A_REP * DEC_ROWS, A_DH), F32)])
    body = functools.partial(_dec_sel_body, past_len=past_len, t_new=t_new)
    return pl.pallas_call(
        body,
        out_shape=jax.ShapeDtypeStruct((dbsz, DEC_ROWS, A_QW), F32),
        grid_spec=grid_spec,
        compiler_params=_params(2), name="nsa_decode_attention",
    )(page_table, *([cache_t] * SEL_PAGES_PER_STEP), q8, msel, ocmp, sel_new, win_past_t,
      win_new, gate8, b_gate_row)


def tokens_minor(rows):
    n, t = rows.shape[:2]
    return jnp.transpose(rows, (0, 2, 3, 4, 1)).reshape(n, ROW_W, t)


PROMPT_TM = 256
PEER_TM = 512
PEER_TE = 2048
PEER_TOKEN_GROUPS = 1
NSA_TQ = 128


def kernel(x_prompt, x_sample, cache_cmp_kv, cache_sel_kv, state_win_kv, state_C, state_n, state_m,
           page_table, norm_mix, norm_ffn, norm_final, mlstm_w_in, mlstm_b_i, mlstm_b_f,
           mlstm_head_norm, mlstm_w_out, nsa_w_in, nsa_b_gate, nsa_cmp_w1, nsa_cmp_b1, nsa_cmp_w2,
           nsa_w_out, peer_w_q, peer_sub_keys, peer_u, peer_v):
    bsz, t_len, d = x_prompt.shape
    dbsz, t_s, _ = x_sample.shape
    n_pages = page_table.shape[1]
    past_len = n_pages * PAGE_SIZE
    assert norm_mix.shape[0] == 2 and d == D_MODEL and t_s <= DEC_ROWS // 2
    assert t_len % M_CHUNK == 0 and t_len % NSA_TQ == 0 and t_len >= WINDOW

    xp = x_prompt.reshape(bsz * t_len, d)
    reps = DEC_ROWS // t_s
    xs = jnp.concatenate([x_sample] * reps, axis=1).reshape(dbsz * DEC_ROWS, d)
    n_s = dbsz * DEC_ROWS

    def peer(x, layer, tm, out_norm=None):
        keys = peer_sub_keys[layer].astype(BF16)
        return peer_ffn_residual(x, norm_ffn[layer], peer_w_q[layer].astype(BF16),
                                 keys[:, 0], keys[:, 1], peer_u[layer].astype(BF16),
                                 peer_v[layer].T.astype(BF16), tm, PEER_TE, out_norm)

    w_in = jnp.pad(mlstm_w_in[0], ((0, 0), (0, M_PROJ_PAD - M_PROJ))).astype(BF16)
    w_out = mlstm_w_out[0].astype(BF16)
    gate_bias = jnp.pad(jnp.concatenate([mlstm_b_i[0], mlstm_b_f[0]]),
                        (0, LANE - 2 * M_HEADS)).reshape(1, LANE)
    proj_p = rms_matmul(xp, norm_mix[0], w_in, PROMPT_TM).reshape(bsz, t_len, M_PROJ_PAD)
    proj_s = rms_matmul(xs, norm_mix[0], w_in, n_s).reshape(dbsz, DEC_ROWS, M_PROJ_PAD)
    hid_p, c_p, n_p, m_p = mlstm_scan(
        proj_p, jnp.zeros((bsz, M_HEADS, M_DK, M_DV), F32), jnp.zeros((bsz, M_HEADS, M_DK), F32),
        jnp.zeros((bsz, M_HEADS), F32), gate_bias, mlstm_head_norm[0], M_CHUNK, M_CHUNK - 1)
    hid_s, c_s, n_st, m_st = mlstm_scan(
        proj_s, state_C[0], state_n[0], state_m[0], gate_bias, mlstm_head_norm[0],
        DEC_ROWS, t_s - 1)
    xp = matmul_residual(hid_p.reshape(bsz * t_len, d), w_out, xp, PROMPT_TM)
    xs = matmul_residual(hid_s.reshape(n_s, d), w_out, xs, n_s)
    xp = peer(xp, 0, PEER_TM)
    xs = peer(xs, 0, n_s)

    w_in_a = jnp.pad(nsa_w_in[0], ((0, 0), (0, A_PROJ_PAD - A_PROJ))).astype(BF16)
    w_out_a = nsa_w_out[0].astype(BF16)
    b_gate_row = jnp.pad(nsa_b_gate[0].reshape(-1), (0, LANE - 3 * A_HEADS)).reshape(1, LANE)
    w1bd, b1t, w2bd = _compress_weights(nsa_cmp_w1[0], nsa_cmp_b1[0], nsa_cmp_w2[0])
    tabs_p = rope_tables(jnp.arange(t_len, dtype=jnp.int32))
    pos_s = past_len + jnp.arange(DEC_ROWS, dtype=jnp.int32)
    tabs_s = tuple(jnp.tile(a, (dbsz, 1)) for a in rope_tables(pos_s))

    q_p, cmp_p, sel_p, win_p, gate_p = nsa_project(xp, norm_mix[1], w_in_a, tabs_p, PROMPT_TM)
    q_s, cmp_s, sel_s, win_s, gate_s = nsa_project(xs, norm_mix[1], w_in_a, tabs_s, n_s)

    ck, cv = compress_prompt(cmp_p.reshape(bsz, t_len // CMP_STRIDE, CMP_STRIDE * ROW_W),
                             w1bd, b1t, w2bd)
    o_p = nsa_prompt_attention(q_p, gate_p, ck, cv, sel_p, win_p, b_gate_row, bsz, t_len, NSA_TQ)
    xp = matmul_residual(o_p, w_out_a, xp, PROMPT_TM)

    n_blk = -(-(past_len + t_s) // SEL_BLK)
    n_blk_pad = -(-n_blk // LANE) * LANE
    q8 = q_s.reshape(dbsz, DEC_ROWS, A_QW)
    o_cmp, msel = nsa_decode_compress(tokens_minor(cache_cmp_kv[0]), page_table, w1bd, b1t, w2bd,
                                      q8, past_len, n_blk_pad)
    o_s = nsa_decode_attention(
        tokens_minor(cache_sel_kv[0]), page_table, q8, msel, o_cmp,
        sel_s.reshape(dbsz, DEC_ROWS, ROW_W), tokens_minor(state_win_kv[0]),
        win_s.reshape(dbsz, DEC_ROWS, ROW_W), gate_s.reshape(dbsz, DEC_ROWS, LANE), b_gate_row,
        past_len, t_s)
    xs = matmul_residual(o_s.reshape(n_s, d), w_out_a, xs, n_s)
    y_p = peer(xp, 1, PEER_TM, norm_final).reshape(bsz, t_len, d)
    y_s = peer(xs, 1, n_s, norm_final).reshape(dbsz, DEC_ROWS, d)[:, :t_s]

    row_shape = (2, A_KV, A_DH)
    kv_p = lambda a: a.reshape((1, bsz, t_len) + row_shape)
    kv_s = lambda a: a.reshape((dbsz, DEC_ROWS) + row_shape)[None, :, :t_s]
    win_rows = min(WINDOW, t_len)
    win_buf_p = win_p.reshape((bsz, t_len) + row_shape)[None, :, t_len - win_rows:]
    win_new_s = win_s.reshape((dbsz, DEC_ROWS) + row_shape)[:, :t_s]
    win_buf_s = jnp.concatenate([state_win_kv[0], win_new_s], axis=1)[None, :, t_s:]
    return (y_p, y_s,
            kv_p(cmp_p), kv_p(sel_p), win_buf_p,
            c_p[None], n_p[None], m_p[None],
            kv_s(cmp_s), kv_s(sel_s), win_buf_s,
            c_s[None], n_st[None], m_st[None])
```

```python
import functools

import numpy as np
import jax
import jax.numpy as jnp
from jax import lax
from jax.experimental import pallas as pl
from jax.experimental.pallas import tpu as pltpu

F32 = jnp.float32
BF16 = jnp.bfloat16
HI = lax.Precision.HIGHEST

V7X_VMEM_BYTES = 64 * 1024 * 1024
VMEM_LIMIT = V7X_VMEM_BYTES - 8 * 1024 * 1024
LANE = 128
SUBLANE = 8

D_MODEL = 1024
NORM_EPS = 1e-6

M_HEADS = 8
M_DK = 64
M_DV = 128
M_CHUNK = 64
M_PROJ = 2 * M_HEADS * M_DK + 2 * M_HEADS * M_DV + 2 * M_HEADS
M_PROJ_PAD = 3200

A_HEADS = 16
A_KV = 4
A_REP = 4
A_DH = 64
ROT_DIM = 16
ROPE_THETA = 500000.0
CMP_STRIDE = 16
CMP_LEN = 32
SEL_BLK = 64
SEL_BLK_LOG2 = 6
N_SEL = 16
WINDOW = 512
A_QW = 1024
A_KVW = 1536
A_PROJ = A_QW + A_KVW + 3 * A_HEADS
A_PROJ_PAD = 2688
ATTN_SCALE = A_DH ** -0.5
GRP_W = A_KV * A_DH
ROW_W = 2 * GRP_W
PAGE_SIZE = 128

P_HEADS = 8
P_NKEYS = 128
P_EXPERTS = P_NKEYS * P_NKEYS
P_QDIM = 256
P_TOPK = 16

NEG_INF = float("-inf")
LOG2E = 1.4426950408889634
MASKED = -(2.0 ** 100)


def _params(n_axes):
    return pltpu.CompilerParams(dimension_semantics=("arbitrary",) * n_axes,
                                vmem_limit_bytes=VMEM_LIMIT)


def _nn(a, b, precision=None):
    return jnp.dot(a, b, preferred_element_type=F32, precision=precision)


def _nt(a, b, precision=None):
    return lax.dot_general(a, b, (((1,), (1,)), ((), ())), preferred_element_type=F32,
                           precision=precision)


def _tn(a, b, precision=None):
    return lax.dot_general(a, b, (((0,), (0,)), ((), ())), preferred_element_type=F32,
                           precision=precision)


def _gelu(x):
    return 0.5 * x * (1.0 + lax.erf(x * (2.0 ** -0.5)))


def _rms(x, g):
    return x * lax.rsqrt(jnp.mean(x * x, axis=-1, keepdims=True) + NORM_EPS) * g


def _masked_softmax(s, valid):
    s = jnp.where(valid, s, NEG_INF)
    m = jnp.max(s, axis=-1, keepdims=True)
    m = jnp.where(m == NEG_INF, 0.0, m)
    e = jnp.exp(s - m)
    return e / jnp.maximum(jnp.sum(e, axis=-1, keepdims=True), 1e-30)


def _iota(shape, dim):
    return lax.broadcasted_iota(jnp.int32, shape, dim)


def _rms_mm_body(x_ref, g_ref, w_ref, o_ref):
    o_ref[...] = _nn(_rms(x_ref[...], g_ref[...]).astype(BF16), w_ref[...])


def rms_matmul(x, g, w, tm):
    n, d = x.shape
    nout = w.shape[1]
    return pl.pallas_call(
        _rms_mm_body,
        out_shape=jax.ShapeDtypeStruct((n, nout), F32),
        grid=(n // tm,),
        in_specs=[pl.BlockSpec((tm, d), lambda i: (i, 0)),
                  pl.BlockSpec((1, d), lambda i: (0, 0)),
                  pl.BlockSpec((d, nout), lambda i: (0, 0))],
        out_specs=pl.BlockSpec((tm, nout), lambda i: (i, 0)),
        compiler_params=_params(1), name="rms_matmul",
    )(x, g.reshape(1, d), w)


def _mm_res_body(a_ref, w_ref, x_ref, o_ref):
    o_ref[...] = x_ref[...] + _nn(a_ref[...].astype(BF16), w_ref[...])


def matmul_residual(a, w, x, tm):
    n, k = a.shape
    d = w.shape[1]
    return pl.pallas_call(
        _mm_res_body,
        out_shape=jax.ShapeDtypeStruct((n, d), F32),
        grid=(n // tm,),
        in_specs=[pl.BlockSpec((tm, k), lambda i: (i, 0)),
                  pl.BlockSpec((k, d), lambda i: (0, 0)),
                  pl.BlockSpec((tm, d), lambda i: (i, 0))],
        out_specs=pl.BlockSpec((tm, d), lambda i: (i, 0)),
        compiler_params=_params(1), name="matmul_residual",
    )(a, w, x)


def _nsa_proj_body(x_ref, g_ref, w_ref, c_ref, sa_ref, sb_ref,
                   q_ref, cmp_ref, sel_ref, win_ref, gate_ref):
    y = _nn(_rms(x_ref[...], g_ref[...]).astype(BF16), w_ref[...])
    c = c_ref[...]
    sa = sa_ref[...]
    sb = sb_ref[...]

    def rot(z):
        return (z * c + pltpu.roll(z, GRP_W - ROT_DIM // 2, 1) * sa
                + pltpu.roll(z, ROT_DIM // 2, 1) * sb)

    for j in range(A_QW // GRP_W):
        q_ref[:, j * GRP_W:(j + 1) * GRP_W] = rot(y[:, j * GRP_W:(j + 1) * GRP_W])
    for ref, base in ((cmp_ref, A_QW), (sel_ref, A_QW + ROW_W), (win_ref, A_QW + 2 * ROW_W)):
        ref[:, 0:GRP_W] = rot(y[:, base:base + GRP_W])
        ref[:, GRP_W:ROW_W] = y[:, base + GRP_W:base + ROW_W]
    gate_ref[...] = y[:, A_QW + A_KVW:A_PROJ_PAD]


def nsa_project(x, g, w, tabs, tm):
    n, d = x.shape
    n_tab = tabs[0].shape[0] // tm
    tab_spec = pl.BlockSpec((tm, GRP_W), lambda i: (i % n_tab, 0))
    row = lambda w_: pl.BlockSpec((tm, w_), lambda i: (i, 0))
    return pl.pallas_call(
        _nsa_proj_body,
        out_shape=(jax.ShapeDtypeStruct((n, A_QW), F32),
                   jax.ShapeDtypeStruct((n, ROW_W), F32),
                   jax.ShapeDtypeStruct((n, ROW_W), F32),
                   jax.ShapeDtypeStruct((n, ROW_W), F32),
                   jax.ShapeDtypeStruct((n, LANE), F32)),
        grid=(n // tm,),
        in_specs=[row(d), pl.BlockSpec((1, d), lambda i: (0, 0)),
                  pl.BlockSpec((d, A_PROJ_PAD), lambda i: (0, 0)),
                  tab_spec, tab_spec, tab_spec],
        out_specs=(row(A_QW), row(ROW_W), row(ROW_W), row(ROW_W), row(LANE)),
        compiler_params=_params(1), name="nsa_project",
    )(x, g.reshape(1, d), w, *tabs)


def rope_tables(pos):
    half = ROT_DIM // 2
    inv_freq = ROPE_THETA ** (-(jnp.arange(half, dtype=F32) * (2.0 / ROT_DIM)))
    ang = pos.astype(F32)[:, None] * inv_freq[None, :]
    cos, sin = jnp.cos(ang), jnp.sin(ang)
    t = pos.shape[0]
    one = jnp.ones((t, A_DH - ROT_DIM), F32)
    zero = jnp.zeros((t, A_DH - ROT_DIM), F32)
    z8 = jnp.zeros((t, half), F32)
    c = jnp.concatenate([cos, cos, one], axis=1)
    sa = jnp.concatenate([-sin, z8, zero], axis=1)
    sb = jnp.concatenate([z8, sin, zero], axis=1)
    return tuple(jnp.tile(a, (1, A_KV)) for a in (c, sa, sb))


def _mlstm_body(q_ref, k_ref, v_ref, og_ref, gt_ref, c0_ref, n0_ref, m0_ref, gb_ref, hn_ref,
                hid_ref, c_out, n_out, m_out, c_s, n_s, m_s, *, chunk, t_last, nb):
    ci = pl.program_id(1)

    @pl.when(ci == 0)
    def _():
        c_s[...] = c0_ref[...]
        n_s[...] = n0_ref[...]
        m_s[...] = m0_ref[...]

    L = chunk
    row = _iota((L, L), 0)
    col = _iota((L, L), 1)
    causal = row >= col
    tril = causal.astype(F32)
    eye8 = (_iota((SUBLANE, LANE), 0) == _iota((SUBLANE, LANE), 1)).astype(F32)
    rlane = _iota((L, LANE), 0)
    lane1 = _iota((1, LANE), 1)
    sel_row = _iota((LANE, LANE), 0)
    ones_l = jnp.ones((L, LANE), BF16)
    ones_k = jnp.ones((M_DK, LANE), BF16)
    ones_v = jnp.ones((M_DV, LANE), BF16)

    def replicate(parts, col_idx):
        sel = jnp.where(sel_row == col_idx, 1.0, 0.0).astype(BF16)
        return _nn(parts[0], sel) + _nn(parts[1], sel) + _nn(parts[2], sel)

    units = [(bi, h) for bi in range(nb) for h in range(M_HEADS)]
    b_parts, g_parts, m_parts, g_rows = [], [], [], []
    for bi in range(nb):
        gt = gt_ref[bi] + gb_ref[...]
        log_f = jnp.minimum(gt, 0.0) - jnp.log1p(jnp.exp(-jnp.abs(gt)))
        ba = _nn(tril, log_f, HI)
        g_mat = gt - pltpu.roll(ba, LANE - M_HEADS, 1)
        b_parts.append(_split3(ba))
        g_parts.append(_split3(g_mat))
        m_parts.append(_split3(jnp.broadcast_to(m_s[bi], (SUBLANE, LANE))))
        g_rows.append(_nt(eye8, g_mat, HI))

    qs, ks, vs, cs, ns = {}, {}, {}, {}, {}
    s_raw, q_c, b_rep, g_rep, m_prev = {}, {}, {}, {}, {}
    for u in units:
        bi, h = u
        qs[u] = q_ref[bi, :, h * M_DK:(h + 1) * M_DK]
        ks[u] = k_ref[bi, :, h * M_DK:(h + 1) * M_DK] * (M_DK ** -0.5)
        vs[u] = v_ref[bi, :, h * M_DV:(h + 1) * M_DV]
        cs[u] = c_s[bi, h]
        ns[u] = n_s[bi, h:h + 1, :]
        s_raw[u] = _nt(qs[u].astype(BF16), ks[u].astype(BF16))
        q_c[u] = _nn(qs[u].astype(BF16), cs[u].astype(BF16))
        b_rep[u] = replicate(b_parts[bi], M_HEADS + h)
        g_rep[u] = replicate(g_parts[bi], h)
        m_prev[u] = replicate(m_parts[bi], h)[0:1, :]

    a_rep, m_t, w_d, w_a = {}, {}, {}, {}
    for u in units:
        bi, h = u
        pm = g_rep[u]
        shift = 1
        while shift < L:
            pm = jnp.where(rlane >= shift, jnp.maximum(pm, pltpu.roll(pm, shift, 0)), pm)
            shift *= 2
        a_rep[u] = b_rep[u] + m_prev[u]
        m_t[u] = jnp.maximum(a_rep[u], b_rep[u] + pm)
        dm = jnp.where(causal, b_rep[u][:, :L] + g_rows[bi][h:h + 1, :], NEG_INF)
        w_d[u] = jnp.exp(dm - m_t[u][:, :L])
        w_a[u] = jnp.exp(a_rep[u] - m_t[u])

    for u in units:
        bi, h = u
        s = s_raw[u] * w_d[u]
        s_b = s.astype(BF16)
        num = w_a[u] * q_c[u] + _nn(s_b, vs[u].astype(BF16))
        den = w_a[u] * _nn((qs[u] * ns[u]).astype(BF16), ones_k) + _nn(s_b, ones_l)
        h_out = num / jnp.maximum(jnp.abs(den), jnp.exp(-m_t[u]))
        msq = _nn((h_out * h_out).astype(BF16), ones_v) * (1.0 / M_DV)
        hid = h_out * lax.rsqrt(msq + NORM_EPS) * hn_ref[:, h * M_DV:(h + 1) * M_DV]
        hid_ref[bi, :, h * M_DV:(h + 1) * M_DV] = (
            jax.nn.sigmoid(og_ref[bi, :, h * M_DV:(h + 1) * M_DV]) * hid)

    m_rows = [m_s[bi] for bi in range(nb)]
    for u in units:
        bi, h = u
        m_new = m_t[u][t_last:t_last + 1, :]
        d_last = jnp.where(rlane <= t_last, b_rep[u][t_last:t_last + 1, :] + g_rep[u], NEG_INF)
        w_last = jnp.exp(d_last - m_new)
        decay = jnp.exp(a_rep[u][t_last:t_last + 1, :] - m_new)
        kw = ks[u] * w_last[:, :M_DK]
        c_s[bi, h] = decay * cs[u] + _tn(kw, vs[u], HI)
        n_s[bi, h:h + 1, :] = decay[:, :M_DK] * ns[u] + jnp.sum(kw, axis=0, keepdims=True)
        m_rows[bi] = jnp.where(lane1 == h, m_new, m_rows[bi])
    for bi in range(nb):
        m_s[bi] = m_rows[bi]

    @pl.when(ci == pl.num_programs(1) - 1)
    def _():
        c_out[...] = c_s[...]
        n_out[...] = n_s[...]
        m_out[...] = m_s[...]


MLSTM_SEQ_PER_STEP = 4


def mlstm_scan(proj, c0, n0, m0, gate_bias, head_norm, chunk, t_last):
    b, t, _ = proj.shape
    nb = MLSTM_SEQ_PER_STEP
    assert b % nb == 0
    n_chunks = t // chunk
    hd = M_HEADS * M_DK
    vd = M_HEADS * M_DV
    m0p = jnp.pad(m0, ((0, 0), (0, LANE - M_HEADS))).reshape(b, 1, LANE)
    body = functools.partial(_mlstm_body, chunk=chunk, t_last=t_last, nb=nb)
    hid, c_t, n_t, m_t = pl.pallas_call(
        body,
        out_shape=(jax.ShapeDtypeStruct((b, t, vd), F32),
                   jax.ShapeDtypeStruct((b, M_HEADS, M_DK, M_DV), F32),
                   jax.ShapeDtypeStruct((b, M_HEADS, M_DK), F32),
                   jax.ShapeDtypeStruct((b, 1, LANE), F32)),
        grid=(b // nb, n_chunks),
        in_specs=[pl.BlockSpec((nb, chunk, hd), lambda i, c: (i, c, 0)),
                  pl.BlockSpec((nb, chunk, hd), lambda i, c: (i, c, 1)),
                  pl.BlockSpec((nb, chunk, vd), lambda i, c: (i, c, 1)),
                  pl.BlockSpec((nb, chunk, vd), lambda i, c: (i, c, 2)),
                  pl.BlockSpec((nb, chunk, LANE), lambda i, c: (i, c, (2 * hd + 2 * vd) // LANE)),
                  pl.BlockSpec((nb, M_HEADS, M_DK, M_DV), lambda i, c: (i, 0, 0, 0)),
                  pl.BlockSpec((nb, M_HEADS, M_DK), lambda i, c: (i, 0, 0)),
                  pl.BlockSpec((nb, 1, LANE), lambda i, c: (i, 0, 0)),
                  pl.BlockSpec((1, LANE), lambda i, c: (0, 0)),
                  pl.BlockSpec((1, vd), lambda i, c: (0, 0))],
        out_specs=(pl.BlockSpec((nb, chunk, vd), lambda i, c: (i, c, 0)),
                   pl.BlockSpec((nb, M_HEADS, M_DK, M_DV), lambda i, c: (i, 0, 0, 0)),
                   pl.BlockSpec((nb, M_HEADS, M_DK), lambda i, c: (i, 0, 0)),
                   pl.BlockSpec((nb, 1, LANE), lambda i, c: (i, 0, 0))),
        scratch_shapes=[pltpu.VMEM((nb, M_HEADS, M_DK, M_DV), F32),
                        pltpu.VMEM((nb, M_HEADS, M_DK), F32),
                        pltpu.VMEM((nb, 1, LANE), F32)],
        compiler_params=_params(2), name="mlstm_scan",
    )(proj, proj, proj, proj, proj, c0, n0, m0p, gate_bias, head_norm.reshape(1, vd))
    return hid, c_t, n_t, m_t[:, 0, :M_HEADS]


N_CAND = (2 + SUBLANE) * SUBLANE
RANK_OUT = float(P_TOPK)


def _top_rows(xs, dsts, k, with_rank):
    xs = list(xs)
    ranks = [jnp.full(x.shape, RANK_OUT, F32) if w else None for x, w in zip(xs, with_rank)]
    for r in range(k):
        for i, dst in enumerate(dsts):
            m = jnp.max(xs[i], axis=1, keepdims=True)
            dst[:, r:r + 1, :] = m
            hit = xs[i] == m
            if ranks[i] is not None:
                ranks[i] = jnp.where(hit, float(r), ranks[i])
            xs[i] = jnp.where(hit, NEG_INF, xs[i])
    return ranks


def _peer_select_body(x_ref, g_ref, wq_ref, k1_ref, k2_ref, r2_ref, e2_ref, l_ref, c_ref,
                      s1_s, s2_s, v1_s, v2_s, cand_s, vals_s):
    qp = _nn(_rms(x_ref[...], g_ref[...]).astype(BF16), wq_ref[...])
    for h in range(P_HEADS):
        q1 = qp[:, h * P_QDIM:h * P_QDIM + P_QDIM // 2].astype(BF16)
        q2 = qp[:, h * P_QDIM + P_QDIM // 2:(h + 1) * P_QDIM].astype(BF16)
        s1_s[h] = _nt(k1_ref[h], q1)
        s2_s[h] = _nt(k2_ref[h], q2)
    s1 = s1_s[...]
    s2 = s2_s[...]
    _, rank2 = _top_rows((s1, s2), (v1_s, v2_s), P_TOPK, (False, True))
    v2_lo = v2_s[:, 0:SUBLANE, :]
    cand_s[:, 0:SUBLANE, :] = v2_lo + v1_s[:, 0:1, :]
    cand_s[:, SUBLANE:2 * SUBLANE, :] = v2_s[:, SUBLANE:2 * SUBLANE, :] + v1_s[:, 0:1, :]
    cand_s[:, 2 * SUBLANE:3 * SUBLANE, :] = v1_s[:, SUBLANE:2 * SUBLANE, :] + v2_s[:, 0:1, :]
    for r in range(1, SUBLANE):
        cand_s[:, (2 + r) * SUBLANE:(3 + r) * SUBLANE, :] = v2_lo + v1_s[:, r:r + 1, :]
    _top_rows((cand_s[...],), (vals_s,), P_TOPK, (False,))
    top = vals_s[:, 0:1, :]
    tau = vals_s[:, P_TOPK - 1:P_TOPK, :]
    z = jnp.sum(jnp.exp(vals_s[...] - top), axis=1, keepdims=True)
    v1 = v1_s[...]
    kept = jnp.zeros(v1.shape, F32)
    for r in range(P_TOPK):
        kept = kept + jnp.where(v2_s[:, r:r + 1, :] + v1 >= tau, 1.0, 0.0)
    n_keep = jnp.zeros(s1.shape, F32)
    for r in range(P_TOPK):
        n_keep = jnp.where(s1 == v1_s[:, r:r + 1, :], kept[:, r:r + 1, :], n_keep)
    r2_ref[...] = rank2.astype(BF16)
    e2_ref[...] = jnp.exp(s2 - v2_s[:, 0:1, :]).astype(BF16)
    l_ref[...] = n_keep
    c_ref[...] = 0.5 * jnp.exp(s1 - v1_s[:, 0:1, :]) / z


def peer_select(x, g, w_q, k1, k2, tm):
    n, d = x.shape
    spec = pl.BlockSpec((P_HEADS, P_NKEYS, tm), lambda i: (0, 0, i))
    key_spec = pl.BlockSpec((P_HEADS, P_NKEYS, P_QDIM // 2), lambda i: (0, 0, 0))
    shp = lambda dt: jax.ShapeDtypeStruct((P_HEADS, P_NKEYS, n), dt)
    return pl.pallas_call(
        _peer_select_body,
        out_shape=(shp(BF16), shp(BF16), shp(F32), shp(F32)),
        grid=(n // tm,),
        in_specs=[pl.BlockSpec((tm, d), lambda i: (i, 0)),
                  pl.BlockSpec((1, d), lambda i: (0, 0)),
                  pl.BlockSpec((d, P_HEADS * P_QDIM), lambda i: (0, 0)),
                  key_spec, key_spec],
        out_specs=(spec, spec, spec, spec),
        scratch_shapes=[pltpu.VMEM((P_HEADS, P_NKEYS, tm), F32), pltpu.VMEM((P_HEADS, P_NKEYS, tm), F32),
                        pltpu.VMEM((P_HEADS, P_TOPK, tm), F32), pltpu.VMEM((P_HEADS, P_TOPK, tm), F32),
                        pltpu.VMEM((P_HEADS, N_CAND, tm), F32), pltpu.VMEM((P_HEADS, P_TOPK, tm), F32)],
        compiler_params=_params(1), name="peer_select",
    )(x, g.reshape(1, d), w_q, k1, k2)


def _peer_main_body(x_ref, g_ref, og_ref, u_ref, vt_ref, r2_ref, e2_ref, l_ref, c_ref,
                    o_ref, hn_s, acc_s, *, a_tile, norm_out):
    t = pl.program_id(1)

    @pl.when(t == 0)
    def _():
        hn_s[...] = _rms(x_ref[...], g_ref[...]).astype(BF16)
        acc_s[...] = jnp.zeros_like(acc_s)

    tm = hn_s.shape[0]
    zero = jnp.zeros((P_NKEYS, tm), BF16)
    a_sub = PEER_SUB_EXPERTS // P_NKEYS
    total = None
    for sub in range(a_tile // a_sub):
        es = slice(sub * PEER_SUB_EXPERTS, (sub + 1) * PEER_SUB_EXPERTS)
        sc = _nt(u_ref[es, :], hn_s[...])
        act = (sc + sc * lax.erf(sc * (2.0 ** -0.5))).astype(BF16)
        parts = []
        for k in range(a_sub):
            aa = sub * a_sub + k
            w = None
            for h in range(P_HEADS):
                keep = jnp.broadcast_to(l_ref[h, aa:aa + 1, :], (P_NKEYS, tm)).astype(BF16)
                coef = jnp.broadcast_to(c_ref[h, aa:aa + 1, :], (P_NKEYS, tm)).astype(BF16)
                term = jnp.where(r2_ref[h] < keep, e2_ref[h], zero) * coef
                w = term if w is None else w + term
            parts.append(w * act[k * P_NKEYS:(k + 1) * P_NKEYS])
        part = _nn(vt_ref[:, es], jnp.concatenate(parts, axis=0))
        total = part if total is None else total + part
    acc_s[...] += total

    @pl.when(t == pl.num_programs(1) - 1)
    def _():
        y = x_ref[...] + acc_s[...].T
        o_ref[...] = _rms(y, og_ref[...]) if norm_out else y


def peer_main(x, g, u, vt, sel, tm, te, out_norm=None):
    n, d = x.shape
    og = g if out_norm is None else out_norm
    r2, e2, n_keep, coef = sel
    a_tile = te // P_NKEYS
    n_tiles = P_EXPERTS // te
    assert a_tile % SUBLANE == 0
    full_spec = pl.BlockSpec((P_HEADS, P_NKEYS, tm), lambda i, t: (0, 0, i))
    row_spec = pl.BlockSpec((P_HEADS, a_tile, tm), lambda i, t: (0, t, i))
    body = functools.partial(_peer_main_body, a_tile=a_tile, norm_out=out_norm is not None)
    return pl.pallas_call(
        body,
        out_shape=jax.ShapeDtypeStruct((n, d), F32),
        grid=(n // tm, n_tiles),
        in_specs=[pl.BlockSpec((tm, d), lambda i, t: (i, 0)),
                  pl.BlockSpec((1, d), lambda i, t: (0, 0)),
                  pl.BlockSpec((1, d), lambda i, t: (0, 0)),
                  pl.BlockSpec((te, d), lambda i, t: (t, 0)),
                  pl.BlockSpec((d, te), lambda i, t: (0, t)),
                  full_spec, full_spec, row_spec, row_spec],
        out_specs=pl.BlockSpec((tm, d), lambda i, t: (i, 0)),
        scratch_shapes=[pltpu.VMEM((tm, d), BF16), pltpu.VMEM((d, tm), F32)],
        compiler_params=_params(2), name="peer_main",
    )(x, g.reshape(1, d), og.reshape(1, d), u, vt, r2, e2, n_keep, coef)


def peer_ffn_residual(x, g, w_q, k1, k2, u, vt, tm, te, out_norm=None):
    sel = peer_select(x, g, w_q, k1, k2, tm)
    return peer_main(x, g, u, vt, sel, tm, te, out_norm)


def _topk_mask(score, lane_f, k):
    sel = jnp.zeros(score.shape, F32)
    for _ in range(k):
        m = jnp.max(score, axis=1, keepdims=True)
        idx = jnp.min(jnp.where(score == m, lane_f, 1e9), axis=1, keepdims=True)
        hit = lane_f == idx
        sel = jnp.where(hit, 1.0, sel)
        score = jnp.where(hit, NEG_INF, score)
    return sel


def _block_scores(imp, pos_col, lane):
    cur = lax.shift_right_arithmetic(pos_col, SEL_BLK_LOG2)
    forced = (lane == 0) | (lane == cur) | (lane == cur - 1)
    return jnp.where(forced, jnp.inf, jnp.where(lane <= cur, imp, NEG_INF))


def _stack_heads(q, g):
    return jnp.concatenate(
        [q[:, (g * A_REP + r) * A_DH:(g * A_REP + r + 1) * A_DH] for r in range(A_REP)], axis=0)


def _combine_heads(o_ref, gate, o_cmp, o_sel, o_win, g, rows):
    for r in range(A_REP):
        h = g * A_REP + r
        rs = slice(r * rows, (r + 1) * rows)
        o_ref[:, h * A_DH:(h + 1) * A_DH] = (gate[:, 3 * h:3 * h + 1] * o_cmp[rs]
                                             + gate[:, 3 * h + 1:3 * h + 2] * o_sel[rs]
                                             + gate[:, 3 * h + 2:3 * h + 3] * o_win[rs])


def _online_softmax_step(m_ref, l_ref, acc_ref, g, sc, valid, pv):
    sc = jnp.where(valid, sc, NEG_INF)
    m_old = m_ref[g]
    m_new = jnp.maximum(m_old, jnp.max(sc, axis=1, keepdims=True))
    m_safe = jnp.where(m_new == NEG_INF, 0.0, m_new)
    alpha = jnp.exp(m_old - m_safe)
    e = jnp.exp(sc - m_safe)
    l_ref[g] = alpha * l_ref[g] + jnp.sum(e, axis=1, keepdims=True)
    acc_ref[g] = alpha * acc_ref[g] + pv(e.astype(BF16))
    m_ref[g] = m_new


def _compress_weights(w1, b1, w2):
    eye = jnp.eye(A_KV, dtype=F32)
    w1bd = jnp.einsum("cldh,gk->clgdkh", w1, eye).reshape(2, CMP_LEN, GRP_W, GRP_W).astype(BF16)
    w1cat = jnp.concatenate([w1bd[:, :CMP_STRIDE], w1bd[:, CMP_STRIDE:]], axis=-1)
    w2bd = jnp.einsum("chd,gk->cghkd", w2, eye).reshape(2, GRP_W, GRP_W).astype(BF16)
    b1t = jnp.tile(b1, (1, A_KV)).reshape(2, 1, GRP_W)
    return w1cat, b1t, w2bd


def _split3(x):
    hi = x.astype(BF16)
    r1 = x - hi.astype(F32)
    mid = r1.astype(BF16)
    lo = (r1 - mid.astype(F32)).astype(BF16)
    return hi, mid, lo


def _block_importance(p_sum, ov_ref):
    hi, mid, lo = _split3(p_sum)
    ov = ov_ref[...]
    return _nn(hi, ov) + _nn(mid, ov) + _nn(lo, ov)


def _overlap_matrix(n_rows, n_cols, shift):
    c_start = (np.arange(n_rows) - shift) * CMP_STRIDE
    s_start = np.arange(n_cols) * SEL_BLK
    ov = np.clip(np.minimum(c_start[:, None] + CMP_LEN, s_start[None, :] + SEL_BLK)
                 - np.maximum(c_start[:, None], s_start[None, :]), 0, None) / CMP_STRIDE
    ov[c_start < 0] = 0.0
    return jnp.asarray(ov, BF16)


def _cmp_prompt_body(r_ref, w1_ref, b1_ref, w2_ref, ck_ref, cv_ref):
    for c, out_ref in ((0, ck_ref), (1, cv_ref)):
        both = None
        for l in range(CMP_STRIDE):
            lo = l * ROW_W + c * GRP_W
            y = _nn(r_ref[0, :, lo:lo + GRP_W].astype(BF16), w1_ref[c, l])
            both = y if both is None else both + y
        first, second = both[:, :GRP_W], both[:, GRP_W:]
        n = first.shape[0]
        hid = _gelu(first + pltpu.roll(second, n - 1, 0) + b1_ref[c])
        out_ref[0] = _nn(hid.astype(BF16), w2_ref[c])


def compress_prompt(rows16, w1bd, b1t, w2bd):
    b, n, w = rows16.shape
    out = jax.ShapeDtypeStruct((b, n, GRP_W), F32)
    return pl.pallas_call(
        _cmp_prompt_body,
        out_shape=(out, out),
        grid=(b,),
        in_specs=[pl.BlockSpec((1, n, w), lambda i: (i, 0, 0)),
                  pl.BlockSpec(w1bd.shape, lambda i: (0, 0, 0, 0)),
                  pl.BlockSpec(b1t.shape, lambda i: (0, 0, 0)),
                  pl.BlockSpec(w2bd.shape, lambda i: (0, 0, 0))],
        out_specs=(pl.BlockSpec((1, n, GRP_W), lambda i: (i, 0, 0)),
                   pl.BlockSpec((1, n, GRP_W), lambda i: (i, 0, 0))),
        compiler_params=_params(1), name="compress_prompt",
    )(rows16, w1bd, b1t, w2bd)


NSA_KEY_CHUNK = 512


def _nsa_prompt_body(q_ref, gate_ref, ck_ref, cv_ref, sel_ref, win_ref, ov_ref, ex_ref, bg_ref,
                     o_ref, mask_s, m_s, l_s, acc_s, ocmp_s, *, tq, t_len):
    i = pl.program_id(1)
    q0 = i * tq
    q = q_ref[...] * (ATTN_SCALE * LOG2E)
    pos = q0 + _iota((tq, 1), 0)
    pos4 = jnp.concatenate([pos] * A_REP, axis=0)
    n_cmp_rows = ck_ref.shape[1]
    n_lane = _iota((1, n_cmp_rows), 1)
    valid_cmp = (n_lane * CMP_STRIDE + (CMP_LEN - 1) <= pos) & (n_lane < n_cmp_rows - 1)
    bias_cmp = jnp.where(valid_cmp, 0.0, NEG_INF)
    qgs = [_stack_heads(q, g).astype(BF16) for g in range(A_KV)]

    def add_bias(s, bias):
        return jnp.concatenate([s[r * tq:(r + 1) * tq] + bias for r in range(A_REP)], axis=0)

    imps = []
    for g in range(A_KV):
        gs = slice(g * A_DH, (g + 1) * A_DH)
        s_c = add_bias(_nt(qgs[g], ck_ref[0, :, gs].astype(BF16)), bias_cmp)
        m_c = jnp.max(s_c, axis=1, keepdims=True)
        e_c = jnp.exp2(s_c - jnp.where(m_c == NEG_INF, 0.0, m_c))
        p_cmp = e_c / jnp.maximum(jnp.sum(e_c, axis=1, keepdims=True), 1e-30)
        ocmp_s[g] = _nn(p_cmp.astype(BF16), cv_ref[0, :, gs].astype(BF16))
        p_sum = p_cmp[0:tq]
        for r in range(1, A_REP):
            p_sum = p_sum + p_cmp[r * tq:(r + 1) * tq]
        imps.append(_block_importance(p_sum, ov_ref))
    blk_lane = _iota((A_KV * tq, LANE), 1)
    score = _block_scores(jnp.concatenate(imps, axis=0), pos4, blk_lane)
    few_blocks = q0 + tq <= N_SEL * SEL_BLK

    @pl.when(few_blocks)
    def _():
        mask_s[...] = jnp.where(score == NEG_INF, 0.0, 1.0)

    @pl.when(jnp.logical_not(few_blocks))
    def _():
        mask_s[...] = _topk_mask(score, blk_lane.astype(F32), N_SEL)

    m_s[...] = jnp.full_like(m_s, NEG_INF)
    l_s[...] = jnp.zeros_like(l_s)
    acc_s[...] = jnp.zeros_like(acc_s)
    kc = NSA_KEY_CHUNK
    for c in range(t_len // kc):
        @pl.when(c * kc < q0 + tq)
        def _(c=c):
            key_lane = c * kc + _iota((1, kc), 1)
            causal_bias = jnp.where(key_lane <= pos, 0.0, MASKED)
            for g in range(A_KV):
                blk_bias = _nn((mask_s[g * tq:(g + 1) * tq, :] - 1.0).astype(BF16),
                               ex_ref[:, c * kc:(c + 1) * kc])
                k = sel_ref[c * kc:(c + 1) * kc, g * A_DH:(g + 1) * A_DH].astype(BF16)
                v = sel_ref[c * kc:(c + 1) * kc,
                            GRP_W + g * A_DH:GRP_W + (g + 1) * A_DH].astype(BF16)
                sc = add_bias(_nt(qgs[g], k), blk_bias + causal_bias)
                m_old = m_s[g]
                m_new = jnp.maximum(m_old, jnp.max(sc, axis=1, keepdims=True))
                alpha = jnp.exp2(m_old - m_new)
                e = jnp.exp2(sc - m_new)
                l_s[g] = alpha * l_s[g] + jnp.sum(e, axis=1, keepdims=True)
                acc_s[g] = alpha * acc_s[g] + _nn(e.astype(BF16), v)
                m_s[g] = m_new

    gate = jax.nn.sigmoid(gate_ref[...] + bg_ref[...])
    band = WINDOW + tq
    w_start = pl.multiple_of(jnp.maximum(q0 - WINDOW, 0), tq)
    kpos_w = w_start + _iota((1, band), 1)
    diff_w = pos - kpos_w
    bias_w = jnp.where((diff_w >= 0) & (diff_w <= WINDOW), 0.0, MASKED)
    for g in range(A_KV):
        gs = slice(g * A_DH, (g + 1) * A_DH)
        vs_ = slice(GRP_W + g * A_DH, GRP_W + (g + 1) * A_DH)
        kw = win_ref[pl.ds(w_start, band), gs].astype(BF16)
        vw = win_ref[pl.ds(w_start, band), vs_].astype(BF16)
        s_w = add_bias(_nt(qgs[g], kw), bias_w)
        e_w = jnp.exp2(s_w - jnp.max(s_w, axis=1, keepdims=True))
        o_win = _nn(e_w.astype(BF16), vw) / jnp.sum(e_w, axis=1, keepdims=True)
        o_sel = acc_s[g] / jnp.maximum(l_s[g], 1e-30)
        _combine_heads(o_ref, gate, ocmp_s[g], o_sel, o_win, g, tq)


def nsa_prompt_attention(q, gate_pre, ck, cv, sel_rows, win_rows, b_gate_row, bsz, t_len, tq):
    n = q.shape[0]
    n_q = t_len // tq
    n_blk = t_len // SEL_BLK
    assert t_len % NSA_KEY_CHUNK == 0 and n_blk <= LANE
    ov = jnp.pad(_overlap_matrix(ck.shape[1], n_blk, 0), ((0, 0), (0, LANE - n_blk)))
    ex = (np.arange(LANE)[:, None] == (np.arange(t_len)[None, :] // SEL_BLK))
    ex = jnp.asarray(ex * -MASKED, BF16)
    rows4 = A_REP * tq
    body = functools.partial(_nsa_prompt_body, tq=tq, t_len=t_len)
    return pl.pallas_call(
        body,
        out_shape=jax.ShapeDtypeStruct((n, A_QW), F32),
        grid=(bsz, n_q),
        in_specs=[pl.BlockSpec((tq, A_QW), lambda b, i: (b * n_q + i, 0)),
                  pl.BlockSpec((tq, LANE), lambda b, i: (b * n_q + i, 0)),
                  pl.BlockSpec((1,) + ck.shape[1:], lambda b, i: (b, 0, 0)),
                  pl.BlockSpec((1,) + cv.shape[1:], lambda b, i: (b, 0, 0)),
                  pl.BlockSpec((t_len, ROW_W), lambda b, i: (b, 0)),
                  pl.BlockSpec((t_len, ROW_W), lambda b, i: (b, 0)),
                  pl.BlockSpec(ov.shape, lambda b, i: (0, 0)),
                  pl.BlockSpec(ex.shape, lambda b, i: (0, 0)),
                  pl.BlockSpec((1, LANE), lambda b, i: (0, 0))],
        out_specs=pl.BlockSpec((tq, A_QW), lambda b, i: (b * n_q + i, 0)),
        scratch_shapes=[pltpu.VMEM((A_KV * tq, LANE), F32),
                        pltpu.VMEM((A_KV, rows4, 1), F32),
                        pltpu.VMEM((A_KV, rows4, 1), F32),
                        pltpu.VMEM((A_KV, rows4, A_DH), F32),
                        pltpu.VMEM((A_KV, rows4, A_DH), F32)],
        compiler_params=_params(2), name="nsa_prompt_attention",
    )(q, gate_pre, ck, cv, sel_rows, win_rows, ov, ex, b_gate_row)


DEC_ROWS = 8
CMP_PAGES_PER_STEP = 32
SEL_PAGES_PER_STEP = 32
R16_PER_PAGE = PAGE_SIZE // CMP_STRIDE


def _dec_cmp_body(pt_ref, *refs, past_len, n_blk_pad):
    del pt_ref
    pages = refs[:CMP_PAGES_PER_STEP]
    (w1_ref, b1_ref, w2_ref, q_ref, ov_ref, ocmp_ref, msel_ref,
     ck_s, cv_s, carry_s, x_s) = refs[CMP_PAGES_PER_STEP:]
    s = pl.program_id(1)
    rows = CMP_PAGES_PER_STEP * R16_PER_PAGE

    @pl.when(s == 0)
    def _():
        carry_s[...] = jnp.zeros_like(carry_s)

    out_row = _iota((PAGE_SIZE, PAGE_SIZE), 0)
    src_tok = (CMP_STRIDE * (out_row & (R16_PER_PAGE - 1))
               + lax.shift_right_logical(out_row, R16_PER_PAGE.bit_length() - 1))
    perm = jnp.where(_iota((PAGE_SIZE, PAGE_SIZE), 1) == src_tok, 1.0, 0.0).astype(BF16)
    for k, p in enumerate(pages):
        x_t = _nt(perm, p[0].astype(BF16))
        for l in range(CMP_STRIDE):
            x_s[l, k * R16_PER_PAGE:(k + 1) * R16_PER_PAGE, :] = (
                x_t[l * R16_PER_PAGE:(l + 1) * R16_PER_PAGE, :])

    rid = _iota((rows, 1), 0)
    for c, dst in ((0, ck_s), (1, cv_s)):
        both = None
        for l in range(CMP_STRIDE):
            x = x_s[l, :, c * GRP_W:(c + 1) * GRP_W].astype(BF16)
            y = _nn(x, w1_ref[c, l])
            both = y if both is None else both + y
        first, second = both[:, :GRP_W], both[:, GRP_W:]
        prev = carry_s[c, SUBLANE - 1:SUBLANE, :]
        shifted = jnp.where(rid == 0, prev, pltpu.roll(first, 1, 0))
        hid = _gelu(shifted + second + b1_ref[c])
        dst[pl.ds(pl.multiple_of(s * rows, rows), rows), :] = _nn(hid.astype(BF16), w2_ref[c])
        carry_s[c] = first[rows - SUBLANE:rows, :]

    @pl.when(s == pl.num_programs(1) - 1)
    def _():
        n_rows = ck_s.shape[0]
        q = q_ref[0]
        t_col = _iota((DEC_ROWS, 1), 0)
        pos = past_len + t_col
        pos4 = jnp.concatenate([pos] * A_REP, axis=0)
        m_lane = _iota((1, n_rows), 1)
        valid = (m_lane >= 1) & ((m_lane - 1) * CMP_STRIDE + (CMP_LEN - 1) <= pos4)
        imps = []
        for g in range(A_KV):
            gs = slice(g * A_DH, (g + 1) * A_DH)
            qg = _stack_heads(q, g).astype(BF16)
            p = _masked_softmax(_nt(qg, ck_s[:, gs].astype(BF16)) * ATTN_SCALE, valid)
            o = _nn(p.astype(BF16), cv_s[:, gs].astype(BF16))
            for r in range(A_REP):
                h = g * A_REP + r
                ocmp_ref[0, :, h * A_DH:(h + 1) * A_DH] = o[r * DEC_ROWS:(r + 1) * DEC_ROWS]
            p_sum = p[0:DEC_ROWS]
            for r in range(1, A_REP):
                p_sum = p_sum + p[r * DEC_ROWS:(r + 1) * DEC_ROWS]
            imps.append(_block_importance(p_sum, ov_ref))
        imp = jnp.concatenate(imps, axis=0)
        lane = _iota((A_KV * DEC_ROWS, n_blk_pad), 1)
        score = _block_scores(imp, pos4, lane)
        mask = _topk_mask(score, lane.astype(F32), N_SEL)
        for j in range(n_blk_pad // LANE):
            msel_ref[0, j] = mask[:, j * LANE:(j + 1) * LANE]


def nsa_decode_compress(cache_t, page_table, w1bd, b1t, w2bd, q8, past_len, n_blk_pad):
    dbsz, n_pages = page_table.shape
    assert n_pages % CMP_PAGES_PER_STEP == 0
    n_steps = n_pages // CMP_PAGES_PER_STEP
    n_rows = n_pages * R16_PER_PAGE
    n_tiles = n_blk_pad // LANE
    ov = _overlap_matrix(n_rows, n_blk_pad, 1)

    def page_spec(k):
        return pl.BlockSpec((1, ROW_W, PAGE_SIZE),
                            lambda b, s, pt: (pt[b, s * CMP_PAGES_PER_STEP + k], 0, 0))

    grid_spec = pltpu.PrefetchScalarGridSpec(
        num_scalar_prefetch=1,
        grid=(dbsz, n_steps),
        in_specs=[page_spec(k) for k in range(CMP_PAGES_PER_STEP)] + [
            pl.BlockSpec(w1bd.shape, lambda b, s, pt: (0, 0, 0, 0)),
            pl.BlockSpec(b1t.shape, lambda b, s, pt: (0, 0, 0)),
            pl.BlockSpec(w2bd.shape, lambda b, s, pt: (0, 0, 0)),
            pl.BlockSpec((1, DEC_ROWS, A_QW), lambda b, s, pt: (b, 0, 0)),
            pl.BlockSpec(ov.shape, lambda b, s, pt: (0, 0))],
        out_specs=(pl.BlockSpec((1, DEC_ROWS, A_QW), lambda b, s, pt: (b, 0, 0)),
                   pl.BlockSpec((1, n_tiles, A_KV * DEC_ROWS, LANE), lambda b, s, pt: (b, 0, 0, 0))),
        scratch_shapes=[pltpu.VMEM((n_rows, GRP_W), F32), pltpu.VMEM((n_rows, GRP_W), F32),
                        pltpu.VMEM((2, SUBLANE, GRP_W), F32),
                        pltpu.VMEM((CMP_STRIDE, CMP_PAGES_PER_STEP * R16_PER_PAGE, ROW_W), F32)])
    body = functools.partial(_dec_cmp_body, past_len=past_len, n_blk_pad=n_blk_pad)
    return pl.pallas_call(
        body,
        out_shape=(jax.ShapeDtypeStruct((dbsz, DEC_ROWS, A_QW), F32),
                   jax.ShapeDtypeStruct((dbsz, n_tiles, A_KV * DEC_ROWS, LANE), F32)),
        grid_spec=grid_spec,
        compiler_params=_params(2), name="nsa_decode_compress",
    )(page_table, *([cache_t] * CMP_PAGES_PER_STEP), w1bd, b1t, w2bd, q8, ov)


def _dec_sel_body(pt_ref, *refs, past_len, t_new):
    del pt_ref
    pages = refs[:SEL_PAGES_PER_STEP]
    (q_ref, msel_ref, ocmp_ref, snew_ref, wpast_ref, wnew_ref, gate_ref, bg_ref,
     o_ref, m_s, l_s, acc_s) = refs[SEL_PAGES_PER_STEP:]
    s = pl.program_id(1)
    keys = SEL_PAGES_PER_STEP * PAGE_SIZE
    blks = keys // SEL_BLK
    steps_per_tile = LANE // blks

    @pl.when(s == 0)
    def _():
        m_s[...] = jnp.full_like(m_s, NEG_INF)
        l_s[...] = jnp.zeros_like(l_s)
        acc_s[...] = jnp.zeros_like(acc_s)

    q = q_ref[0]
    tile = s // steps_per_tile
    first_blk = (s - tile * steps_per_tile) * blks
    expand = (_iota((LANE, keys), 0)
              == first_blk + lax.shift_right_arithmetic(_iota((LANE, keys), 1), SEL_BLK_LOG2))
    expand = jnp.where(expand, 1.0, 0.0).astype(BF16)
    m_keys = _nn(msel_ref[0, tile].astype(BF16), expand)

    for g in range(A_KV):
        gs = slice(g * A_DH, (g + 1) * A_DH)
        vs_ = slice(GRP_W + g * A_DH, GRP_W + (g + 1) * A_DH)
        qg = _stack_heads(q, g).astype(BF16)
        k_t = jnp.concatenate([p[0, gs, :] for p in pages], axis=1).astype(BF16)
        v_t = jnp.concatenate([p[0, vs_, :] for p in pages], axis=1).astype(BF16)
        valid = jnp.concatenate([m_keys[g * DEC_ROWS:(g + 1) * DEC_ROWS]] * A_REP, axis=0) > 0.5
        _online_softmax_step(m_s, l_s, acc_s, g, _nn(qg, k_t) * ATTN_SCALE, valid,
                             lambda e, v_t=v_t: _nt(e, v_t))

    @pl.when(s == pl.num_programs(1) - 1)
    def _():
        t_col = _iota((DEC_ROWS, 1), 0)
        t4 = jnp.concatenate([t_col] * A_REP, axis=0)
        t_key = _iota((1, DEC_ROWS), 1)
        valid_new = (t_key <= t4) & (t_key < t_new)
        gate = jax.nn.sigmoid(gate_ref[0] + bg_ref[...])
        n_win = wpast_ref.shape[2]
        kpos_w = past_len - n_win + _iota((1, n_win), 1)
        diff_w = (past_len + t4) - kpos_w
        valid_wp = (diff_w >= 0) & (diff_w <= WINDOW)
        new_blk = past_len // SEL_BLK
        new_tile, new_lane = new_blk // LANE, new_blk % LANE
        for g in range(A_KV):
            gs = slice(g * A_DH, (g + 1) * A_DH)
            vs_ = slice(GRP_W + g * A_DH, GRP_W + (g + 1) * A_DH)
            qg = _stack_heads(q, g).astype(BF16)
            mnew = msel_ref[0, new_tile, g * DEC_ROWS:(g + 1) * DEC_ROWS, new_lane:new_lane + 1]
            mnew4 = jnp.concatenate([mnew] * A_REP, axis=0)
            v_new = snew_ref[0, :, vs_].astype(BF16)
            _online_softmax_step(m_s, l_s, acc_s, g,
                                 _nt(qg, snew_ref[0, :, gs].astype(BF16)) * ATTN_SCALE,
                                 valid_new & (mnew4 > 0.5), lambda e, v_new=v_new: _nn(e, v_new))
            o_sel = acc_s[g] / jnp.maximum(l_s[g], 1e-30)

            s_p = jnp.where(valid_wp, _nn(qg, wpast_ref[0, gs, :].astype(BF16)) * ATTN_SCALE, NEG_INF)
            s_n = jnp.where(valid_new, _nt(qg, wnew_ref[0, :, gs].astype(BF16)) * ATTN_SCALE, NEG_INF)
            mx = jnp.maximum(jnp.max(s_p, axis=1, keepdims=True), jnp.max(s_n, axis=1, keepdims=True))
            mx = jnp.where(mx == NEG_INF, 0.0, mx)
            e_p = jnp.exp(s_p - mx)
            e_n = jnp.exp(s_n - mx)
            den = jnp.maximum(jnp.sum(e_p, axis=1, keepdims=True)
                              + jnp.sum(e_n, axis=1, keepdims=True), 1e-30)
            o_win = (_nt(e_p.astype(BF16), wpast_ref[0, vs_, :].astype(BF16))
                     + _nn(e_n.astype(BF16), wnew_ref[0, :, vs_].astype(BF16))) / den
            o_cmp = _stack_heads(ocmp_ref[0], g)
            _combine_heads(o_ref.at[0], gate, o_cmp, o_sel, o_win, g, DEC_ROWS)


def nsa_decode_attention(cache_t, page_table, q8, msel, ocmp, sel_new, win_past_t, win_new,
                         gate8, b_gate_row, past_len, t_new):
    dbsz, n_pages = page_table.shape
    keys = SEL_PAGES_PER_STEP * PAGE_SIZE
    assert n_pages % SEL_PAGES_PER_STEP == 0 and past_len % SEL_BLK == 0
    assert LANE % (keys // SEL_BLK) == 0
    n_steps = n_pages // SEL_PAGES_PER_STEP
    n_win = win_past_t.shape[2]

    def page_spec(k):
        return pl.BlockSpec((1, ROW_W, PAGE_SIZE),
                            lambda b, s, pt: (pt[b, s * SEL_PAGES_PER_STEP + k], 0, 0))

    per_b = lambda shp: pl.BlockSpec((1,) + shp, lambda b, s, pt: (b,) + (0,) * len(shp))
    grid_spec = pltpu.PrefetchScalarGridSpec(
        num_scalar_prefetch=1,
        grid=(dbsz, n_steps),
        in_specs=[page_spec(k) for k in range(SEL_PAGES_PER_STEP)] + [
            per_b((DEC_ROWS, A_QW)), per_b(msel.shape[1:]), per_b((DEC_ROWS, A_QW)),
            per_b((DEC_ROWS, ROW_W)), per_b((ROW_W, n_win)), per_b((DEC_ROWS, ROW_W)),
            per_b((DEC_ROWS, LANE)), pl.BlockSpec((1, LANE), lambda b, s, pt: (0, 0))],
        out_specs=per_b((DEC_ROWS, A_QW)),
        scratch_shapes=[pltpu.VMEM((A_KV, A_REP * DEC_ROWS, 1), F32),
                        pltpu.VMEM((A_KV, A_REP * DEC_ROWS, 1), F32),
                        pltpu.VMEM((A_KV, A_REP * DEC_ROWS, A_DH), F32)])
    body = functools.partial(_dec_sel_body, past_len=past_len, t_new=t_new)
    return pl.pallas_call(
        body,
        out_shape=jax.ShapeDtypeStruct((dbsz, DEC_ROWS, A_QW), F32),
        grid_spec=grid_spec,
        compiler_params=_params(2), name="nsa_decode_attention",
    )(page_table, *([cache_t] * SEL_PAGES_PER_STEP), q8, msel, ocmp, sel_new, win_past_t,
      win_new, gate8, b_gate_row)


def tokens_minor(rows):
    n, t = rows.shape[:2]
    return jnp.transpose(rows, (0, 2, 3, 4, 1)).reshape(n, ROW_W, t)


PROMPT_TM = 256
PEER_TM = 512
PEER_TE = 2048
PEER_SUB_EXPERTS = 1024
NSA_TQ = 128


def kernel(x_prompt, x_sample, cache_cmp_kv, cache_sel_kv, state_win_kv, state_C, state_n, state_m,
           page_table, norm_mix, norm_ffn, norm_final, mlstm_w_in, mlstm_b_i, mlstm_b_f,
           mlstm_head_norm, mlstm_w_out, nsa_w_in, nsa_b_gate, nsa_cmp_w1, nsa_cmp_b1, nsa_cmp_w2,
           nsa_w_out, peer_w_q, peer_sub_keys, peer_u, peer_v):
    bsz, t_len, d = x_prompt.shape
    dbsz, t_s, _ = x_sample.shape
    n_pages = page_table.shape[1]
    past_len = n_pages * PAGE_SIZE
    assert norm_mix.shape[0] == 2 and d == D_MODEL and t_s <= DEC_ROWS // 2
    assert t_len % M_CHUNK == 0 and t_len % NSA_TQ == 0 and t_len >= WINDOW

    xp = x_prompt.reshape(bsz * t_len, d)
    reps = DEC_ROWS // t_s
    xs = jnp.concatenate([x_sample] * reps, axis=1).reshape(dbsz * DEC_ROWS, d)
    n_s = dbsz * DEC_ROWS

    def peer(x, layer, tm, out_norm=None):
        keys = peer_sub_keys[layer].astype(BF16)
        return peer_ffn_residual(x, norm_ffn[layer], peer_w_q[layer].astype(BF16),
                                 keys[:, 0], keys[:, 1], peer_u[layer].astype(BF16),
                                 peer_v[layer].T.astype(BF16), tm, PEER_TE, out_norm)

    w_in = jnp.pad(mlstm_w_in[0], ((0, 0), (0, M_PROJ_PAD - M_PROJ))).astype(BF16)
    w_out = mlstm_w_out[0].astype(BF16)
    gate_bias = jnp.pad(jnp.concatenate([mlstm_b_i[0], mlstm_b_f[0]]),
                        (0, LANE - 2 * M_HEADS)).reshape(1, LANE)
    proj_p = rms_matmul(xp, norm_mix[0], w_in, PROMPT_TM).reshape(bsz, t_len, M_PROJ_PAD)
    proj_s = rms_matmul(xs, norm_mix[0], w_in, n_s).reshape(dbsz, DEC_ROWS, M_PROJ_PAD)
    hid_p, c_p, n_p, m_p = mlstm_scan(
        proj_p, jnp.zeros((bsz, M_HEADS, M_DK, M_DV), F32), jnp.zeros((bsz, M_HEADS, M_DK), F32),
        jnp.zeros((bsz, M_HEADS), F32), gate_bias, mlstm_head_norm[0], M_CHUNK, M_CHUNK - 1)
    hid_s, c_s, n_st, m_st = mlstm_scan(
        proj_s, state_C[0], state_n[0], state_m[0], gate_bias, mlstm_head_norm[0],
        DEC_ROWS, t_s - 1)
    xp = matmul_residual(hid_p.reshape(bsz * t_len, d), w_out, xp, PROMPT_TM)
    xs = matmul_residual(hid_s.reshape(n_s, d), w_out, xs, n_s)
    xp = peer(xp, 0, PEER_TM)
    xs = peer(xs, 0, n_s)

    w_in_a = jnp.pad(nsa_w_in[0], ((0, 0), (0, A_PROJ_PAD - A_PROJ))).astype(BF16)
    w_out_a = nsa_w_out[0].astype(BF16)
    b_gate_row = jnp.pad(nsa_b_gate[0].reshape(-1), (0, LANE - 3 * A_HEADS)).reshape(1, LANE)
    w1bd, b1t, w2bd = _compress_weights(nsa_cmp_w1[0], nsa_cmp_b1[0], nsa_cmp_w2[0])
    tabs_p = rope_tables(jnp.arange(t_len, dtype=jnp.int32))
    pos_s = past_len + jnp.arange(DEC_ROWS, dtype=jnp.int32)
    tabs_s = tuple(jnp.tile(a, (dbsz, 1)) for a in rope_tables(pos_s))

    q_p, cmp_p, sel_p, win_p, gate_p = nsa_project(xp, norm_mix[1], w_in_a, tabs_p, PROMPT_TM)
    q_s, cmp_s, sel_s, win_s, gate_s = nsa_project(xs, norm_mix[1], w_in_a, tabs_s, n_s)

    ck, cv = compress_prompt(cmp_p.reshape(bsz, t_len // CMP_STRIDE, CMP_STRIDE * ROW_W),
                             w1bd, b1t, w2bd)
    o_p = nsa_prompt_attention(q_p, gate_p, ck, cv, sel_p, win_p, b_gate_row, bsz, t_len, NSA_TQ)
    xp = matmul_residual(o_p, w_out_a, xp, PROMPT_TM)

    n_blk = -(-(past_len + t_s) // SEL_BLK)
    n_blk_pad = -(-n_blk // LANE) * LANE
    q8 = q_s.reshape(dbsz, DEC_ROWS, A_QW)
    o_cmp, msel = nsa_decode_compress(tokens_minor(cache_cmp_kv[0]), page_table, w1bd, b1t, w2bd,
                                      q8, past_len, n_blk_pad)
    o_s = nsa_decode_attention(
        tokens_minor(cache_sel_kv[0]), page_table, q8, msel, o_cmp,
        sel_s.reshape(dbsz, DEC_ROWS, ROW_W), tokens_minor(state_win_kv[0]),
        win_s.reshape(dbsz, DEC_ROWS, ROW_W), gate_s.reshape(dbsz, DEC_ROWS, LANE), b_gate_row,
        past_len, t_s)
    xs = matmul_residual(o_s.reshape(n_s, d), w_out_a, xs, n_s)
    y_p = peer(xp, 1, PEER_TM, norm_final).reshape(bsz, t_len, d)
    y_s = peer(xs, 1, n_s, norm_final).reshape(dbsz, DEC_ROWS, d)[:, :t_s]

    row_shape = (2, A_KV, A_DH)
    kv_p = lambda a: a.reshape((1, bsz, t_len) + row_shape)
    kv_s = lambda a: a.reshape((dbsz, DEC_ROWS) + row_shape)[None, :, :t_s]
    win_rows = min(WINDOW, t_len)
    win_buf_p = win_p.reshape((bsz, t_len) + row_shape)[None, :, t_len - win_rows:]
    win_new_s = win_s.reshape((dbsz, DEC_ROWS) + row_shape)[:, :t_s]
    win_buf_s = jnp.concatenate([state_win_kv[0], win_new_s], axis=1)[None, :, t_s:]
    return (y_p, y_s,
            kv_p(cmp_p), kv_p(sel_p), win_buf_p,
            c_p[None], n_p[None], m_p[None],
            kv_s(cmp_s), kv_s(sel_s), win_buf_s,
            c_s[None], n_st[None], m_st[None])
```

```python
import functools

import numpy as np
import jax
import jax.numpy as jnp
from jax import lax
from jax.experimental import pallas as pl
from jax.experimental.pallas import tpu as pltpu

F32 = jnp.float32
BF16 = jnp.bfloat16
HI = lax.Precision.HIGHEST

V7X_VMEM_BYTES = 64 * 1024 * 1024
VMEM_LIMIT = V7X_VMEM_BYTES - 8 * 1024 * 1024
LANE = 128
SUBLANE = 8

D_MODEL = 1024
NORM_EPS = 1e-6

M_HEADS = 8
M_DK = 64
M_DV = 128
M_CHUNK = 64
M_PROJ = 2 * M_HEADS * M_DK + 2 * M_HEADS * M_DV + 2 * M_HEADS
M_PROJ_PAD = 3200

A_HEADS = 16
A_KV = 4
A_REP = 4
A_DH = 64
ROT_DIM = 16
ROPE_THETA = 500000.0
CMP_STRIDE = 16
CMP_LEN = 32
SEL_BLK = 64
SEL_BLK_LOG2 = 6
N_SEL = 16
WINDOW = 512
A_QW = 1024
A_KVW = 1536
A_PROJ = A_QW + A_KVW + 3 * A_HEADS
A_PROJ_PAD = 2688
ATTN_SCALE = A_DH ** -0.5
GRP_W = A_KV * A_DH
ROW_W = 2 * GRP_W
PAGE_SIZE = 128

P_HEADS = 8
P_NKEYS = 128
P_EXPERTS = P_NKEYS * P_NKEYS
P_QDIM = 256
P_TOPK = 16

NEG_INF = float("-inf")
LOG2E = 1.4426950408889634
MASKED = -(2.0 ** 100)


def _params(n_axes):
    return pltpu.CompilerParams(dimension_semantics=("arbitrary",) * n_axes,
                                vmem_limit_bytes=VMEM_LIMIT)


def _nn(a, b, precision=None):
    return jnp.dot(a, b, preferred_element_type=F32, precision=precision)


def _nt(a, b, precision=None):
    return lax.dot_general(a, b, (((1,), (1,)), ((), ())), preferred_element_type=F32,
                           precision=precision)


def _tn(a, b, precision=None):
    return lax.dot_general(a, b, (((0,), (0,)), ((), ())), preferred_element_type=F32,
                           precision=precision)


def _gelu(x):
    return 0.5 * x * (1.0 + lax.erf(x * (2.0 ** -0.5)))


def _rms(x, g):
    return x * lax.rsqrt(jnp.mean(x * x, axis=-1, keepdims=True) + NORM_EPS) * g


def _masked_softmax(s, valid):
    s = jnp.where(valid, s, NEG_INF)
    m = jnp.max(s, axis=-1, keepdims=True)
    m = jnp.where(m == NEG_INF, 0.0, m)
    e = jnp.exp(s - m)
    return e / jnp.maximum(jnp.sum(e, axis=-1, keepdims=True), 1e-30)


def _iota(shape, dim):
    return lax.broadcasted_iota(jnp.int32, shape, dim)


def _rms_mm_body(x_ref, g_ref, w_ref, o_ref):
    o_ref[...] = _nn(_rms(x_ref[...], g_ref[...]).astype(BF16), w_ref[...])


def rms_matmul(x, g, w, tm):
    n, d = x.shape
    nout = w.shape[1]
    return pl.pallas_call(
        _rms_mm_body,
        out_shape=jax.ShapeDtypeStruct((n, nout), F32),
        grid=(n // tm,),
        in_specs=[pl.BlockSpec((tm, d), lambda i: (i, 0)),
                  pl.BlockSpec((1, d), lambda i: (0, 0)),
                  pl.BlockSpec((d, nout), lambda i: (0, 0))],
        out_specs=pl.BlockSpec((tm, nout), lambda i: (i, 0)),
        compiler_params=_params(1), name="rms_matmul",
    )(x, g.reshape(1, d), w)


def _mm_res_body(a_ref, w_ref, x_ref, o_ref):
    o_ref[...] = x_ref[...] + _nn(a_ref[...].astype(BF16), w_ref[...])


def matmul_residual(a, w, x, tm):
    n, k = a.shape
    d = w.shape[1]
    return pl.pallas_call(
        _mm_res_body,
        out_shape=jax.ShapeDtypeStruct((n, d), F32),
        grid=(n // tm,),
        in_specs=[pl.BlockSpec((tm, k), lambda i: (i, 0)),
                  pl.BlockSpec((k, d), lambda i: (0, 0)),
                  pl.BlockSpec((tm, d), lambda i: (i, 0))],
        out_specs=pl.BlockSpec((tm, d), lambda i: (i, 0)),
        compiler_params=_params(1), name="matmul_residual",
    )(a, w, x)


def _nsa_proj_body(x_ref, g_ref, w_ref, c_ref, sa_ref, sb_ref,
                   q_ref, cmp_ref, sel_ref, win_ref, gate_ref):
    y = _nn(_rms(x_ref[...], g_ref[...]).astype(BF16), w_ref[...])
    c = c_ref[...]
    sa = sa_ref[...]
    sb = sb_ref[...]

    def rot(z):
        return (z * c + pltpu.roll(z, GRP_W - ROT_DIM // 2, 1) * sa
                + pltpu.roll(z, ROT_DIM // 2, 1) * sb)

    for j in range(A_QW // GRP_W):
        q_ref[:, j * GRP_W:(j + 1) * GRP_W] = rot(y[:, j * GRP_W:(j + 1) * GRP_W])
    for ref, base in ((cmp_ref, A_QW), (sel_ref, A_QW + ROW_W), (win_ref, A_QW + 2 * ROW_W)):
        ref[:, 0:GRP_W] = rot(y[:, base:base + GRP_W])
        ref[:, GRP_W:ROW_W] = y[:, base + GRP_W:base + ROW_W]
    gate_ref[...] = y[:, A_QW + A_KVW:A_PROJ_PAD]


def nsa_project(x, g, w, tabs, tm):
    n, d = x.shape
    n_tab = tabs[0].shape[0] // tm
    tab_spec = pl.BlockSpec((tm, GRP_W), lambda i: (i % n_tab, 0))
    row = lambda w_: pl.BlockSpec((tm, w_), lambda i: (i, 0))
    return pl.pallas_call(
        _nsa_proj_body,
        out_shape=(jax.ShapeDtypeStruct((n, A_QW), F32),
                   jax.ShapeDtypeStruct((n, ROW_W), F32),
                   jax.ShapeDtypeStruct((n, ROW_W), F32),
                   jax.ShapeDtypeStruct((n, ROW_W), F32),
                   jax.ShapeDtypeStruct((n, LANE), F32)),
        grid=(n // tm,),
        in_specs=[row(d), pl.BlockSpec((1, d), lambda i: (0, 0)),
                  pl.BlockSpec((d, A_PROJ_PAD), lambda i: (0, 0)),
                  tab_spec, tab_spec, tab_spec],
        out_specs=(row(A_QW), row(ROW_W), row(ROW_W), row(ROW_W), row(LANE)),
        compiler_params=_params(1), name="nsa_project",
    )(x, g.reshape(1, d), w, *tabs)


def rope_tables(pos):
    half = ROT_DIM // 2
    inv_freq = ROPE_THETA ** (-(jnp.arange(half, dtype=F32) * (2.0 / ROT_DIM)))
    ang = pos.astype(F32)[:, None] * inv_freq[None, :]
    cos, sin = jnp.cos(ang), jnp.sin(ang)
    t = pos.shape[0]
    one = jnp.ones((t, A_DH - ROT_DIM), F32)
    zero = jnp.zeros((t, A_DH - ROT_DIM), F32)
    z8 = jnp.zeros((t, half), F32)
    c = jnp.concatenate([cos, cos, one], axis=1)
    sa = jnp.concatenate([-sin, z8, zero], axis=1)
    sb = jnp.concatenate([z8, sin, zero], axis=1)
    return tuple(jnp.tile(a, (1, A_KV)) for a in (c, sa, sb))


def _mlstm_body(q_ref, k_ref, v_ref, og_ref, gt_ref, c0_ref, n0_ref, m0_ref, gb_ref, hn_ref,
                hid_ref, c_out, n_out, m_out, c_s, n_s, m_s, *, chunk, t_last, nb):
    ci = pl.program_id(1)

    @pl.when(ci == 0)
    def _():
        c_s[...] = c0_ref[...]
        n_s[...] = n0_ref[...]
        m_s[...] = m0_ref[...]

    L = chunk
    row = _iota((L, L), 0)
    col = _iota((L, L), 1)
    causal = row >= col
    tril = causal.astype(F32)
    eye8 = (_iota((SUBLANE, LANE), 0) == _iota((SUBLANE, LANE), 1)).astype(F32)
    rlane = _iota((L, LANE), 0)
    lane1 = _iota((1, LANE), 1)
    sel_row = _iota((LANE, LANE), 0)
    ones_l = jnp.ones((L, LANE), BF16)
    ones_k = jnp.ones((M_DK, LANE), BF16)
    ones_v = jnp.ones((M_DV, LANE), BF16)

    def replicate(parts, col_idx):
        sel = jnp.where(sel_row == col_idx, 1.0, 0.0).astype(BF16)
        return _nn(parts[0], sel) + _nn(parts[1], sel) + _nn(parts[2], sel)

    units = [(bi, h) for bi in range(nb) for h in range(M_HEADS)]
    b_parts, g_parts, m_parts, g_rows = [], [], [], []
    for bi in range(nb):
        gt = gt_ref[bi] + gb_ref[...]
        log_f = jnp.minimum(gt, 0.0) - jnp.log1p(jnp.exp(-jnp.abs(gt)))
        ba = _nn(tril, log_f, HI)
        g_mat = gt - pltpu.roll(ba, LANE - M_HEADS, 1)
        b_parts.append(_split3(ba))
        g_parts.append(_split3(g_mat))
        m_parts.append(_split3(jnp.broadcast_to(m_s[bi], (SUBLANE, LANE))))
        g_rows.append(_nt(eye8, g_mat, HI))

    qs, ks, vs, cs, ns = {}, {}, {}, {}, {}
    s_raw, q_c, b_rep, g_rep, m_prev = {}, {}, {}, {}, {}
    for u in units:
        bi, h = u
        qs[u] = q_ref[bi, :, h * M_DK:(h + 1) * M_DK]
        ks[u] = k_ref[bi, :, h * M_DK:(h + 1) * M_DK] * (M_DK ** -0.5)
        vs[u] = v_ref[bi, :, h * M_DV:(h + 1) * M_DV]
        cs[u] = c_s[bi, h]
        ns[u] = n_s[bi, h:h + 1, :]
        s_raw[u] = _nt(qs[u].astype(BF16), ks[u].astype(BF16))
        q_c[u] = _nn(qs[u].astype(BF16), cs[u].astype(BF16))
        b_rep[u] = replicate(b_parts[bi], M_HEADS + h)
        g_rep[u] = replicate(g_parts[bi], h)
        m_prev[u] = replicate(m_parts[bi], h)[0:1, :]

    a_rep, m_t, w_d, w_a = {}, {}, {}, {}
    for u in units:
        bi, h = u
        pm = g_rep[u]
        shift = 1
        while shift < L:
            pm = jnp.where(rlane >= shift, jnp.maximum(pm, pltpu.roll(pm, shift, 0)), pm)
            shift *= 2
        a_rep[u] = b_rep[u] + m_prev[u]
        m_t[u] = jnp.maximum(a_rep[u], b_rep[u] + pm)
        dm = jnp.where(causal, b_rep[u][:, :L] + g_rows[bi][h:h + 1, :], NEG_INF)
        w_d[u] = jnp.exp(dm - m_t[u][:, :L])
        w_a[u] = jnp.exp(a_rep[u] - m_t[u])

    for u in units:
        bi, h = u
        s = s_raw[u] * w_d[u]
        s_b = s.astype(BF16)
        num = w_a[u] * q_c[u] + _nn(s_b, vs[u].astype(BF16))
        den = w_a[u] * _nn((qs[u] * ns[u]).astype(BF16), ones_k) + _nn(s_b, ones_l)
        h_out = num / jnp.maximum(jnp.abs(den), jnp.exp(-m_t[u]))
        msq = _nn((h_out * h_out).astype(BF16), ones_v) * (1.0 / M_DV)
        hid = h_out * lax.rsqrt(msq + NORM_EPS) * hn_ref[:, h * M_DV:(h + 1) * M_DV]
        hid_ref[bi, :, h * M_DV:(h + 1) * M_DV] = (
            jax.nn.sigmoid(og_ref[bi, :, h * M_DV:(h + 1) * M_DV]) * hid)

    m_rows = [m_s[bi] for bi in range(nb)]
    for u in units:
        bi, h = u
        m_new = m_t[u][t_last:t_last + 1, :]
        d_last = jnp.where(rlane <= t_last, b_rep[u][t_last:t_last + 1, :] + g_rep[u], NEG_INF)
        w_last = jnp.exp(d_last - m_new)
        decay = jnp.exp(a_rep[u][t_last:t_last + 1, :] - m_new)
        kw = ks[u] * w_last[:, :M_DK]
        c_s[bi, h] = decay * cs[u] + _tn(kw, vs[u], HI)
        n_s[bi, h:h + 1, :] = decay[:, :M_DK] * ns[u] + jnp.sum(kw, axis=0, keepdims=True)
        m_rows[bi] = jnp.where(lane1 == h, m_new, m_rows[bi])
    for bi in range(nb):
        m_s[bi] = m_rows[bi]

    @pl.when(ci == pl.num_programs(1) - 1)
    def _():
        c_out[...] = c_s[...]
        n_out[...] = n_s[...]
        m_out[...] = m_s[...]


MLSTM_SEQ_PER_STEP = 4


def mlstm_scan(proj, c0, n0, m0, gate_bias, head_norm, chunk, t_last):
    b, t, _ = proj.shape
    nb = MLSTM_SEQ_PER_STEP
    assert b % nb == 0
    n_chunks = t // chunk
    hd = M_HEADS * M_DK
    vd = M_HEADS * M_DV
    m0p = jnp.pad(m0, ((0, 0), (0, LANE - M_HEADS))).reshape(b, 1, LANE)
    body = functools.partial(_mlstm_body, chunk=chunk, t_last=t_last, nb=nb)
    hid, c_t, n_t, m_t = pl.pallas_call(
        body,
        out_shape=(jax.ShapeDtypeStruct((b, t, vd), F32),
                   jax.ShapeDtypeStruct((b, M_HEADS, M_DK, M_DV), F32),
                   jax.ShapeDtypeStruct((b, M_HEADS, M_DK), F32),
                   jax.ShapeDtypeStruct((b, 1, LANE), F32)),
        grid=(b // nb, n_chunks),
        in_specs=[pl.BlockSpec((nb, chunk, hd), lambda i, c: (i, c, 0)),
                  pl.BlockSpec((nb, chunk, hd), lambda i, c: (i, c, 1)),
                  pl.BlockSpec((nb, chunk, vd), lambda i, c: (i, c, 1)),
                  pl.BlockSpec((nb, chunk, vd), lambda i, c: (i, c, 2)),
                  pl.BlockSpec((nb, chunk, LANE), lambda i, c: (i, c, (2 * hd + 2 * vd) // LANE)),
                  pl.BlockSpec((nb, M_HEADS, M_DK, M_DV), lambda i, c: (i, 0, 0, 0)),
                  pl.BlockSpec((nb, M_HEADS, M_DK), lambda i, c: (i, 0, 0)),
                  pl.BlockSpec((nb, 1, LANE), lambda i, c: (i, 0, 0)),
                  pl.BlockSpec((1, LANE), lambda i, c: (0, 0)),
                  pl.BlockSpec((1, vd), lambda i, c: (0, 0))],
        out_specs=(pl.BlockSpec((nb, chunk, vd), lambda i, c: (i, c, 0)),
                   pl.BlockSpec((nb, M_HEADS, M_DK, M_DV), lambda i, c: (i, 0, 0, 0)),
                   pl.BlockSpec((nb, M_HEADS, M_DK), lambda i, c: (i, 0, 0)),
                   pl.BlockSpec((nb, 1, LANE), lambda i, c: (i, 0, 0))),
        scratch_shapes=[pltpu.VMEM((nb, M_HEADS, M_DK, M_DV), F32),
                        pltpu.VMEM((nb, M_HEADS, M_DK), F32),
                        pltpu.VMEM((nb, 1, LANE), F32)],
        compiler_params=_params(2), name="mlstm_scan",
    )(proj, proj, proj, proj, proj, c0, n0, m0p, gate_bias, head_norm.reshape(1, vd))
    return hid, c_t, n_t, m_t[:, 0, :M_HEADS]


N_CAND = (2 + SUBLANE) * SUBLANE
RANK_OUT = float(P_TOPK)


def _top_rows(xs, dsts, k, with_rank):
    xs = list(xs)
    ranks = [jnp.full(x.shape, RANK_OUT, F32) if w else None for x, w in zip(xs, with_rank)]
    for r in range(k):
        for i, dst in enumerate(dsts):
            m = jnp.max(xs[i], axis=1, keepdims=True)
            dst[:, r:r + 1, :] = m
            hit = xs[i] == m
            if ranks[i] is not None:
                ranks[i] = jnp.where(hit, float(r), ranks[i])
            xs[i] = jnp.where(hit, NEG_INF, xs[i])
    return ranks


def _peer_select_body(x_ref, g_ref, wq_ref, k1_ref, k2_ref, r2_ref, e2_ref, l_ref, c_ref,
                      s1_s, s2_s, v1_s, v2_s, cand_s, vals_s):
    qp = _nn(_rms(x_ref[...], g_ref[...]).astype(BF16), wq_ref[...])
    for h in range(P_HEADS):
        q1 = qp[:, h * P_QDIM:h * P_QDIM + P_QDIM // 2].astype(BF16)
        q2 = qp[:, h * P_QDIM + P_QDIM // 2:(h + 1) * P_QDIM].astype(BF16)
        s1_s[h] = _nt(k1_ref[h], q1)
        s2_s[h] = _nt(k2_ref[h], q2)
    s1 = s1_s[...]
    s2 = s2_s[...]
    _, rank2 = _top_rows((s1, s2), (v1_s, v2_s), P_TOPK, (False, True))
    v2_lo = v2_s[:, 0:SUBLANE, :]
    cand_s[:, 0:SUBLANE, :] = v2_lo + v1_s[:, 0:1, :]
    cand_s[:, SUBLANE:2 * SUBLANE, :] = v2_s[:, SUBLANE:2 * SUBLANE, :] + v1_s[:, 0:1, :]
    cand_s[:, 2 * SUBLANE:3 * SUBLANE, :] = v1_s[:, SUBLANE:2 * SUBLANE, :] + v2_s[:, 0:1, :]
    for r in range(1, SUBLANE):
        cand_s[:, (2 + r) * SUBLANE:(3 + r) * SUBLANE, :] = v2_lo + v1_s[:, r:r + 1, :]
    _top_rows((cand_s[...],), (vals_s,), P_TOPK, (False,))
    top = vals_s[:, 0:1, :]
    tau = vals_s[:, P_TOPK - 1:P_TOPK, :]
    z = jnp.sum(jnp.exp(vals_s[...] - top), axis=1, keepdims=True)
    v1 = v1_s[...]
    kept = jnp.zeros(v1.shape, F32)
    for r in range(P_TOPK):
        kept = kept + jnp.where(v2_s[:, r:r + 1, :] + v1 >= tau, 1.0, 0.0)
    n_keep = jnp.zeros(s1.shape, F32)
    for r in range(P_TOPK):
        n_keep = jnp.where(s1 == v1_s[:, r:r + 1, :], kept[:, r:r + 1, :], n_keep)
    r2_ref[...] = rank2.astype(BF16)
    e2_ref[...] = jnp.exp(s2 - v2_s[:, 0:1, :]).astype(BF16)
    l_ref[...] = n_keep
    c_ref[...] = 0.5 * jnp.exp(s1 - v1_s[:, 0:1, :]) / z


def peer_select(x, g, w_q, k1, k2, tm):
    n, d = x.shape
    spec = pl.BlockSpec((P_HEADS, P_NKEYS, tm), lambda i: (0, 0, i))
    key_spec = pl.BlockSpec((P_HEADS, P_NKEYS, P_QDIM // 2), lambda i: (0, 0, 0))
    shp = lambda dt: jax.ShapeDtypeStruct((P_HEADS, P_NKEYS, n), dt)
    return pl.pallas_call(
        _peer_select_body,
        out_shape=(shp(BF16), shp(BF16), shp(F32), shp(F32)),
        grid=(n // tm,),
        in_specs=[pl.BlockSpec((tm, d), lambda i: (i, 0)),
                  pl.BlockSpec((1, d), lambda i: (0, 0)),
                  pl.BlockSpec((d, P_HEADS * P_QDIM), lambda i: (0, 0)),
                  key_spec, key_spec],
        out_specs=(spec, spec, spec, spec),
        scratch_shapes=[pltpu.VMEM((P_HEADS, P_NKEYS, tm), F32), pltpu.VMEM((P_HEADS, P_NKEYS, tm), F32),
                        pltpu.VMEM((P_HEADS, P_TOPK, tm), F32), pltpu.VMEM((P_HEADS, P_TOPK, tm), F32),
                        pltpu.VMEM((P_HEADS, N_CAND, tm), F32), pltpu.VMEM((P_HEADS, P_TOPK, tm), F32)],
        compiler_params=_params(1), name="peer_select",
    )(x, g.reshape(1, d), w_q, k1, k2)


def _peer_main_body(x_ref, g_ref, og_ref, u_ref, vt_ref, r2_ref, e2_ref, l_ref, c_ref,
                    o_ref, hn_s, acc_s, *, a_tile, norm_out):
    t = pl.program_id(1)

    @pl.when(t == 0)
    def _():
        hn_s[...] = _rms(x_ref[...], g_ref[...]).astype(BF16)
        acc_s[...] = jnp.zeros_like(acc_s)

    tm = hn_s.shape[0]
    zero = jnp.zeros((P_NKEYS, tm), BF16)
    a_sub = PEER_SUB_EXPERTS // P_NKEYS
    total = None
    for sub in range(a_tile // a_sub):
        es = slice(sub * PEER_SUB_EXPERTS, (sub + 1) * PEER_SUB_EXPERTS)
        sc = _nt(u_ref[es, :], hn_s[...])
        act = (sc + sc * lax.erf(sc * (2.0 ** -0.5))).astype(BF16)
        parts = []
        for k in range(a_sub):
            aa = sub * a_sub + k
            w = None
            for h in range(P_HEADS):
                keep = jnp.broadcast_to(l_ref[h, aa:aa + 1, :], (P_NKEYS, tm)).astype(BF16)
                coef = jnp.broadcast_to(c_ref[h, aa:aa + 1, :], (P_NKEYS, tm)).astype(BF16)
                term = jnp.where(r2_ref[h] < keep, e2_ref[h], zero) * coef
                w = term if w is None else w + term
            parts.append(w * act[k * P_NKEYS:(k + 1) * P_NKEYS])
        part = _nn(vt_ref[:, es], jnp.concatenate(parts, axis=0))
        total = part if total is None else total + part
    acc_s[...] += total

    @pl.when(t == pl.num_programs(1) - 1)
    def _():
        y = x_ref[...] + acc_s[...].T
        o_ref[...] = _rms(y, og_ref[...]) if norm_out else y


def peer_main(x, g, u, vt, layer, sel, tm, te, out_norm=None):
    n, d = x.shape
    og = g if out_norm is None else out_norm
    r2, e2, n_keep, coef = sel
    a_tile = te // P_NKEYS
    n_tiles = P_EXPERTS // te
    assert a_tile % SUBLANE == 0
    full_spec = pl.BlockSpec((P_HEADS, P_NKEYS, tm), lambda i, t: (0, 0, i))
    row_spec = pl.BlockSpec((P_HEADS, a_tile, tm), lambda i, t: (0, t, i))
    body = functools.partial(_peer_main_body, a_tile=a_tile, norm_out=out_norm is not None)
    return pl.pallas_call(
        body,
        out_shape=jax.ShapeDtypeStruct((n, d), F32),
        grid=(n // tm, n_tiles),
        in_specs=[pl.BlockSpec((tm, d), lambda i, t: (i, 0)),
                  pl.BlockSpec((1, d), lambda i, t: (0, 0)),
                  pl.BlockSpec((1, d), lambda i, t: (0, 0)),
                  pl.BlockSpec((None, te, d), lambda i, t: (layer, t, 0)),
                  pl.BlockSpec((None, d, te), lambda i, t: (layer, 0, t)),
                  full_spec, full_spec, row_spec, row_spec],
        out_specs=pl.BlockSpec((tm, d), lambda i, t: (i, 0)),
        scratch_shapes=[pltpu.VMEM((tm, d), BF16), pltpu.VMEM((d, tm), F32)],
        compiler_params=_params(2), name="peer_main",
    )(x, g.reshape(1, d), og.reshape(1, d), u, vt, r2, e2, n_keep, coef)


def peer_ffn_residual(x, g, w_q, k1, k2, u, vt, layer, tm, te, out_norm=None):
    sel = peer_select(x, g, w_q, k1, k2, tm)
    return peer_main(x, g, u, vt, layer, sel, tm, te, out_norm)


def _topk_mask(score, lane_f, k):
    sel = jnp.zeros(score.shape, F32)
    for _ in range(k):
        m = jnp.max(score, axis=1, keepdims=True)
        idx = jnp.min(jnp.where(score == m, lane_f, 1e9), axis=1, keepdims=True)
        hit = lane_f == idx
        sel = jnp.where(hit, 1.0, sel)
        score = jnp.where(hit, NEG_INF, score)
    return sel


def _block_scores(imp, pos_col, lane):
    cur = lax.shift_right_arithmetic(pos_col, SEL_BLK_LOG2)
    forced = (lane == 0) | (lane == cur) | (lane == cur - 1)
    return jnp.where(forced, jnp.inf, jnp.where(lane <= cur, imp, NEG_INF))


def _stack_heads(q, g):
    return jnp.concatenate(
        [q[:, (g * A_REP + r) * A_DH:(g * A_REP + r + 1) * A_DH] for r in range(A_REP)], axis=0)


def _combine_heads(o_ref, gate, o_cmp, o_sel, o_win, g, rows):
    for r in range(A_REP):
        h = g * A_REP + r
        rs = slice(r * rows, (r + 1) * rows)
        o_ref[:, h * A_DH:(h + 1) * A_DH] = (gate[:, 3 * h:3 * h + 1] * o_cmp[rs]
                                             + gate[:, 3 * h + 1:3 * h + 2] * o_sel[rs]
                                             + gate[:, 3 * h + 2:3 * h + 3] * o_win[rs])


def _online_softmax_step(m_ref, l_ref, acc_ref, g, sc, valid, pv):
    sc = jnp.where(valid, sc, NEG_INF)
    m_old = m_ref[g]
    m_new = jnp.maximum(m_old, jnp.max(sc, axis=1, keepdims=True))
    m_safe = jnp.where(m_new == NEG_INF, 0.0, m_new)
    alpha = jnp.exp(m_old - m_safe)
    e = jnp.exp(sc - m_safe)
    l_ref[g] = alpha * l_ref[g] + jnp.sum(e, axis=1, keepdims=True)
    acc_ref[g] = alpha * acc_ref[g] + pv(e.astype(BF16))
    m_ref[g] = m_new


def _compress_weights(w1, b1, w2):
    eye = jnp.eye(A_KV, dtype=F32)
    w1bd = jnp.einsum("cldh,gk->clgdkh", w1, eye).reshape(2, CMP_LEN, GRP_W, GRP_W).astype(BF16)
    w1cat = jnp.concatenate([w1bd[:, :CMP_STRIDE], w1bd[:, CMP_STRIDE:]], axis=-1)
    w2bd = jnp.einsum("chd,gk->cghkd", w2, eye).reshape(2, GRP_W, GRP_W).astype(BF16)
    b1t = jnp.tile(b1, (1, A_KV)).reshape(2, 1, GRP_W)
    return w1cat, b1t, w2bd


def _split3(x):
    hi = x.astype(BF16)
    r1 = x - hi.astype(F32)
    mid = r1.astype(BF16)
    lo = (r1 - mid.astype(F32)).astype(BF16)
    return hi, mid, lo


def _block_importance(p_sum, ov_ref):
    hi, mid, lo = _split3(p_sum)
    ov = ov_ref[...]
    return _nn(hi, ov) + _nn(mid, ov) + _nn(lo, ov)


def _overlap_matrix(n_rows, n_cols, shift):
    c_start = (np.arange(n_rows) - shift) * CMP_STRIDE
    s_start = np.arange(n_cols) * SEL_BLK
    ov = np.clip(np.minimum(c_start[:, None] + CMP_LEN, s_start[None, :] + SEL_BLK)
                 - np.maximum(c_start[:, None], s_start[None, :]), 0, None) / CMP_STRIDE
    ov[c_start < 0] = 0.0
    return jnp.asarray(ov, BF16)


R16_PER_PAGE = PAGE_SIZE // CMP_STRIDE


def _offset_grouping_perm():
    out_row = _iota((PAGE_SIZE, PAGE_SIZE), 0)
    src_tok = (CMP_STRIDE * (out_row & (R16_PER_PAGE - 1))
               + lax.shift_right_logical(out_row, R16_PER_PAGE.bit_length() - 1))
    return jnp.where(_iota((PAGE_SIZE, PAGE_SIZE), 1) == src_tok, 1.0, 0.0).astype(BF16)


def _store_by_offset(x_s, group, x_t):
    for l in range(CMP_STRIDE):
        x_s[l, group * R16_PER_PAGE:(group + 1) * R16_PER_PAGE, :] = (
            x_t[l * R16_PER_PAGE:(l + 1) * R16_PER_PAGE, :])


def _cmp_prompt_body(r_ref, w1_ref, b1_ref, w2_ref, ck_ref, cv_ref, x_s):
    perm = _offset_grouping_perm()
    for k in range(r_ref.shape[0] // PAGE_SIZE):
        _store_by_offset(x_s, k, _nn(perm, r_ref[k * PAGE_SIZE:(k + 1) * PAGE_SIZE, :].astype(BF16)))
    for c, out_ref in ((0, ck_ref), (1, cv_ref)):
        both = None
        for l in range(CMP_STRIDE):
            y = _nn(x_s[l, :, c * GRP_W:(c + 1) * GRP_W].astype(BF16), w1_ref[c, l])
            both = y if both is None else both + y
        first, second = both[:, :GRP_W], both[:, GRP_W:]
        n = first.shape[0]
        hid = _gelu(first + pltpu.roll(second, n - 1, 0) + b1_ref[c])
        out_ref[0] = _nn(hid.astype(BF16), w2_ref[c])


def compress_prompt(rows, bsz, w1bd, b1t, w2bd):
    t = rows.shape[0] // bsz
    n = t // CMP_STRIDE
    assert t % PAGE_SIZE == 0
    out = jax.ShapeDtypeStruct((bsz, n, GRP_W), F32)
    return pl.pallas_call(
        _cmp_prompt_body,
        out_shape=(out, out),
        grid=(bsz,),
        in_specs=[pl.BlockSpec((t, ROW_W), lambda i: (i, 0)),
                  pl.BlockSpec(w1bd.shape, lambda i: (0, 0, 0, 0)),
                  pl.BlockSpec(b1t.shape, lambda i: (0, 0, 0)),
                  pl.BlockSpec(w2bd.shape, lambda i: (0, 0, 0))],
        out_specs=(pl.BlockSpec((1, n, GRP_W), lambda i: (i, 0, 0)),
                   pl.BlockSpec((1, n, GRP_W), lambda i: (i, 0, 0))),
        scratch_shapes=[pltpu.VMEM((CMP_STRIDE, n, ROW_W), F32)],
        compiler_params=_params(1), name="compress_prompt",
    )(rows, w1bd, b1t, w2bd)


NSA_KEY_CHUNK = 512


def _nsa_prompt_body(q_ref, gate_ref, ck_ref, cv_ref, sel_ref, win_ref, ov_ref, ex_ref, bg_ref,
                     o_ref, mask_s, m_s, l_s, acc_s, ocmp_s, *, tq, t_len):
    i = pl.program_id(1)
    q0 = i * tq
    q = q_ref[...] * (ATTN_SCALE * LOG2E)
    pos = q0 + _iota((tq, 1), 0)
    pos4 = jnp.concatenate([pos] * A_REP, axis=0)
    n_cmp_rows = ck_ref.shape[1]
    n_lane = _iota((1, n_cmp_rows), 1)
    valid_cmp = (n_lane * CMP_STRIDE + (CMP_LEN - 1) <= pos) & (n_lane < n_cmp_rows - 1)
    bias_cmp = jnp.where(valid_cmp, 0.0, NEG_INF)
    qgs = [_stack_heads(q, g).astype(BF16) for g in range(A_KV)]

    def add_bias(s, bias):
        return jnp.concatenate([s[r * tq:(r + 1) * tq] + bias for r in range(A_REP)], axis=0)

    imps = []
    for g in range(A_KV):
        gs = slice(g * A_DH, (g + 1) * A_DH)
        s_c = add_bias(_nt(qgs[g], ck_ref[0, :, gs].astype(BF16)), bias_cmp)
        m_c = jnp.max(s_c, axis=1, keepdims=True)
        e_c = jnp.exp2(s_c - jnp.where(m_c == NEG_INF, 0.0, m_c))
        p_cmp = e_c / jnp.maximum(jnp.sum(e_c, axis=1, keepdims=True), 1e-30)
        ocmp_s[g] = _nn(p_cmp.astype(BF16), cv_ref[0, :, gs].astype(BF16))
        p_sum = p_cmp[0:tq]
        for r in range(1, A_REP):
            p_sum = p_sum + p_cmp[r * tq:(r + 1) * tq]
        imps.append(_block_importance(p_sum, ov_ref))
    blk_lane = _iota((A_KV * tq, LANE), 1)
    score = _block_scores(jnp.concatenate(imps, axis=0), pos4, blk_lane)
    few_blocks = q0 + tq <= N_SEL * SEL_BLK

    @pl.when(few_blocks)
    def _():
        mask_s[...] = jnp.where(score == NEG_INF, 0.0, 1.0)

    @pl.when(jnp.logical_not(few_blocks))
    def _():
        mask_s[...] = _topk_mask(score, blk_lane.astype(F32), N_SEL)

    m_s[...] = jnp.full_like(m_s, NEG_INF)
    l_s[...] = jnp.zeros_like(l_s)
    acc_s[...] = jnp.zeros_like(acc_s)
    kc = NSA_KEY_CHUNK
    for c in range(t_len // kc):
        @pl.when(c * kc < q0 + tq)
        def _(c=c):
            key_lane = c * kc + _iota((1, kc), 1)
            causal_bias = jnp.where(key_lane <= pos, 0.0, MASKED)
            for g in range(A_KV):
                blk_bias = _nn((mask_s[g * tq:(g + 1) * tq, :] - 1.0).astype(BF16),
                               ex_ref[:, c * kc:(c + 1) * kc])
                k = sel_ref[c * kc:(c + 1) * kc, g * A_DH:(g + 1) * A_DH].astype(BF16)
                v = sel_ref[c * kc:(c + 1) * kc,
                            GRP_W + g * A_DH:GRP_W + (g + 1) * A_DH].astype(BF16)
                sc = add_bias(_nt(qgs[g], k), blk_bias + causal_bias)
                m_old = m_s[g]
                m_new = jnp.maximum(m_old, jnp.max(sc, axis=1, keepdims=True))
                alpha = jnp.exp2(m_old - m_new)
                e = jnp.exp2(sc - m_new)
                l_s[g] = alpha * l_s[g] + jnp.sum(e, axis=1, keepdims=True)
                acc_s[g] = alpha * acc_s[g] + _nn(e.astype(BF16), v)
                m_s[g] = m_new

    gate = jax.nn.sigmoid(gate_ref[...] + bg_ref[...])
    band = WINDOW + tq
    w_start = pl.multiple_of(jnp.maximum(q0 - WINDOW, 0), tq)
    kpos_w = w_start + _iota((1, band), 1)
    diff_w = pos - kpos_w
    bias_w = jnp.where((diff_w >= 0) & (diff_w <= WINDOW), 0.0, MASKED)
    for g in range(A_KV):
        gs = slice(g * A_DH, (g + 1) * A_DH)
        vs_ = slice(GRP_W + g * A_DH, GRP_W + (g + 1) * A_DH)
        kw = win_ref[pl.ds(w_start, band), gs].astype(BF16)
        vw = win_ref[pl.ds(w_start, band), vs_].astype(BF16)
        s_w = add_bias(_nt(qgs[g], kw), bias_w)
        e_w = jnp.exp2(s_w - jnp.max(s_w, axis=1, keepdims=True))
        o_win = _nn(e_w.astype(BF16), vw) / jnp.sum(e_w, axis=1, keepdims=True)
        o_sel = acc_s[g] / jnp.maximum(l_s[g], 1e-30)
        _combine_heads(o_ref, gate, ocmp_s[g], o_sel, o_win, g, tq)


def nsa_prompt_attention(q, gate_pre, ck, cv, sel_rows, win_rows, b_gate_row, bsz, t_len, tq):
    n = q.shape[0]
    n_q = t_len // tq
    n_blk = t_len // SEL_BLK
    assert t_len % NSA_KEY_CHUNK == 0 and n_blk <= LANE
    ov = jnp.pad(_overlap_matrix(ck.shape[1], n_blk, 0), ((0, 0), (0, LANE - n_blk)))
    ex = (np.arange(LANE)[:, None] == (np.arange(t_len)[None, :] // SEL_BLK))
    ex = jnp.asarray(ex * -MASKED, BF16)
    rows4 = A_REP * tq
    body = functools.partial(_nsa_prompt_body, tq=tq, t_len=t_len)
    return pl.pallas_call(
        body,
        out_shape=jax.ShapeDtypeStruct((n, A_QW), F32),
        grid=(bsz, n_q),
        in_specs=[pl.BlockSpec((tq, A_QW), lambda b, i: (b * n_q + i, 0)),
                  pl.BlockSpec((tq, LANE), lambda b, i: (b * n_q + i, 0)),
                  pl.BlockSpec((1,) + ck.shape[1:], lambda b, i: (b, 0, 0)),
                  pl.BlockSpec((1,) + cv.shape[1:], lambda b, i: (b, 0, 0)),
                  pl.BlockSpec((t_len, ROW_W), lambda b, i: (b, 0)),
                  pl.BlockSpec((t_len, ROW_W), lambda b, i: (b, 0)),
                  pl.BlockSpec(ov.shape, lambda b, i: (0, 0)),
                  pl.BlockSpec(ex.shape, lambda b, i: (0, 0)),
                  pl.BlockSpec((1, LANE), lambda b, i: (0, 0))],
        out_specs=pl.BlockSpec((tq, A_QW), lambda b, i: (b * n_q + i, 0)),
        scratch_shapes=[pltpu.VMEM((A_KV * tq, LANE), F32),
                        pltpu.VMEM((A_KV, rows4, 1), F32),
                        pltpu.VMEM((A_KV, rows4, 1), F32),
                        pltpu.VMEM((A_KV, rows4, A_DH), F32),
                        pltpu.VMEM((A_KV, rows4, A_DH), F32)],
        compiler_params=_params(2), name="nsa_prompt_attention",
    )(q, gate_pre, ck, cv, sel_rows, win_rows, ov, ex, b_gate_row)


DEC_ROWS = 8
CMP_PAGES_PER_STEP = 32
SEL_PAGES_PER_STEP = 32


def _dec_cmp_body(pt_ref, *refs, past_len, n_blk_pad):
    del pt_ref
    pages = refs[:CMP_PAGES_PER_STEP]
    (w1_ref, b1_ref, w2_ref, q_ref, ov_ref, ocmp_ref, msel_ref,
     ck_s, cv_s, carry_s, x_s) = refs[CMP_PAGES_PER_STEP:]
    s = pl.program_id(1)
    rows = CMP_PAGES_PER_STEP * R16_PER_PAGE

    @pl.when(s == 0)
    def _():
        carry_s[...] = jnp.zeros_like(carry_s)

    perm = _offset_grouping_perm()
    for k, p in enumerate(pages):
        _store_by_offset(x_s, k, _nt(perm, p[0].astype(BF16)))

    rid = _iota((rows, 1), 0)
    for c, dst in ((0, ck_s), (1, cv_s)):
        both = None
        for l in range(CMP_STRIDE):
            x = x_s[l, :, c * GRP_W:(c + 1) * GRP_W].astype(BF16)
            y = _nn(x, w1_ref[c, l])
            both = y if both is None else both + y
        first, second = both[:, :GRP_W], both[:, GRP_W:]
        prev = carry_s[c, SUBLANE - 1:SUBLANE, :]
        shifted = jnp.where(rid == 0, prev, pltpu.roll(first, 1, 0))
        hid = _gelu(shifted + second + b1_ref[c])
        dst[pl.ds(pl.multiple_of(s * rows, rows), rows), :] = _nn(hid.astype(BF16), w2_ref[c])
        carry_s[c] = first[rows - SUBLANE:rows, :]

    @pl.when(s == pl.num_programs(1) - 1)
    def _():
        n_rows = ck_s.shape[0]
        q = q_ref[0]
        t_col = _iota((DEC_ROWS, 1), 0)
        pos = past_len + t_col
        pos4 = jnp.concatenate([pos] * A_REP, axis=0)
        m_lane = _iota((1, n_rows), 1)
        valid = (m_lane >= 1) & ((m_lane - 1) * CMP_STRIDE + (CMP_LEN - 1) <= pos4)
        imps = []
        for g in range(A_KV):
            gs = slice(g * A_DH, (g + 1) * A_DH)
            qg = _stack_heads(q, g).astype(BF16)
            p = _masked_softmax(_nt(qg, ck_s[:, gs].astype(BF16)) * ATTN_SCALE, valid)
            o = _nn(p.astype(BF16), cv_s[:, gs].astype(BF16))
            for r in range(A_REP):
                h = g * A_REP + r
                ocmp_ref[0, :, h * A_DH:(h + 1) * A_DH] = o[r * DEC_ROWS:(r + 1) * DEC_ROWS]
            p_sum = p[0:DEC_ROWS]
            for r in range(1, A_REP):
                p_sum = p_sum + p[r * DEC_ROWS:(r + 1) * DEC_ROWS]
            imps.append(_block_importance(p_sum, ov_ref))
        imp = jnp.concatenate(imps, axis=0)
        lane = _iota((A_KV * DEC_ROWS, n_blk_pad), 1)
        score = _block_scores(imp, pos4, lane)
        mask = _topk_mask(score, lane.astype(F32), N_SEL)
        for j in range(n_blk_pad // LANE):
            msel_ref[0, j] = mask[:, j * LANE:(j + 1) * LANE]


def nsa_decode_compress(cache_t, page_table, w1bd, b1t, w2bd, q8, past_len, n_blk_pad):
    dbsz, n_pages = page_table.shape
    assert n_pages % CMP_PAGES_PER_STEP == 0
    n_steps = n_pages // CMP_PAGES_PER_STEP
    n_rows = n_pages * R16_PER_PAGE
    n_tiles = n_blk_pad // LANE
    ov = _overlap_matrix(n_rows, n_blk_pad, 1)

    def page_spec(k):
        return pl.BlockSpec((1, ROW_W, PAGE_SIZE),
                            lambda b, s, pt: (pt[b, s * CMP_PAGES_PER_STEP + k], 0, 0))

    grid_spec = pltpu.PrefetchScalarGridSpec(
        num_scalar_prefetch=1,
        grid=(dbsz, n_steps),
        in_specs=[page_spec(k) for k in range(CMP_PAGES_PER_STEP)] + [
            pl.BlockSpec(w1bd.shape, lambda b, s, pt: (0, 0, 0, 0)),
            pl.BlockSpec(b1t.shape, lambda b, s, pt: (0, 0, 0)),
            pl.BlockSpec(w2bd.shape, lambda b, s, pt: (0, 0, 0)),
            pl.BlockSpec((1, DEC_ROWS, A_QW), lambda b, s, pt: (b, 0, 0)),
            pl.BlockSpec(ov.shape, lambda b, s, pt: (0, 0))],
        out_specs=(pl.BlockSpec((1, DEC_ROWS, A_QW), lambda b, s, pt: (b, 0, 0)),
                   pl.BlockSpec((1, n_tiles, A_KV * DEC_ROWS, LANE), lambda b, s, pt: (b, 0, 0, 0))),
        scratch_shapes=[pltpu.VMEM((n_rows, GRP_W), F32), pltpu.VMEM((n_rows, GRP_W), F32),
                        pltpu.VMEM((2, SUBLANE, GRP_W), F32),
                        pltpu.VMEM((CMP_STRIDE, CMP_PAGES_PER_STEP * R16_PER_PAGE, ROW_W), F32)])
    body = functools.partial(_dec_cmp_body, past_len=past_len, n_blk_pad=n_blk_pad)
    return pl.pallas_call(
        body,
        out_shape=(jax.ShapeDtypeStruct((dbsz, DEC_ROWS, A_QW), F32),
                   jax.ShapeDtypeStruct((dbsz, n_tiles, A_KV * DEC_ROWS, LANE), F32)),
        grid_spec=grid_spec,
        compiler_params=_params(2), name="nsa_decode_compress",
    )(page_table, *([cache_t] * CMP_PAGES_PER_STEP), w1bd, b1t, w2bd, q8, ov)


def _dec_sel_body(pt_ref, *refs, past_len, t_new):
    del pt_ref
    pages = refs[:SEL_PAGES_PER_STEP]
    (q_ref, msel_ref, ocmp_ref, snew_ref, wpast_ref, wnew_ref, gate_ref, bg_ref,
     o_ref, m_s, l_s, acc_s) = refs[SEL_PAGES_PER_STEP:]
    s = pl.program_id(1)
    keys = SEL_PAGES_PER_STEP * PAGE_SIZE
    blks = keys // SEL_BLK
    steps_per_tile = LANE // blks

    @pl.when(s == 0)
    def _():
        m_s[...] = jnp.full_like(m_s, NEG_INF)
        l_s[...] = jnp.zeros_like(l_s)
        acc_s[...] = jnp.zeros_like(acc_s)

    q = q_ref[0]
    tile = s // steps_per_tile
    first_blk = (s - tile * steps_per_tile) * blks
    expand = (_iota((LANE, keys), 0)
              == first_blk + lax.shift_right_arithmetic(_iota((LANE, keys), 1), SEL_BLK_LOG2))
    expand = jnp.where(expand, 1.0, 0.0).astype(BF16)
    m_keys = _nn(msel_ref[0, tile].astype(BF16), expand)

    for g in range(A_KV):
        gs = slice(g * A_DH, (g + 1) * A_DH)
        vs_ = slice(GRP_W + g * A_DH, GRP_W + (g + 1) * A_DH)
        qg = _stack_heads(q, g).astype(BF16)
        k_t = jnp.concatenate([p[0, gs, :] for p in pages], axis=1).astype(BF16)
        v_t = jnp.concatenate([p[0, vs_, :] for p in pages], axis=1).astype(BF16)
        valid = jnp.concatenate([m_keys[g * DEC_ROWS:(g + 1) * DEC_ROWS]] * A_REP, axis=0) > 0.5
        _online_softmax_step(m_s, l_s, acc_s, g, _nn(qg, k_t) * ATTN_SCALE, valid,
                             lambda e, v_t=v_t: _nt(e, v_t))

    @pl.when(s == pl.num_programs(1) - 1)
    def _():
        t_col = _iota((DEC_ROWS, 1), 0)
        t4 = jnp.concatenate([t_col] * A_REP, axis=0)
        t_key = _iota((1, DEC_ROWS), 1)
        valid_new = (t_key <= t4) & (t_key < t_new)
        gate = jax.nn.sigmoid(gate_ref[0] + bg_ref[...])
        n_win = wpast_ref.shape[2]
        kpos_w = past_len - n_win + _iota((1, n_win), 1)
        diff_w = (past_len + t4) - kpos_w
        valid_wp = (diff_w >= 0) & (diff_w <= WINDOW)
        new_blk = past_len // SEL_BLK
        new_tile, new_lane = new_blk // LANE, new_blk % LANE
        for g in range(A_KV):
            gs = slice(g * A_DH, (g + 1) * A_DH)
            vs_ = slice(GRP_W + g * A_DH, GRP_W + (g + 1) * A_DH)
            qg = _stack_heads(q, g).astype(BF16)
            mnew = msel_ref[0, new_tile, g * DEC_ROWS:(g + 1) * DEC_ROWS, new_lane:new_lane + 1]
            mnew4 = jnp.concatenate([mnew] * A_REP, axis=0)
            v_new = snew_ref[0, :, vs_].astype(BF16)
            _online_softmax_step(m_s, l_s, acc_s, g,
                                 _nt(qg, snew_ref[0, :, gs].astype(BF16)) * ATTN_SCALE,
                                 valid_new & (mnew4 > 0.5), lambda e, v_new=v_new: _nn(e, v_new))
            o_sel = acc_s[g] / jnp.maximum(l_s[g], 1e-30)

            s_p = jnp.where(valid_wp, _nn(qg, wpast_ref[0, gs, :].astype(BF16)) * ATTN_SCALE, NEG_INF)
            s_n = jnp.where(valid_new, _nt(qg, wnew_ref[0, :, gs].astype(BF16)) * ATTN_SCALE, NEG_INF)
            mx = jnp.maximum(jnp.max(s_p, axis=1, keepdims=True), jnp.max(s_n, axis=1, keepdims=True))
            mx = jnp.where(mx == NEG_INF, 0.0, mx)
            e_p = jnp.exp(s_p - mx)
            e_n = jnp.exp(s_n - mx)
            den = jnp.maximum(jnp.sum(e_p, axis=1, keepdims=True)
                              + jnp.sum(e_n, axis=1, keepdims=True), 1e-30)
            o_win = (_nt(e_p.astype(BF16), wpast_ref[0, vs_, :].astype(BF16))
                     + _nn(e_n.astype(BF16), wnew_ref[0, :, vs_].astype(BF16))) / den
            o_cmp = _stack_heads(ocmp_ref[0], g)
            _combine_heads(o_ref.at[0], gate, o_cmp, o_sel, o_win, g, DEC_ROWS)


def nsa_decode_attention(cache_t, page_table, q8, msel, ocmp, sel_new, win_past_t, win_new,
                         gate8, b_gate_row, past_len, t_new):
    dbsz, n_pages = page_table.shape
    keys = SEL_PAGES_PER_STEP * PAGE_SIZE
    assert n_pages % SEL_PAGES_PER_STEP == 0 and past_len % SEL_BLK == 0
    assert LANE % (keys // SEL_BLK) == 0
    n_steps = n_pages // SEL_PAGES_PER_STEP
    n_win = win_past_t.shape[2]

    def page_spec(k):
        return pl.BlockSpec((1, ROW_W, PAGE_SIZE),
                            lambda b, s, pt: (pt[b, s * SEL_PAGES_PER_STEP + k], 0, 0))

    per_b = lambda shp: pl.BlockSpec((1,) + shp, lambda b, s, pt: (b,) + (0,) * len(shp))
    grid_spec = pltpu.PrefetchScalarGridSpec(
        num_scalar_prefetch=1,
        grid=(dbsz, n_steps),
        in_specs=[page_spec(k) for k in range(SEL_PAGES_PER_STEP)] + [
            per_b((DEC_ROWS, A_QW)), per_b(msel.shape[1:]), per_b((DEC_ROWS, A_QW)),
            per_b((DEC_ROWS, ROW_W)), per_b((ROW_W, n_win)), per_b((DEC_ROWS, ROW_W)),
            per_b((DEC_ROWS, LANE)), pl.BlockSpec((1, LANE), lambda b, s, pt: (0, 0))],
        out_specs=per_b((DEC_ROWS, A_QW)),
        scratch_shapes=[pltpu.VMEM((A_KV, A_REP * DEC_ROWS, 1), F32),
                        pltpu.VMEM((A_KV, A_REP * DEC_ROWS, 1), F32),
                        pltpu.VMEM((A_KV, A_REP * DEC_ROWS, A_DH), F32)])
    body = functools.partial(_dec_sel_body, past_len=past_len, t_new=t_new)
    return pl.pallas_call(
        body,
        out_shape=jax.ShapeDtypeStruct((dbsz, DEC_ROWS, A_QW), F32),
        grid_spec=grid_spec,
        compiler_params=_params(2), name="nsa_decode_attention",
    )(page_table, *([cache_t] * SEL_PAGES_PER_STEP), q8, msel, ocmp, sel_new, win_past_t,
      win_new, gate8, b_gate_row)


def tokens_minor(rows):
    n, t = rows.shape[:2]
    return jnp.transpose(rows, (0, 2, 3, 4, 1)).reshape(n, ROW_W, t)


PROMPT_TM = 256
PEER_TM = 512
PEER_TE = 2048
PEER_SUB_EXPERTS = 1024
NSA_TQ = 128


def kernel(x_prompt, x_sample, cache_cmp_kv, cache_sel_kv, state_win_kv, state_C, state_n, state_m,
           page_table, norm_mix, norm_ffn, norm_final, mlstm_w_in, mlstm_b_i, mlstm_b_f,
           mlstm_head_norm, mlstm_w_out, nsa_w_in, nsa_b_gate, nsa_cmp_w1, nsa_cmp_b1, nsa_cmp_w2,
           nsa_w_out, peer_w_q, peer_sub_keys, peer_u, peer_v):
    bsz, t_len, d = x_prompt.shape
    dbsz, t_s, _ = x_sample.shape
    n_pages = page_table.shape[1]
    past_len = n_pages * PAGE_SIZE
    assert norm_mix.shape[0] == 2 and d == D_MODEL and t_s <= DEC_ROWS // 2
    assert t_len % M_CHUNK == 0 and t_len % NSA_TQ == 0 and t_len >= WINDOW

    xp = x_prompt.reshape(bsz * t_len, d)
    reps = DEC_ROWS // t_s
    xs = jnp.concatenate([x_sample] * reps, axis=1).reshape(dbsz * DEC_ROWS, d)
    n_s = dbsz * DEC_ROWS

    u_all = peer_u.astype(BF16)
    vt_all = jnp.swapaxes(peer_v, 1, 2).astype(BF16)

    def peer(x, layer, tm, out_norm=None):
        keys = peer_sub_keys[layer].astype(BF16)
        return peer_ffn_residual(x, norm_ffn[layer], peer_w_q[layer].astype(BF16),
                                 keys[:, 0], keys[:, 1], u_all, vt_all, layer, tm, PEER_TE, out_norm)

    w_in = jnp.pad(mlstm_w_in[0], ((0, 0), (0, M_PROJ_PAD - M_PROJ))).astype(BF16)
    w_out = mlstm_w_out[0].astype(BF16)
    gate_bias = jnp.pad(jnp.concatenate([mlstm_b_i[0], mlstm_b_f[0]]),
                        (0, LANE - 2 * M_HEADS)).reshape(1, LANE)
    proj_p = rms_matmul(xp, norm_mix[0], w_in, PROMPT_TM).reshape(bsz, t_len, M_PROJ_PAD)
    proj_s = rms_matmul(xs, norm_mix[0], w_in, n_s).reshape(dbsz, DEC_ROWS, M_PROJ_PAD)
    hid_p, c_p, n_p, m_p = mlstm_scan(
        proj_p, jnp.zeros((bsz, M_HEADS, M_DK, M_DV), F32), jnp.zeros((bsz, M_HEADS, M_DK), F32),
        jnp.zeros((bsz, M_HEADS), F32), gate_bias, mlstm_head_norm[0], M_CHUNK, M_CHUNK - 1)
    hid_s, c_s, n_st, m_st = mlstm_scan(
        proj_s, state_C[0], state_n[0], state_m[0], gate_bias, mlstm_head_norm[0],
        DEC_ROWS, t_s - 1)
    xp = matmul_residual(hid_p.reshape(bsz * t_len, d), w_out, xp, PROMPT_TM)
    xs = matmul_residual(hid_s.reshape(n_s, d), w_out, xs, n_s)
    xp = peer(xp, 0, PEER_TM)
    xs = peer(xs, 0, n_s)

    w_in_a = jnp.pad(nsa_w_in[0], ((0, 0), (0, A_PROJ_PAD - A_PROJ))).astype(BF16)
    w_out_a = nsa_w_out[0].astype(BF16)
    b_gate_row = jnp.pad(nsa_b_gate[0].reshape(-1), (0, LANE - 3 * A_HEADS)).reshape(1, LANE)
    w1bd, b1t, w2bd = _compress_weights(nsa_cmp_w1[0], nsa_cmp_b1[0], nsa_cmp_w2[0])
    tabs_p = rope_tables(jnp.arange(t_len, dtype=jnp.int32))
    pos_s = past_len + jnp.arange(DEC_ROWS, dtype=jnp.int32)
    tabs_s = tuple(jnp.tile(a, (dbsz, 1)) for a in rope_tables(pos_s))

    q_p, cmp_p, sel_p, win_p, gate_p = nsa_project(xp, norm_mix[1], w_in_a, tabs_p, PROMPT_TM)
    q_s, cmp_s, sel_s, win_s, gate_s = nsa_project(xs, norm_mix[1], w_in_a, tabs_s, n_s)

    ck, cv = compress_prompt(cmp_p, bsz, w1bd, b1t, w2bd)
    o_p = nsa_prompt_attention(q_p, gate_p, ck, cv, sel_p, win_p, b_gate_row, bsz, t_len, NSA_TQ)
    xp = matmul_residual(o_p, w_out_a, xp, PROMPT_TM)

    n_blk = -(-(past_len + t_s) // SEL_BLK)
    n_blk_pad = -(-n_blk // LANE) * LANE
    q8 = q_s.reshape(dbsz, DEC_ROWS, A_QW)
    o_cmp, msel = nsa_decode_compress(tokens_minor(cache_cmp_kv[0]), page_table, w1bd, b1t, w2bd,
                                      q8, past_len, n_blk_pad)
    o_s = nsa_decode_attention(
        tokens_minor(cache_sel_kv[0]), page_table, q8, msel, o_cmp,
        sel_s.reshape(dbsz, DEC_ROWS, ROW_W), tokens_minor(state_win_kv[0]),
        win_s.reshape(dbsz, DEC_ROWS, ROW_W), gate_s.reshape(dbsz, DEC_ROWS, LANE), b_gate_row,
        past_len, t_s)
    xs = matmul_residual(o_s.reshape(n_s, d), w_out_a, xs, n_s)
    y_p = peer(xp, 1, PEER_TM, norm_final).reshape(bsz, t_len, d)
    y_s = peer(xs, 1, n_s, norm_final).reshape(dbsz, DEC_ROWS, d)[:, :t_s]

    row_shape = (2, A_KV, A_DH)
    kv_p = lambda a: a.reshape((1, bsz, t_len) + row_shape)
    kv_s = lambda a: a.reshape((dbsz, DEC_ROWS) + row_shape)[None, :, :t_s]
    win_rows = min(WINDOW, t_len)
    win_buf_p = win_p.reshape((bsz, t_len) + row_shape)[None, :, t_len - win_rows:]
    win_new_s = win_s.reshape((dbsz, DEC_ROWS) + row_shape)[:, :t_s]
    win_buf_s = jnp.concatenate([state_win_kv[0], win_new_s], axis=1)[None, :, t_s:]
    return (y_p, y_s,
            kv_p(cmp_p), kv_p(sel_p), win_buf_p,
            c_p[None], n_p[None], m_p[None],
            kv_s(cmp_s), kv_s(sel_s), win_buf_s,
            c_s[None], n_st[None], m_st[None])
```

```python
import functools

import numpy as np
import jax
import jax.numpy as jnp
from jax import lax
from jax.experimental import pallas as pl
from jax.experimental.pallas import tpu as pltpu

F32 = jnp.float32
BF16 = jnp.bfloat16
HI = lax.Precision.HIGHEST

V7X_VMEM_BYTES = 64 * 1024 * 1024
VMEM_LIMIT = V7X_VMEM_BYTES - 8 * 1024 * 1024
LANE = 128
SUBLANE = 8

D_MODEL = 1024
NORM_EPS = 1e-6

M_HEADS = 8
M_DK = 64
M_DV = 128
M_CHUNK = 64
M_PROJ = 2 * M_HEADS * M_DK + 2 * M_HEADS * M_DV + 2 * M_HEADS
M_PROJ_PAD = 3200

A_HEADS = 16
A_KV = 4
A_REP = 4
A_DH = 64
ROT_DIM = 16
ROPE_THETA = 500000.0
CMP_STRIDE = 16
CMP_LEN = 32
SEL_BLK = 64
SEL_BLK_LOG2 = 6
N_SEL = 16
WINDOW = 512
A_QW = 1024
A_KVW = 1536
A_PROJ = A_QW + A_KVW + 3 * A_HEADS
A_PROJ_PAD = 2688
ATTN_SCALE = A_DH ** -0.5
GRP_W = A_KV * A_DH
ROW_W = 2 * GRP_W
PAGE_SIZE = 128

P_HEADS = 8
P_NKEYS = 128
P_EXPERTS = P_NKEYS * P_NKEYS
P_QDIM = 256
P_TOPK = 16

NEG_INF = float("-inf")
LOG2E = 1.4426950408889634
MASKED = -(2.0 ** 100)


def _params(n_axes):
    return pltpu.CompilerParams(dimension_semantics=("arbitrary",) * n_axes,
                                vmem_limit_bytes=VMEM_LIMIT)


def _nn(a, b, precision=None):
    return jnp.dot(a, b, preferred_element_type=F32, precision=precision)


def _nt(a, b, precision=None):
    return lax.dot_general(a, b, (((1,), (1,)), ((), ())), preferred_element_type=F32,
                           precision=precision)


def _tn(a, b, precision=None):
    return lax.dot_general(a, b, (((0,), (0,)), ((), ())), preferred_element_type=F32,
                           precision=precision)


def _gelu(x):
    return 0.5 * x * (1.0 + lax.erf(x * (2.0 ** -0.5)))


def _rms(x, g):
    return x * lax.rsqrt(jnp.mean(x * x, axis=-1, keepdims=True) + NORM_EPS) * g


def _masked_softmax(s, valid):
    s = jnp.where(valid, s, NEG_INF)
    m = jnp.max(s, axis=-1, keepdims=True)
    m = jnp.where(m == NEG_INF, 0.0, m)
    e = jnp.exp(s - m)
    return e / jnp.maximum(jnp.sum(e, axis=-1, keepdims=True), 1e-30)


def _iota(shape, dim):
    return lax.broadcasted_iota(jnp.int32, shape, dim)


def _rms_mm_body(x_ref, g_ref, w_ref, o_ref):
    o_ref[...] = _nn(_rms(x_ref[...], g_ref[...]).astype(BF16), w_ref[...])


def rms_matmul(x, g, w, tm):
    n, d = x.shape
    nout = w.shape[1]
    return pl.pallas_call(
        _rms_mm_body,
        out_shape=jax.ShapeDtypeStruct((n, nout), F32),
        grid=(n // tm,),
        in_specs=[pl.BlockSpec((tm, d), lambda i: (i, 0)),
                  pl.BlockSpec((1, d), lambda i: (0, 0)),
                  pl.BlockSpec((d, nout), lambda i: (0, 0))],
        out_specs=pl.BlockSpec((tm, nout), lambda i: (i, 0)),
        compiler_params=_params(1), name="rms_matmul",
    )(x, g.reshape(1, d), w)


def _mm_res_body(a_ref, w_ref, x_ref, o_ref):
    o_ref[...] = x_ref[...] + _nn(a_ref[...].astype(BF16), w_ref[...])


def matmul_residual(a, w, x, tm):
    n, k = a.shape
    d = w.shape[1]
    return pl.pallas_call(
        _mm_res_body,
        out_shape=jax.ShapeDtypeStruct((n, d), F32),
        grid=(n // tm,),
        in_specs=[pl.BlockSpec((tm, k), lambda i: (i, 0)),
                  pl.BlockSpec((k, d), lambda i: (0, 0)),
                  pl.BlockSpec((tm, d), lambda i: (i, 0))],
        out_specs=pl.BlockSpec((tm, d), lambda i: (i, 0)),
        compiler_params=_params(1), name="matmul_residual",
    )(a, w, x)


def _nsa_proj_body(x_ref, g_ref, w_ref, c_ref, sa_ref, sb_ref,
                   q_ref, cmp_ref, sel_ref, win_ref, gate_ref, *t_refs):
    y = _nn(_rms(x_ref[...], g_ref[...]).astype(BF16), w_ref[...])
    c = c_ref[...]
    sa = sa_ref[...]
    sb = sb_ref[...]

    def rot(z):
        return (z * c + pltpu.roll(z, GRP_W - ROT_DIM // 2, 1) * sa
                + pltpu.roll(z, ROT_DIM // 2, 1) * sb)

    for j in range(A_QW // GRP_W):
        q_ref[:, j * GRP_W:(j + 1) * GRP_W] = rot(y[:, j * GRP_W:(j + 1) * GRP_W])
    for ref, base in ((cmp_ref, A_QW), (sel_ref, A_QW + ROW_W), (win_ref, A_QW + 2 * ROW_W)):
        ref[:, 0:GRP_W] = rot(y[:, base:base + GRP_W])
        ref[:, GRP_W:ROW_W] = y[:, base + GRP_W:base + ROW_W]
    gate_ref[...] = y[:, A_QW + A_KVW:A_PROJ_PAD]
    for t_ref, src in zip(t_refs, (cmp_ref, sel_ref)):
        t_ref[0] = src[...].T


def nsa_project(x, g, w, tabs, tm, tokens_minor_seqs=None):
    n, d = x.shape
    n_tab = tabs[0].shape[0] // tm
    tab_spec = pl.BlockSpec((tm, GRP_W), lambda i: (i % n_tab, 0))
    row = lambda w_: pl.BlockSpec((tm, w_), lambda i: (i, 0))
    out_shape = [jax.ShapeDtypeStruct((n, A_QW), F32),
                 jax.ShapeDtypeStruct((n, ROW_W), F32),
                 jax.ShapeDtypeStruct((n, ROW_W), F32),
                 jax.ShapeDtypeStruct((n, ROW_W), F32),
                 jax.ShapeDtypeStruct((n, LANE), F32)]
    out_specs = [row(A_QW), row(ROW_W), row(ROW_W), row(ROW_W), row(LANE)]
    if tokens_minor_seqs is not None:
        t_len = n // tokens_minor_seqs
        n_t = t_len // tm
        for _ in range(2):
            out_shape.append(jax.ShapeDtypeStruct((tokens_minor_seqs, ROW_W, t_len), F32))
            out_specs.append(pl.BlockSpec((1, ROW_W, tm), lambda i: (i // n_t, 0, i % n_t)))
    return pl.pallas_call(
        _nsa_proj_body,
        out_shape=tuple(out_shape),
        grid=(n // tm,),
        in_specs=[row(d), pl.BlockSpec((1, d), lambda i: (0, 0)),
                  pl.BlockSpec((d, A_PROJ_PAD), lambda i: (0, 0)),
                  tab_spec, tab_spec, tab_spec],
        out_specs=tuple(out_specs),
        compiler_params=_params(1), name="nsa_project",
    )(x, g.reshape(1, d), w, *tabs)


def rope_tables(pos):
    half = ROT_DIM // 2
    inv_freq = ROPE_THETA ** (-(jnp.arange(half, dtype=F32) * (2.0 / ROT_DIM)))
    ang = pos.astype(F32)[:, None] * inv_freq[None, :]
    cos, sin = jnp.cos(ang), jnp.sin(ang)
    t = pos.shape[0]
    one = jnp.ones((t, A_DH - ROT_DIM), F32)
    zero = jnp.zeros((t, A_DH - ROT_DIM), F32)
    z8 = jnp.zeros((t, half), F32)
    c = jnp.concatenate([cos, cos, one], axis=1)
    sa = jnp.concatenate([-sin, z8, zero], axis=1)
    sb = jnp.concatenate([z8, sin, zero], axis=1)
    return tuple(jnp.tile(a, (1, A_KV)) for a in (c, sa, sb))


def _mlstm_body(q_ref, k_ref, v_ref, og_ref, gt_ref, c0_ref, n0_ref, m0_ref, gb_ref, hn_ref,
                hid_ref, c_out, n_out, m_out, c_s, n_s, m_s, *, chunk, t_last, nb):
    ci = pl.program_id(1)

    @pl.when(ci == 0)
    def _():
        c_s[...] = c0_ref[...]
        n_s[...] = n0_ref[...]
        m_s[...] = m0_ref[...]

    L = chunk
    row = _iota((L, L), 0)
    col = _iota((L, L), 1)
    causal = row >= col
    tril = causal.astype(F32)
    eye8 = (_iota((SUBLANE, LANE), 0) == _iota((SUBLANE, LANE), 1)).astype(F32)
    rlane = _iota((L, LANE), 0)
    lane1 = _iota((1, LANE), 1)
    sel_row = _iota((LANE, LANE), 0)
    ones_l = jnp.ones((L, LANE), BF16)
    ones_k = jnp.ones((M_DK, LANE), BF16)
    ones_v = jnp.ones((M_DV, LANE), BF16)

    def replicate(parts, col_idx):
        sel = jnp.where(sel_row == col_idx, 1.0, 0.0).astype(BF16)
        return _nn(parts[0], sel) + _nn(parts[1], sel) + _nn(parts[2], sel)

    units = [(bi, h) for bi in range(nb) for h in range(M_HEADS)]
    b_parts, g_parts, m_parts, g_rows = [], [], [], []
    for bi in range(nb):
        gt = gt_ref[bi] + gb_ref[...]
        log_f = jnp.minimum(gt, 0.0) - jnp.log1p(jnp.exp(-jnp.abs(gt)))
        ba = _nn(tril, log_f, HI)
        g_mat = gt - pltpu.roll(ba, LANE - M_HEADS, 1)
        b_parts.append(_split3(ba))
        g_parts.append(_split3(g_mat))
        m_parts.append(_split3(jnp.broadcast_to(m_s[bi], (SUBLANE, LANE))))
        g_rows.append(_nt(eye8, g_mat, HI))

    qs, ks, vs, cs, ns = {}, {}, {}, {}, {}
    s_raw, q_c, b_rep, g_rep, m_prev = {}, {}, {}, {}, {}
    for u in units:
        bi, h = u
        qs[u] = q_ref[bi, :, h * M_DK:(h + 1) * M_DK]
        ks[u] = k_ref[bi, :, h * M_DK:(h + 1) * M_DK] * (M_DK ** -0.5)
        vs[u] = v_ref[bi, :, h * M_DV:(h + 1) * M_DV]
        cs[u] = c_s[bi, h]
        ns[u] = n_s[bi, h:h + 1, :]
        s_raw[u] = _nt(qs[u].astype(BF16), ks[u].astype(BF16))
        q_c[u] = _nn(qs[u].astype(BF16), cs[u].astype(BF16))
        b_rep[u] = replicate(b_parts[bi], M_HEADS + h)
        g_rep[u] = replicate(g_parts[bi], h)
        m_prev[u] = replicate(m_parts[bi], h)[0:1, :]

    a_rep, m_t, w_d, w_a = {}, {}, {}, {}
    for u in units:
        bi, h = u
        pm = g_rep[u]
        shift = 1
        while shift < L:
            pm = jnp.where(rlane >= shift, jnp.maximum(pm, pltpu.roll(pm, shift, 0)), pm)
            shift *= 2
        a_rep[u] = b_rep[u] + m_prev[u]
        m_t[u] = jnp.maximum(a_rep[u], b_rep[u] + pm)
        dm = jnp.where(causal, b_rep[u][:, :L] + g_rows[bi][h:h + 1, :], NEG_INF)
        w_d[u] = jnp.exp(dm - m_t[u][:, :L])
        w_a[u] = jnp.exp(a_rep[u] - m_t[u])

    for u in units:
        bi, h = u
        s = s_raw[u] * w_d[u]
        s_b = s.astype(BF16)
        num = w_a[u] * q_c[u] + _nn(s_b, vs[u].astype(BF16))
        den = w_a[u] * _nn((qs[u] * ns[u]).astype(BF16), ones_k) + _nn(s_b, ones_l)
        h_out = num / jnp.maximum(jnp.abs(den), jnp.exp(-m_t[u]))
        msq = _nn((h_out * h_out).astype(BF16), ones_v) * (1.0 / M_DV)
        hid = h_out * lax.rsqrt(msq + NORM_EPS) * hn_ref[:, h * M_DV:(h + 1) * M_DV]
        hid_ref[bi, :, h * M_DV:(h + 1) * M_DV] = (
            jax.nn.sigmoid(og_ref[bi, :, h * M_DV:(h + 1) * M_DV]) * hid)

    m_rows = [m_s[bi] for bi in range(nb)]
    for u in units:
        bi, h = u
        m_new = m_t[u][t_last:t_last + 1, :]
        d_last = jnp.where(rlane <= t_last, b_rep[u][t_last:t_last + 1, :] + g_rep[u], NEG_INF)
        w_last = jnp.exp(d_last - m_new)
        decay = jnp.exp(a_rep[u][t_last:t_last + 1, :] - m_new)
        kw = ks[u] * w_last[:, :M_DK]
        c_s[bi, h] = decay * cs[u] + _tn(kw, vs[u], HI)
        n_s[bi, h:h + 1, :] = decay[:, :M_DK] * ns[u] + jnp.sum(kw, axis=0, keepdims=True)
        m_rows[bi] = jnp.where(lane1 == h, m_new, m_rows[bi])
    for bi in range(nb):
        m_s[bi] = m_rows[bi]

    @pl.when(ci == pl.num_programs(1) - 1)
    def _():
        c_out[...] = c_s[...]
        n_out[...] = n_s[...]
        m_out[...] = m_s[...]


MLSTM_SEQ_PER_STEP = 4


def mlstm_scan(proj, c0, n0, m0, gate_bias, head_norm, chunk, t_last):
    b, t, _ = proj.shape
    nb = MLSTM_SEQ_PER_STEP
    assert b % nb == 0
    n_chunks = t // chunk
    hd = M_HEADS * M_DK
    vd = M_HEADS * M_DV
    m0p = jnp.pad(m0, ((0, 0), (0, LANE - M_HEADS))).reshape(b, 1, LANE)
    body = functools.partial(_mlstm_body, chunk=chunk, t_last=t_last, nb=nb)
    hid, c_t, n_t, m_t = pl.pallas_call(
        body,
        out_shape=(jax.ShapeDtypeStruct((b, t, vd), F32),
                   jax.ShapeDtypeStruct((b, M_HEADS, M_DK, M_DV), F32),
                   jax.ShapeDtypeStruct((b, M_HEADS, M_DK), F32),
                   jax.ShapeDtypeStruct((b, 1, LANE), F32)),
        grid=(b // nb, n_chunks),
        in_specs=[pl.BlockSpec((nb, chunk, hd), lambda i, c: (i, c, 0)),
                  pl.BlockSpec((nb, chunk, hd), lambda i, c: (i, c, 1)),
                  pl.BlockSpec((nb, chunk, vd), lambda i, c: (i, c, 1)),
                  pl.BlockSpec((nb, chunk, vd), lambda i, c: (i, c, 2)),
                  pl.BlockSpec((nb, chunk, LANE), lambda i, c: (i, c, (2 * hd + 2 * vd) // LANE)),
                  pl.BlockSpec((nb, M_HEADS, M_DK, M_DV), lambda i, c: (i, 0, 0, 0)),
                  pl.BlockSpec((nb, M_HEADS, M_DK), lambda i, c: (i, 0, 0)),
                  pl.BlockSpec((nb, 1, LANE), lambda i, c: (i, 0, 0)),
                  pl.BlockSpec((1, LANE), lambda i, c: (0, 0)),
                  pl.BlockSpec((1, vd), lambda i, c: (0, 0))],
        out_specs=(pl.BlockSpec((nb, chunk, vd), lambda i, c: (i, c, 0)),
                   pl.BlockSpec((nb, M_HEADS, M_DK, M_DV), lambda i, c: (i, 0, 0, 0)),
                   pl.BlockSpec((nb, M_HEADS, M_DK), lambda i, c: (i, 0, 0)),
                   pl.BlockSpec((nb, 1, LANE), lambda i, c: (i, 0, 0))),
        scratch_shapes=[pltpu.VMEM((nb, M_HEADS, M_DK, M_DV), F32),
                        pltpu.VMEM((nb, M_HEADS, M_DK), F32),
                        pltpu.VMEM((nb, 1, LANE), F32)],
        compiler_params=_params(2), name="mlstm_scan",
    )(proj, proj, proj, proj, proj, c0, n0, m0p, gate_bias, head_norm.reshape(1, vd))
    return hid, c_t, n_t, m_t[:, 0, :M_HEADS]


N_CAND = (2 + SUBLANE) * SUBLANE
RANK_OUT = float(P_TOPK)


def _top_rows(xs, dsts, k, with_rank):
    xs = list(xs)
    ranks = [jnp.full(x.shape, RANK_OUT, F32) if w else None for x, w in zip(xs, with_rank)]
    for r in range(k):
        for i, dst in enumerate(dsts):
            m = jnp.max(xs[i], axis=1, keepdims=True)
            dst[:, r:r + 1, :] = m
            hit = xs[i] == m
            if ranks[i] is not None:
                ranks[i] = jnp.where(hit, float(r), ranks[i])
            xs[i] = jnp.where(hit, NEG_INF, xs[i])
    return ranks


def _peer_select_body(x_ref, g_ref, wq_ref, k1_ref, k2_ref, r2_ref, e2_ref, l_ref, c_ref,
                      s1_s, s2_s, v1_s, v2_s, cand_s, vals_s):
    qp = _nn(_rms(x_ref[...], g_ref[...]).astype(BF16), wq_ref[...])
    for h in range(P_HEADS):
        q1 = qp[:, h * P_QDIM:h * P_QDIM + P_QDIM // 2].astype(BF16)
        q2 = qp[:, h * P_QDIM + P_QDIM // 2:(h + 1) * P_QDIM].astype(BF16)
        s1_s[h] = _nt(k1_ref[h], q1)
        s2_s[h] = _nt(k2_ref[h], q2)
    s1 = s1_s[...]
    s2 = s2_s[...]
    _, rank2 = _top_rows((s1, s2), (v1_s, v2_s), P_TOPK, (False, True))
    v2_lo = v2_s[:, 0:SUBLANE, :]
    cand_s[:, 0:SUBLANE, :] = v2_lo + v1_s[:, 0:1, :]
    cand_s[:, SUBLANE:2 * SUBLANE, :] = v2_s[:, SUBLANE:2 * SUBLANE, :] + v1_s[:, 0:1, :]
    cand_s[:, 2 * SUBLANE:3 * SUBLANE, :] = v1_s[:, SUBLANE:2 * SUBLANE, :] + v2_s[:, 0:1, :]
    for r in range(1, SUBLANE):
        cand_s[:, (2 + r) * SUBLANE:(3 + r) * SUBLANE, :] = v2_lo + v1_s[:, r:r + 1, :]
    _top_rows((cand_s[...],), (vals_s,), P_TOPK, (False,))
    top = vals_s[:, 0:1, :]
    tau = vals_s[:, P_TOPK - 1:P_TOPK, :]
    z = jnp.sum(jnp.exp(vals_s[...] - top), axis=1, keepdims=True)
    v1 = v1_s[...]
    kept = jnp.zeros(v1.shape, F32)
    for r in range(P_TOPK):
        kept = kept + jnp.where(v2_s[:, r:r + 1, :] + v1 >= tau, 1.0, 0.0)
    n_keep = jnp.zeros(s1.shape, F32)
    for r in range(P_TOPK):
        n_keep = jnp.where(s1 == v1_s[:, r:r + 1, :], kept[:, r:r + 1, :], n_keep)
    r2_ref[...] = rank2.astype(BF16)
    e2_ref[...] = jnp.exp(s2 - v2_s[:, 0:1, :]).astype(BF16)
    l_ref[...] = n_keep
    c_ref[...] = 0.5 * jnp.exp(s1 - v1_s[:, 0:1, :]) / z


def peer_select(x, g, w_q, k1, k2, tm):
    n, d = x.shape
    spec = pl.BlockSpec((P_HEADS, P_NKEYS, tm), lambda i: (0, 0, i))
    key_spec = pl.BlockSpec((P_HEADS, P_NKEYS, P_QDIM // 2), lambda i: (0, 0, 0))
    shp = lambda dt: jax.ShapeDtypeStruct((P_HEADS, P_NKEYS, n), dt)
    return pl.pallas_call(
        _peer_select_body,
        out_shape=(shp(BF16), shp(BF16), shp(F32), shp(F32)),
        grid=(n // tm,),
        in_specs=[pl.BlockSpec((tm, d), lambda i: (i, 0)),
                  pl.BlockSpec((1, d), lambda i: (0, 0)),
                  pl.BlockSpec((d, P_HEADS * P_QDIM), lambda i: (0, 0)),
                  key_spec, key_spec],
        out_specs=(spec, spec, spec, spec),
        scratch_shapes=[pltpu.VMEM((P_HEADS, P_NKEYS, tm), F32), pltpu.VMEM((P_HEADS, P_NKEYS, tm), F32),
                        pltpu.VMEM((P_HEADS, P_TOPK, tm), F32), pltpu.VMEM((P_HEADS, P_TOPK, tm), F32),
                        pltpu.VMEM((P_HEADS, N_CAND, tm), F32), pltpu.VMEM((P_HEADS, P_TOPK, tm), F32)],
        compiler_params=_params(1), name="peer_select",
    )(x, g.reshape(1, d), w_q, k1, k2)


def _peer_main_body(x_ref, g_ref, og_ref, u_ref, vt_ref, r2_ref, e2_ref, l_ref, c_ref,
                    o_ref, hn_s, acc_s, *, a_tile, norm_out):
    t = pl.program_id(1)

    @pl.when(t == 0)
    def _():
        hn_s[...] = _rms(x_ref[...], g_ref[...]).astype(BF16)
        acc_s[...] = jnp.zeros_like(acc_s)

    tm = hn_s.shape[0]
    zero = jnp.zeros((P_NKEYS, tm), BF16)
    a_sub = PEER_SUB_EXPERTS // P_NKEYS
    total = None
    for sub in range(a_tile // a_sub):
        es = slice(sub * PEER_SUB_EXPERTS, (sub + 1) * PEER_SUB_EXPERTS)
        sc = _nt(u_ref[es, :], hn_s[...])
        act = (sc + sc * lax.erf(sc * (2.0 ** -0.5))).astype(BF16)
        parts = []
        for k in range(a_sub):
            aa = sub * a_sub + k
            w = None
            for h in range(P_HEADS):
                keep = jnp.broadcast_to(l_ref[h, aa:aa + 1, :], (P_NKEYS, tm)).astype(BF16)
                coef = jnp.broadcast_to(c_ref[h, aa:aa + 1, :], (P_NKEYS, tm)).astype(BF16)
                term = jnp.where(r2_ref[h] < keep, e2_ref[h], zero) * coef
                w = term if w is None else w + term
            parts.append(w * act[k * P_NKEYS:(k + 1) * P_NKEYS])
        part = _nn(vt_ref[:, es], jnp.concatenate(parts, axis=0))
        total = part if total is None else total + part
    acc_s[...] += total

    @pl.when(t == pl.num_programs(1) - 1)
    def _():
        y = x_ref[...] + acc_s[...].T
        o_ref[...] = _rms(y, og_ref[...]) if norm_out else y


def peer_main(x, g, u, vt, layer, sel, tm, te, out_norm=None):
    n, d = x.shape
    og = g if out_norm is None else out_norm
    r2, e2, n_keep, coef = sel
    a_tile = te // P_NKEYS
    n_tiles = P_EXPERTS // te
    assert a_tile % SUBLANE == 0
    full_spec = pl.BlockSpec((P_HEADS, P_NKEYS, tm), lambda i, t: (0, 0, i))
    row_spec = pl.BlockSpec((P_HEADS, a_tile, tm), lambda i, t: (0, t, i))
    body = functools.partial(_peer_main_body, a_tile=a_tile, norm_out=out_norm is not None)
    return pl.pallas_call(
        body,
        out_shape=jax.ShapeDtypeStruct((n, d), F32),
        grid=(n // tm, n_tiles),
        in_specs=[pl.BlockSpec((tm, d), lambda i, t: (i, 0)),
                  pl.BlockSpec((1, d), lambda i, t: (0, 0)),
                  pl.BlockSpec((1, d), lambda i, t: (0, 0)),
                  pl.BlockSpec((None, te, d), lambda i, t: (layer, t, 0)),
                  pl.BlockSpec((None, d, te), lambda i, t: (layer, 0, t)),
                  full_spec, full_spec, row_spec, row_spec],
        out_specs=pl.BlockSpec((tm, d), lambda i, t: (i, 0)),
        scratch_shapes=[pltpu.VMEM((tm, d), BF16), pltpu.VMEM((d, tm), F32)],
        compiler_params=_params(2), name="peer_main",
    )(x, g.reshape(1, d), og.reshape(1, d), u, vt, r2, e2, n_keep, coef)


def peer_ffn_residual(x, g, w_q, k1, k2, u, vt, layer, tm, te, out_norm=None):
    sel = peer_select(x, g, w_q, k1, k2, tm)
    return peer_main(x, g, u, vt, layer, sel, tm, te, out_norm)


def _topk_mask(score, lane_f, k):
    sel = jnp.zeros(score.shape, F32)
    for _ in range(k):
        m = jnp.max(score, axis=1, keepdims=True)
        idx = jnp.min(jnp.where(score == m, lane_f, 1e9), axis=1, keepdims=True)
        hit = lane_f == idx
        sel = jnp.where(hit, 1.0, sel)
        score = jnp.where(hit, NEG_INF, score)
    return sel


def _block_scores(imp, pos_col, lane):
    cur = lax.shift_right_arithmetic(pos_col, SEL_BLK_LOG2)
    forced = (lane == 0) | (lane == cur) | (lane == cur - 1)
    return jnp.where(forced, jnp.inf, jnp.where(lane <= cur, imp, NEG_INF))


def _stack_heads(q, g):
    return jnp.concatenate(
        [q[:, (g * A_REP + r) * A_DH:(g * A_REP + r + 1) * A_DH] for r in range(A_REP)], axis=0)


def _combine_heads(o_ref, gate, o_cmp, o_sel, o_win, g, rows):
    for r in range(A_REP):
        h = g * A_REP + r
        rs = slice(r * rows, (r + 1) * rows)
        o_ref[:, h * A_DH:(h + 1) * A_DH] = (gate[:, 3 * h:3 * h + 1] * o_cmp[rs]
                                             + gate[:, 3 * h + 1:3 * h + 2] * o_sel[rs]
                                             + gate[:, 3 * h + 2:3 * h + 3] * o_win[rs])


def _online_softmax_step(m_ref, l_ref, acc_ref, g, sc, valid, pv):
    sc = jnp.where(valid, sc, NEG_INF)
    m_old = m_ref[g]
    m_new = jnp.maximum(m_old, jnp.max(sc, axis=1, keepdims=True))
    m_safe = jnp.where(m_new == NEG_INF, 0.0, m_new)
    alpha = jnp.exp(m_old - m_safe)
    e = jnp.exp(sc - m_safe)
    l_ref[g] = alpha * l_ref[g] + jnp.sum(e, axis=1, keepdims=True)
    acc_ref[g] = alpha * acc_ref[g] + pv(e.astype(BF16))
    m_ref[g] = m_new


def _compress_weights(w1, b1, w2):
    eye = jnp.eye(A_KV, dtype=F32)
    w1bd = jnp.einsum("cldh,gk->clgdkh", w1, eye).reshape(2, CMP_LEN, GRP_W, GRP_W).astype(BF16)
    w1cat = jnp.concatenate([w1bd[:, :CMP_STRIDE], w1bd[:, CMP_STRIDE:]], axis=-1)
    w2bd = jnp.einsum("chd,gk->cghkd", w2, eye).reshape(2, GRP_W, GRP_W).astype(BF16)
    b1t = jnp.tile(b1, (1, A_KV)).reshape(2, 1, GRP_W)
    return w1cat, b1t, w2bd


def _split3(x):
    hi = x.astype(BF16)
    r1 = x - hi.astype(F32)
    mid = r1.astype(BF16)
    lo = (r1 - mid.astype(F32)).astype(BF16)
    return hi, mid, lo


def _block_importance(p_sum, ov_ref):
    hi, mid, lo = _split3(p_sum)
    ov = ov_ref[...]
    return _nn(hi, ov) + _nn(mid, ov) + _nn(lo, ov)


def _overlap_matrix(n_rows, n_cols, shift):
    c_start = (np.arange(n_rows) - shift) * CMP_STRIDE
    s_start = np.arange(n_cols) * SEL_BLK
    ov = np.clip(np.minimum(c_start[:, None] + CMP_LEN, s_start[None, :] + SEL_BLK)
                 - np.maximum(c_start[:, None], s_start[None, :]), 0, None) / CMP_STRIDE
    ov[c_start < 0] = 0.0
    return jnp.asarray(ov, BF16)


R16_PER_PAGE = PAGE_SIZE // CMP_STRIDE


def _offset_grouping_perm():
    out_row = _iota((PAGE_SIZE, PAGE_SIZE), 0)
    src_tok = (CMP_STRIDE * (out_row & (R16_PER_PAGE - 1))
               + lax.shift_right_logical(out_row, R16_PER_PAGE.bit_length() - 1))
    return jnp.where(_iota((PAGE_SIZE, PAGE_SIZE), 1) == src_tok, 1.0, 0.0).astype(BF16)


def _store_by_offset(x_s, group, x_t):
    for l in range(CMP_STRIDE):
        x_s[l, group * R16_PER_PAGE:(group + 1) * R16_PER_PAGE, :] = (
            x_t[l * R16_PER_PAGE:(l + 1) * R16_PER_PAGE, :])


def _cmp_prompt_body(r_ref, w1_ref, b1_ref, w2_ref, ck_ref, cv_ref, x_s):
    perm = _offset_grouping_perm()
    for k in range(r_ref.shape[0] // PAGE_SIZE):
        _store_by_offset(x_s, k, _nn(perm, r_ref[k * PAGE_SIZE:(k + 1) * PAGE_SIZE, :].astype(BF16)))
    for c, out_ref in ((0, ck_ref), (1, cv_ref)):
        both = None
        for l in range(CMP_STRIDE):
            y = _nn(x_s[l, :, c * GRP_W:(c + 1) * GRP_W].astype(BF16), w1_ref[c, l])
            both = y if both is None else both + y
        first, second = both[:, :GRP_W], both[:, GRP_W:]
        n = first.shape[0]
        hid = _gelu(first + pltpu.roll(second, n - 1, 0) + b1_ref[c])
        out_ref[0] = _nn(hid.astype(BF16), w2_ref[c])


def compress_prompt(rows, bsz, w1bd, b1t, w2bd):
    t = rows.shape[0] // bsz
    n = t // CMP_STRIDE
    assert t % PAGE_SIZE == 0
    out = jax.ShapeDtypeStruct((bsz, n, GRP_W), F32)
    return pl.pallas_call(
        _cmp_prompt_body,
        out_shape=(out, out),
        grid=(bsz,),
        in_specs=[pl.BlockSpec((t, ROW_W), lambda i: (i, 0)),
                  pl.BlockSpec(w1bd.shape, lambda i: (0, 0, 0, 0)),
                  pl.BlockSpec(b1t.shape, lambda i: (0, 0, 0)),
                  pl.BlockSpec(w2bd.shape, lambda i: (0, 0, 0))],
        out_specs=(pl.BlockSpec((1, n, GRP_W), lambda i: (i, 0, 0)),
                   pl.BlockSpec((1, n, GRP_W), lambda i: (i, 0, 0))),
        scratch_shapes=[pltpu.VMEM((CMP_STRIDE, n, ROW_W), F32)],
        compiler_params=_params(1), name="compress_prompt",
    )(rows, w1bd, b1t, w2bd)


NSA_KEY_CHUNK = 512


def _nsa_prompt_body(q_ref, gate_ref, ck_ref, cv_ref, sel_ref, win_ref, ov_ref, ex_ref, bg_ref,
                     o_ref, mask_s, m_s, l_s, acc_s, ocmp_s, *, tq, t_len):
    i = pl.program_id(1)
    q0 = i * tq
    q = q_ref[...] * (ATTN_SCALE * LOG2E)
    pos = q0 + _iota((tq, 1), 0)
    pos4 = jnp.concatenate([pos] * A_REP, axis=0)
    n_cmp_rows = ck_ref.shape[1]
    n_lane = _iota((1, n_cmp_rows), 1)
    valid_cmp = (n_lane * CMP_STRIDE + (CMP_LEN - 1) <= pos) & (n_lane < n_cmp_rows - 1)
    bias_cmp = jnp.where(valid_cmp, 0.0, NEG_INF)
    qgs = [_stack_heads(q, g).astype(BF16) for g in range(A_KV)]

    def add_bias(s, bias):
        return jnp.concatenate([s[r * tq:(r + 1) * tq] + bias for r in range(A_REP)], axis=0)

    imps = []
    for g in range(A_KV):
        gs = slice(g * A_DH, (g + 1) * A_DH)
        s_c = add_bias(_nt(qgs[g], ck_ref[0, :, gs].astype(BF16)), bias_cmp)
        m_c = jnp.max(s_c, axis=1, keepdims=True)
        e_c = jnp.exp2(s_c - jnp.where(m_c == NEG_INF, 0.0, m_c))
        p_cmp = e_c / jnp.maximum(jnp.sum(e_c, axis=1, keepdims=True), 1e-30)
        ocmp_s[g] = _nn(p_cmp.astype(BF16), cv_ref[0, :, gs].astype(BF16))
        p_sum = p_cmp[0:tq]
        for r in range(1, A_REP):
            p_sum = p_sum + p_cmp[r * tq:(r + 1) * tq]
        imps.append(_block_importance(p_sum, ov_ref))
    blk_lane = _iota((A_KV * tq, LANE), 1)
    score = _block_scores(jnp.concatenate(imps, axis=0), pos4, blk_lane)
    few_blocks = q0 + tq <= N_SEL * SEL_BLK

    @pl.when(few_blocks)
    def _():
        mask_s[...] = jnp.where(score == NEG_INF, 0.0, 1.0)

    @pl.when(jnp.logical_not(few_blocks))
    def _():
        mask_s[...] = _topk_mask(score, blk_lane.astype(F32), N_SEL)

    m_s[...] = jnp.full_like(m_s, NEG_INF)
    l_s[...] = jnp.zeros_like(l_s)
    acc_s[...] = jnp.zeros_like(acc_s)
    kc = NSA_KEY_CHUNK
    for c in range(t_len // kc):
        @pl.when(c * kc < q0 + tq)
        def _(c=c):
            key_lane = c * kc + _iota((1, kc), 1)
            causal_bias = jnp.where(key_lane <= pos, 0.0, MASKED)
            for g in range(A_KV):
                blk_bias = _nn((mask_s[g * tq:(g + 1) * tq, :] - 1.0).astype(BF16),
                               ex_ref[:, c * kc:(c + 1) * kc])
                k = sel_ref[c * kc:(c + 1) * kc, g * A_DH:(g + 1) * A_DH].astype(BF16)
                v = sel_ref[c * kc:(c + 1) * kc,
                            GRP_W + g * A_DH:GRP_W + (g + 1) * A_DH].astype(BF16)
                sc = add_bias(_nt(qgs[g], k), blk_bias + causal_bias)
                m_old = m_s[g]
                m_new = jnp.maximum(m_old, jnp.max(sc, axis=1, keepdims=True))
                alpha = jnp.exp2(m_old - m_new)
                e = jnp.exp2(sc - m_new)
                l_s[g] = alpha * l_s[g] + jnp.sum(e, axis=1, keepdims=True)
                acc_s[g] = alpha * acc_s[g] + _nn(e.astype(BF16), v)
                m_s[g] = m_new

    gate = jax.nn.sigmoid(gate_ref[...] + bg_ref[...])
    band = WINDOW + tq
    w_start = pl.multiple_of(jnp.maximum(q0 - WINDOW, 0), tq)
    kpos_w = w_start + _iota((1, band), 1)
    diff_w = pos - kpos_w
    bias_w = jnp.where((diff_w >= 0) & (diff_w <= WINDOW), 0.0, MASKED)
    for g in range(A_KV):
        gs = slice(g * A_DH, (g + 1) * A_DH)
        vs_ = slice(GRP_W + g * A_DH, GRP_W + (g + 1) * A_DH)
        kw = win_ref[pl.ds(w_start, band), gs].astype(BF16)
        vw = win_ref[pl.ds(w_start, band), vs_].astype(BF16)
        s_w = add_bias(_nt(qgs[g], kw), bias_w)
        e_w = jnp.exp2(s_w - jnp.max(s_w, axis=1, keepdims=True))
        o_win = _nn(e_w.astype(BF16), vw) / jnp.sum(e_w, axis=1, keepdims=True)
        o_sel = acc_s[g] / jnp.maximum(l_s[g], 1e-30)
        _combine_heads(o_ref, gate, ocmp_s[g], o_sel, o_win, g, tq)


def nsa_prompt_attention(q, gate_pre, ck, cv, sel_rows, win_rows, b_gate_row, bsz, t_len, tq):
    n = q.shape[0]
    n_q = t_len // tq
    n_blk = t_len // SEL_BLK
    assert t_len % NSA_KEY_CHUNK == 0 and n_blk <= LANE
    ov = jnp.pad(_overlap_matrix(ck.shape[1], n_blk, 0), ((0, 0), (0, LANE - n_blk)))
    ex = (np.arange(LANE)[:, None] == (np.arange(t_len)[None, :] // SEL_BLK))
    ex = jnp.asarray(ex * -MASKED, BF16)
    rows4 = A_REP * tq
    body = functools.partial(_nsa_prompt_body, tq=tq, t_len=t_len)
    return pl.pallas_call(
        body,
        out_shape=jax.ShapeDtypeStruct((n, A_QW), F32),
        grid=(bsz, n_q),
        in_specs=[pl.BlockSpec((tq, A_QW), lambda b, i: (b * n_q + i, 0)),
                  pl.BlockSpec((tq, LANE), lambda b, i: (b * n_q + i, 0)),
                  pl.BlockSpec((1,) + ck.shape[1:], lambda b, i: (b, 0, 0)),
                  pl.BlockSpec((1,) + cv.shape[1:], lambda b, i: (b, 0, 0)),
                  pl.BlockSpec((t_len, ROW_W), lambda b, i: (b, 0)),
                  pl.BlockSpec((t_len, ROW_W), lambda b, i: (b, 0)),
                  pl.BlockSpec(ov.shape, lambda b, i: (0, 0)),
                  pl.BlockSpec(ex.shape, lambda b, i: (0, 0)),
                  pl.BlockSpec((1, LANE), lambda b, i: (0, 0))],
        out_specs=pl.BlockSpec((tq, A_QW), lambda b, i: (b * n_q + i, 0)),
        scratch_shapes=[pltpu.VMEM((A_KV * tq, LANE), F32),
                        pltpu.VMEM((A_KV, rows4, 1), F32),
                        pltpu.VMEM((A_KV, rows4, 1), F32),
                        pltpu.VMEM((A_KV, rows4, A_DH), F32),
                        pltpu.VMEM((A_KV, rows4, A_DH), F32)],
        compiler_params=_params(2), name="nsa_prompt_attention",
    )(q, gate_pre, ck, cv, sel_rows, win_rows, ov, ex, b_gate_row)


DEC_ROWS = 8
CMP_PAGES_PER_STEP = 32
SEL_PAGES_PER_STEP = 32


def _dec_cmp_body(pt_ref, *refs, past_len, n_blk_pad):
    del pt_ref
    pages = refs[:CMP_PAGES_PER_STEP]
    (w1_ref, b1_ref, w2_ref, q_ref, ov_ref, ocmp_ref, msel_ref,
     ck_s, cv_s, carry_s, x_s) = refs[CMP_PAGES_PER_STEP:]
    s = pl.program_id(1)
    rows = CMP_PAGES_PER_STEP * R16_PER_PAGE

    @pl.when(s == 0)
    def _():
        carry_s[...] = jnp.zeros_like(carry_s)

    perm = _offset_grouping_perm()
    for k, p in enumerate(pages):
        _store_by_offset(x_s, k, _nt(perm, p[0].astype(BF16)))

    rid = _iota((rows, 1), 0)
    for c, dst in ((0, ck_s), (1, cv_s)):
        both = None
        for l in range(CMP_STRIDE):
            x = x_s[l, :, c * GRP_W:(c + 1) * GRP_W].astype(BF16)
            y = _nn(x, w1_ref[c, l])
            both = y if both is None else both + y
        first, second = both[:, :GRP_W], both[:, GRP_W:]
        prev = carry_s[c, SUBLANE - 1:SUBLANE, :]
        shifted = jnp.where(rid == 0, prev, pltpu.roll(first, 1, 0))
        hid = _gelu(shifted + second + b1_ref[c])
        dst[pl.ds(pl.multiple_of(s * rows, rows), rows), :] = _nn(hid.astype(BF16), w2_ref[c])
        carry_s[c] = first[rows - SUBLANE:rows, :]

    @pl.when(s == pl.num_programs(1) - 1)
    def _():
        n_rows = ck_s.shape[0]
        q = q_ref[0]
        t_col = _iota((DEC_ROWS, 1), 0)
        pos = past_len + t_col
        pos4 = jnp.concatenate([pos] * A_REP, axis=0)
        m_lane = _iota((1, n_rows), 1)
        valid = (m_lane >= 1) & ((m_lane - 1) * CMP_STRIDE + (CMP_LEN - 1) <= pos4)
        imps = []
        for g in range(A_KV):
            gs = slice(g * A_DH, (g + 1) * A_DH)
            qg = _stack_heads(q, g).astype(BF16)
            p = _masked_softmax(_nt(qg, ck_s[:, gs].astype(BF16)) * ATTN_SCALE, valid)
            o = _nn(p.astype(BF16), cv_s[:, gs].astype(BF16))
            for r in range(A_REP):
                h = g * A_REP + r
                ocmp_ref[0, :, h * A_DH:(h + 1) * A_DH] = o[r * DEC_ROWS:(r + 1) * DEC_ROWS]
            p_sum = p[0:DEC_ROWS]
            for r in range(1, A_REP):
                p_sum = p_sum + p[r * DEC_ROWS:(r + 1) * DEC_ROWS]
            imps.append(_block_importance(p_sum, ov_ref))
        imp = jnp.concatenate(imps, axis=0)
        lane = _iota((A_KV * DEC_ROWS, n_blk_pad), 1)
        score = _block_scores(imp, pos4, lane)
        mask = _topk_mask(score, lane.astype(F32), N_SEL)
        for j in range(n_blk_pad // LANE):
            msel_ref[0, j] = mask[:, j * LANE:(j + 1) * LANE]


def nsa_decode_compress(cache_t, page_table, w1bd, b1t, w2bd, q8, past_len, n_blk_pad):
    dbsz, n_pages = page_table.shape
    assert n_pages % CMP_PAGES_PER_STEP == 0
    n_steps = n_pages // CMP_PAGES_PER_STEP
    n_rows = n_pages * R16_PER_PAGE
    n_tiles = n_blk_pad // LANE
    ov = _overlap_matrix(n_rows, n_blk_pad, 1)

    def page_spec(k):
        return pl.BlockSpec((1, ROW_W, PAGE_SIZE),
                            lambda b, s, pt: (pt[b, s * CMP_PAGES_PER_STEP + k], 0, 0))

    grid_spec = pltpu.PrefetchScalarGridSpec(
        num_scalar_prefetch=1,
        grid=(dbsz, n_steps),
        in_specs=[page_spec(k) for k in range(CMP_PAGES_PER_STEP)] + [
            pl.BlockSpec(w1bd.shape, lambda b, s, pt: (0, 0, 0, 0)),
            pl.BlockSpec(b1t.shape, lambda b, s, pt: (0, 0, 0)),
            pl.BlockSpec(w2bd.shape, lambda b, s, pt: (0, 0, 0)),
            pl.BlockSpec((1, DEC_ROWS, A_QW), lambda b, s, pt: (b, 0, 0)),
            pl.BlockSpec(ov.shape, lambda b, s, pt: (0, 0))],
        out_specs=(pl.BlockSpec((1, DEC_ROWS, A_QW), lambda b, s, pt: (b, 0, 0)),
                   pl.BlockSpec((1, n_tiles, A_KV * DEC_ROWS, LANE), lambda b, s, pt: (b, 0, 0, 0))),
        scratch_shapes=[pltpu.VMEM((n_rows, GRP_W), F32), pltpu.VMEM((n_rows, GRP_W), F32),
                        pltpu.VMEM((2, SUBLANE, GRP_W), F32),
                        pltpu.VMEM((CMP_STRIDE, CMP_PAGES_PER_STEP * R16_PER_PAGE, ROW_W), F32)])
    body = functools.partial(_dec_cmp_body, past_len=past_len, n_blk_pad=n_blk_pad)
    return pl.pallas_call(
        body,
        out_shape=(jax.ShapeDtypeStruct((dbsz, DEC_ROWS, A_QW), F32),
                   jax.ShapeDtypeStruct((dbsz, n_tiles, A_KV * DEC_ROWS, LANE), F32)),
        grid_spec=grid_spec,
        compiler_params=_params(2), name="nsa_decode_compress",
    )(page_table, *([cache_t] * CMP_PAGES_PER_STEP), w1bd, b1t, w2bd, q8, ov)


def _dec_sel_body(pt_ref, *refs, past_len, t_new):
    del pt_ref
    pages = refs[:SEL_PAGES_PER_STEP]
    (q_ref, msel_ref, ocmp_ref, snew_ref, wpast_ref, wnew_ref, gate_ref, bg_ref,
     o_ref, m_s, l_s, acc_s) = refs[SEL_PAGES_PER_STEP:]
    s = pl.program_id(1)
    keys = SEL_PAGES_PER_STEP * PAGE_SIZE
    blks = keys // SEL_BLK
    steps_per_tile = LANE // blks

    @pl.when(s == 0)
    def _():
        m_s[...] = jnp.full_like(m_s, NEG_INF)
        l_s[...] = jnp.zeros_like(l_s)
        acc_s[...] = jnp.zeros_like(acc_s)

    q = q_ref[0]
    tile = s // steps_per_tile
    first_blk = (s - tile * steps_per_tile) * blks
    expand = (_iota((LANE, keys), 0)
              == first_blk + lax.shift_right_arithmetic(_iota((LANE, keys), 1), SEL_BLK_LOG2))
    expand = jnp.where(expand, 1.0, 0.0).astype(BF16)
    m_keys = _nn(msel_ref[0, tile].astype(BF16), expand)

    for g in range(A_KV):
        gs = slice(g * A_DH, (g + 1) * A_DH)
        vs_ = slice(GRP_W + g * A_DH, GRP_W + (g + 1) * A_DH)
        qg = _stack_heads(q, g).astype(BF16)
        k_t = jnp.concatenate([p[0, gs, :] for p in pages], axis=1).astype(BF16)
        v_t = jnp.concatenate([p[0, vs_, :] for p in pages], axis=1).astype(BF16)
        valid = jnp.concatenate([m_keys[g * DEC_ROWS:(g + 1) * DEC_ROWS]] * A_REP, axis=0) > 0.5
        _online_softmax_step(m_s, l_s, acc_s, g, _nn(qg, k_t) * ATTN_SCALE, valid,
                             lambda e, v_t=v_t: _nt(e, v_t))

    @pl.when(s == pl.num_programs(1) - 1)
    def _():
        t_col = _iota((DEC_ROWS, 1), 0)
        t4 = jnp.concatenate([t_col] * A_REP, axis=0)
        t_key = _iota((1, DEC_ROWS), 1)
        valid_new = (t_key <= t4) & (t_key < t_new)
        gate = jax.nn.sigmoid(gate_ref[0] + bg_ref[...])
        n_win = wpast_ref.shape[2]
        kpos_w = past_len - n_win + _iota((1, n_win), 1)
        diff_w = (past_len + t4) - kpos_w
        valid_wp = (diff_w >= 0) & (diff_w <= WINDOW)
        new_blk = past_len // SEL_BLK
        new_tile, new_lane = new_blk // LANE, new_blk % LANE
        for g in range(A_KV):
            gs = slice(g * A_DH, (g + 1) * A_DH)
            vs_ = slice(GRP_W + g * A_DH, GRP_W + (g + 1) * A_DH)
            qg = _stack_heads(q, g).astype(BF16)
            mnew = msel_ref[0, new_tile, g * DEC_ROWS:(g + 1) * DEC_ROWS, new_lane:new_lane + 1]
            mnew4 = jnp.concatenate([mnew] * A_REP, axis=0)
            v_new = snew_ref[0, :, vs_].astype(BF16)
            _online_softmax_step(m_s, l_s, acc_s, g,
                                 _nt(qg, snew_ref[0, :, gs].astype(BF16)) * ATTN_SCALE,
                                 valid_new & (mnew4 > 0.5), lambda e, v_new=v_new: _nn(e, v_new))
            o_sel = acc_s[g] / jnp.maximum(l_s[g], 1e-30)

            s_p = jnp.where(valid_wp, _nn(qg, wpast_ref[0, gs, :].astype(BF16)) * ATTN_SCALE, NEG_INF)
            s_n = jnp.where(valid_new, _nt(qg, wnew_ref[0, :, gs].astype(BF16)) * ATTN_SCALE, NEG_INF)
            mx = jnp.maximum(jnp.max(s_p, axis=1, keepdims=True), jnp.max(s_n, axis=1, keepdims=True))
            mx = jnp.where(mx == NEG_INF, 0.0, mx)
            e_p = jnp.exp(s_p - mx)
            e_n = jnp.exp(s_n - mx)
            den = jnp.maximum(jnp.sum(e_p, axis=1, keepdims=True)
                              + jnp.sum(e_n, axis=1, keepdims=True), 1e-30)
            o_win = (_nt(e_p.astype(BF16), wpast_ref[0, vs_, :].astype(BF16))
                     + _nn(e_n.astype(BF16), wnew_ref[0, :, vs_].astype(BF16))) / den
            o_cmp = _stack_heads(ocmp_ref[0], g)
            _combine_heads(o_ref.at[0], gate, o_cmp, o_sel, o_win, g, DEC_ROWS)


def nsa_decode_attention(cache_t, page_table, q8, msel, ocmp, sel_new, win_past_t, win_new,
                         gate8, b_gate_row, past_len, t_new):
    dbsz, n_pages = page_table.shape
    keys = SEL_PAGES_PER_STEP * PAGE_SIZE
    assert n_pages % SEL_PAGES_PER_STEP == 0 and past_len % SEL_BLK == 0
    assert LANE % (keys // SEL_BLK) == 0
    n_steps = n_pages // SEL_PAGES_PER_STEP
    n_win = win_past_t.shape[2]

    def page_spec(k):
        return pl.BlockSpec((1, ROW_W, PAGE_SIZE),
                            lambda b, s, pt: (pt[b, s * SEL_PAGES_PER_STEP + k], 0, 0))

    per_b = lambda shp: pl.BlockSpec((1,) + shp, lambda b, s, pt: (b,) + (0,) * len(shp))
    grid_spec = pltpu.PrefetchScalarGridSpec(
        num_scalar_prefetch=1,
        grid=(dbsz, n_steps),
        in_specs=[page_spec(k) for k in range(SEL_PAGES_PER_STEP)] + [
            per_b((DEC_ROWS, A_QW)), per_b(msel.shape[1:]), per_b((DEC_ROWS, A_QW)),
            per_b((DEC_ROWS, ROW_W)), per_b((ROW_W, n_win)), per_b((DEC_ROWS, ROW_W)),
            per_b((DEC_ROWS, LANE)), pl.BlockSpec((1, LANE), lambda b, s, pt: (0, 0))],
        out_specs=per_b((DEC_ROWS, A_QW)),
        scratch_shapes=[pltpu.VMEM((A_KV, A_REP * DEC_ROWS, 1), F32),
                        pltpu.VMEM((A_KV, A_REP * DEC_ROWS, 1), F32),
                        pltpu.VMEM((A_KV, A_REP * DEC_ROWS, A_DH), F32)])
    body = functools.partial(_dec_sel_body, past_len=past_len, t_new=t_new)
    return pl.pallas_call(
        body,
        out_shape=jax.ShapeDtypeStruct((dbsz, DEC_ROWS, A_QW), F32),
        grid_spec=grid_spec,
        compiler_params=_params(2), name="nsa_decode_attention",
    )(page_table, *([cache_t] * SEL_PAGES_PER_STEP), q8, msel, ocmp, sel_new, win_past_t,
      win_new, gate8, b_gate_row)


def tokens_minor(rows):
    n, t = rows.shape[:2]
    return jnp.transpose(rows, (0, 2, 3, 4, 1)).reshape(n, ROW_W, t)


PROMPT_TM = 256
PEER_TM = 512
PEER_TE = 2048
PEER_SUB_EXPERTS = 1024
NSA_TQ = 128


def kernel(x_prompt, x_sample, cache_cmp_kv, cache_sel_kv, state_win_kv, state_C, state_n, state_m,
           page_table, norm_mix, norm_ffn, norm_final, mlstm_w_in, mlstm_b_i, mlstm_b_f,
           mlstm_head_norm, mlstm_w_out, nsa_w_in, nsa_b_gate, nsa_cmp_w1, nsa_cmp_b1, nsa_cmp_w2,
           nsa_w_out, peer_w_q, peer_sub_keys, peer_u, peer_v):
    bsz, t_len, d = x_prompt.shape
    dbsz, t_s, _ = x_sample.shape
    n_pages = page_table.shape[1]
    past_len = n_pages * PAGE_SIZE
    assert norm_mix.shape[0] == 2 and d == D_MODEL and t_s <= DEC_ROWS // 2
    assert t_len % M_CHUNK == 0 and t_len % NSA_TQ == 0 and t_len >= WINDOW

    xp = x_prompt.reshape(bsz * t_len, d)
    reps = DEC_ROWS // t_s
    xs = jnp.concatenate([x_sample] * reps, axis=1).reshape(dbsz * DEC_ROWS, d)
    n_s = dbsz * DEC_ROWS

    u_all = peer_u.astype(BF16)
    vt_all = jnp.swapaxes(peer_v, 1, 2).astype(BF16)

    def peer(x, layer, tm, out_norm=None):
        keys = peer_sub_keys[layer].astype(BF16)
        return peer_ffn_residual(x, norm_ffn[layer], peer_w_q[layer].astype(BF16),
                                 keys[:, 0], keys[:, 1], u_all, vt_all, layer, tm, PEER_TE, out_norm)

    w_in = jnp.pad(mlstm_w_in[0], ((0, 0), (0, M_PROJ_PAD - M_PROJ))).astype(BF16)
    w_out = mlstm_w_out[0].astype(BF16)
    gate_bias = jnp.pad(jnp.concatenate([mlstm_b_i[0], mlstm_b_f[0]]),
                        (0, LANE - 2 * M_HEADS)).reshape(1, LANE)
    proj_p = rms_matmul(xp, norm_mix[0], w_in, PROMPT_TM).reshape(bsz, t_len, M_PROJ_PAD)
    proj_s = rms_matmul(xs, norm_mix[0], w_in, n_s).reshape(dbsz, DEC_ROWS, M_PROJ_PAD)
    hid_p, c_p, n_p, m_p = mlstm_scan(
        proj_p, jnp.zeros((bsz, M_HEADS, M_DK, M_DV), F32), jnp.zeros((bsz, M_HEADS, M_DK), F32),
        jnp.zeros((bsz, M_HEADS), F32), gate_bias, mlstm_head_norm[0], M_CHUNK, M_CHUNK - 1)
    hid_s, c_s, n_st, m_st = mlstm_scan(
        proj_s, state_C[0], state_n[0], state_m[0], gate_bias, mlstm_head_norm[0],
        DEC_ROWS, t_s - 1)
    xp = matmul_residual(hid_p.reshape(bsz * t_len, d), w_out, xp, PROMPT_TM)
    xs = matmul_residual(hid_s.reshape(n_s, d), w_out, xs, n_s)
    xp = peer(xp, 0, PEER_TM)
    xs = peer(xs, 0, n_s)

    w_in_a = jnp.pad(nsa_w_in[0], ((0, 0), (0, A_PROJ_PAD - A_PROJ))).astype(BF16)
    w_out_a = nsa_w_out[0].astype(BF16)
    b_gate_row = jnp.pad(nsa_b_gate[0].reshape(-1), (0, LANE - 3 * A_HEADS)).reshape(1, LANE)
    w1bd, b1t, w2bd = _compress_weights(nsa_cmp_w1[0], nsa_cmp_b1[0], nsa_cmp_w2[0])
    tabs_p = rope_tables(jnp.arange(t_len, dtype=jnp.int32))
    pos_s = past_len + jnp.arange(DEC_ROWS, dtype=jnp.int32)
    tabs_s = tuple(jnp.tile(a, (dbsz, 1)) for a in rope_tables(pos_s))

    q_p, cmp_p, sel_p, win_p, gate_p, cmp_pt, sel_pt = nsa_project(
        xp, norm_mix[1], w_in_a, tabs_p, PROMPT_TM, tokens_minor_seqs=bsz)
    q_s, cmp_s, sel_s, win_s, gate_s = nsa_project(xs, norm_mix[1], w_in_a, tabs_s, n_s)

    ck, cv = compress_prompt(cmp_p, bsz, w1bd, b1t, w2bd)
    o_p = nsa_prompt_attention(q_p, gate_p, ck, cv, sel_p, win_p, b_gate_row, bsz, t_len, NSA_TQ)
    xp = matmul_residual(o_p, w_out_a, xp, PROMPT_TM)

    n_blk = -(-(past_len + t_s) // SEL_BLK)
    n_blk_pad = -(-n_blk // LANE) * LANE
    q8 = q_s.reshape(dbsz, DEC_ROWS, A_QW)
    o_cmp, msel = nsa_decode_compress(tokens_minor(cache_cmp_kv[0]), page_table, w1bd, b1t, w2bd,
                                      q8, past_len, n_blk_pad)
    o_s = nsa_decode_attention(
        tokens_minor(cache_sel_kv[0]), page_table, q8, msel, o_cmp,
        sel_s.reshape(dbsz, DEC_ROWS, ROW_W), tokens_minor(state_win_kv[0]),
        win_s.reshape(dbsz, DEC_ROWS, ROW_W), gate_s.reshape(dbsz, DEC_ROWS, LANE), b_gate_row,
        past_len, t_s)
    xs = matmul_residual(o_s.reshape(n_s, d), w_out_a, xs, n_s)
    y_p = peer(xp, 1, PEER_TM, norm_final).reshape(bsz, t_len, d)
    y_s = peer(xs, 1, n_s, norm_final).reshape(dbsz, DEC_ROWS, d)[:, :t_s]

    row_shape = (2, A_KV, A_DH)
    kv_p = lambda a_t: jnp.transpose(a_t.reshape((bsz,) + row_shape + (t_len,)), (0, 4, 1, 2, 3))[None]
    kv_s = lambda a: a.reshape((dbsz, DEC_ROWS) + row_shape)[None, :, :t_s]
    win_rows = min(WINDOW, t_len)
    win_buf_p = win_p.reshape((bsz, t_len) + row_shape)[None, :, t_len - win_rows:]
    win_new_s = win_s.reshape((dbsz, DEC_ROWS) + row_shape)[:, :t_s]
    win_buf_s = jnp.concatenate([state_win_kv[0], win_new_s], axis=1)[None, :, t_s:]
    return (y_p, y_s,
            kv_p(cmp_pt), kv_p(sel_pt), win_buf_p,
            c_p[None], n_p[None], m_p[None],
            kv_s(cmp_s), kv_s(sel_s), win_buf_s,
            c_s[None], n_st[None], m_st[None])
```

```python
import functools

import numpy as np
import jax
import jax.numpy as jnp
from jax import lax
from jax.experimental import pallas as pl
from jax.experimental.pallas import tpu as pltpu

F32 = jnp.float32
BF16 = jnp.bfloat16
HI = lax.Precision.HIGHEST

V7X_VMEM_BYTES = 64 * 1024 * 1024
VMEM_LIMIT = V7X_VMEM_BYTES - 8 * 1024 * 1024
LANE = 128
SUBLANE = 8

D_MODEL = 1024
NORM_EPS = 1e-6

M_HEADS = 8
M_DK = 64
M_DV = 128
M_CHUNK = 64
M_PROJ = 2 * M_HEADS * M_DK + 2 * M_HEADS * M_DV + 2 * M_HEADS
M_PROJ_PAD = 3200

A_HEADS = 16
A_KV = 4
A_REP = 4
A_DH = 64
ROT_DIM = 16
ROPE_THETA = 500000.0
CMP_STRIDE = 16
CMP_LEN = 32
SEL_BLK = 64
SEL_BLK_LOG2 = 6
N_SEL = 16
WINDOW = 512
A_QW = 1024
A_KVW = 1536
A_PROJ = A_QW + A_KVW + 3 * A_HEADS
A_PROJ_PAD = 2688
ATTN_SCALE = A_DH ** -0.5
GRP_W = A_KV * A_DH
ROW_W = 2 * GRP_W
PAGE_SIZE = 128

P_HEADS = 8
P_NKEYS = 128
P_EXPERTS = P_NKEYS * P_NKEYS
P_QDIM = 256
P_TOPK = 16

NEG_INF = float("-inf")
LOG2E = 1.4426950408889634
MASKED = -(2.0 ** 100)


def _params(n_axes):
    return pltpu.CompilerParams(dimension_semantics=("arbitrary",) * n_axes,
                                vmem_limit_bytes=VMEM_LIMIT)


def _nn(a, b, precision=None):
    return jnp.dot(a, b, preferred_element_type=F32, precision=precision)


def _nt(a, b, precision=None):
    return lax.dot_general(a, b, (((1,), (1,)), ((), ())), preferred_element_type=F32,
                           precision=precision)


def _tn(a, b, precision=None):
    return lax.dot_general(a, b, (((0,), (0,)), ((), ())), preferred_element_type=F32,
                           precision=precision)


def _gelu(x):
    return 0.5 * x * (1.0 + lax.erf(x * (2.0 ** -0.5)))


def _rms(x, g):
    return x * lax.rsqrt(jnp.mean(x * x, axis=-1, keepdims=True) + NORM_EPS) * g


def _masked_softmax(s, valid):
    s = jnp.where(valid, s, NEG_INF)
    m = jnp.max(s, axis=-1, keepdims=True)
    m = jnp.where(m == NEG_INF, 0.0, m)
    e = jnp.exp(s - m)
    return e / jnp.maximum(jnp.sum(e, axis=-1, keepdims=True), 1e-30)


def _iota(shape, dim):
    return lax.broadcasted_iota(jnp.int32, shape, dim)


def _rms_mm_body(x_ref, g_ref, w_ref, o_ref):
    o_ref[...] = _nn(_rms(x_ref[...], g_ref[...]).astype(BF16), w_ref[...])


def rms_matmul(x, g, w, tm):
    n, d = x.shape
    nout = w.shape[1]
    return pl.pallas_call(
        _rms_mm_body,
        out_shape=jax.ShapeDtypeStruct((n, nout), F32),
        grid=(n // tm,),
        in_specs=[pl.BlockSpec((tm, d), lambda i: (i, 0)),
                  pl.BlockSpec((1, d), lambda i: (0, 0)),
                  pl.BlockSpec((d, nout), lambda i: (0, 0))],
        out_specs=pl.BlockSpec((tm, nout), lambda i: (i, 0)),
        compiler_params=_params(1), name="rms_matmul",
    )(x, g.reshape(1, d), w)


def _mm_res_body(a_ref, w_ref, x_ref, o_ref):
    o_ref[...] = x_ref[...] + _nn(a_ref[...].astype(BF16), w_ref[...])


def matmul_residual(a, w, x, tm):
    n, k = a.shape
    d = w.shape[1]
    return pl.pallas_call(
        _mm_res_body,
        out_shape=jax.ShapeDtypeStruct((n, d), F32),
        grid=(n // tm,),
        in_specs=[pl.BlockSpec((tm, k), lambda i: (i, 0)),
                  pl.BlockSpec((k, d), lambda i: (0, 0)),
                  pl.BlockSpec((tm, d), lambda i: (i, 0))],
        out_specs=pl.BlockSpec((tm, d), lambda i: (i, 0)),
        compiler_params=_params(1), name="matmul_residual",
    )(a, w, x)


def _nsa_proj_body(x_ref, g_ref, w_ref, c_ref, sa_ref, sb_ref,
                   q_ref, cmp_ref, sel_ref, win_ref, gate_ref, *t_refs):
    y = _nn(_rms(x_ref[...], g_ref[...]).astype(BF16), w_ref[...])
    c = c_ref[...]
    sa = sa_ref[...]
    sb = sb_ref[...]

    def rot(z):
        return (z * c + pltpu.roll(z, GRP_W - ROT_DIM // 2, 1) * sa
                + pltpu.roll(z, ROT_DIM // 2, 1) * sb)

    for j in range(A_QW // GRP_W):
        q_ref[:, j * GRP_W:(j + 1) * GRP_W] = rot(y[:, j * GRP_W:(j + 1) * GRP_W])
    for ref, base in ((cmp_ref, A_QW), (sel_ref, A_QW + ROW_W), (win_ref, A_QW + 2 * ROW_W)):
        ref[:, 0:GRP_W] = rot(y[:, base:base + GRP_W])
        ref[:, GRP_W:ROW_W] = y[:, base + GRP_W:base + ROW_W]
    gate_ref[...] = y[:, A_QW + A_KVW:A_PROJ_PAD]
    for t_ref, src in zip(t_refs, (cmp_ref, sel_ref)):
        t_ref[0] = src[...].T


def nsa_project(x, g, w, tabs, tm, tokens_minor_seqs=None):
    n, d = x.shape
    n_tab = tabs[0].shape[0] // tm
    tab_spec = pl.BlockSpec((tm, GRP_W), lambda i: (i % n_tab, 0))
    row = lambda w_: pl.BlockSpec((tm, w_), lambda i: (i, 0))
    out_shape = [jax.ShapeDtypeStruct((n, A_QW), F32),
                 jax.ShapeDtypeStruct((n, ROW_W), F32),
                 jax.ShapeDtypeStruct((n, ROW_W), F32),
                 jax.ShapeDtypeStruct((n, ROW_W), F32),
                 jax.ShapeDtypeStruct((n, LANE), F32)]
    out_specs = [row(A_QW), row(ROW_W), row(ROW_W), row(ROW_W), row(LANE)]
    if tokens_minor_seqs is not None:
        t_len = n // tokens_minor_seqs
        n_t = t_len // tm
        for _ in range(2):
            out_shape.append(jax.ShapeDtypeStruct((tokens_minor_seqs, ROW_W, t_len), F32))
            out_specs.append(pl.BlockSpec((1, ROW_W, tm), lambda i: (i // n_t, 0, i % n_t)))
    return pl.pallas_call(
        _nsa_proj_body,
        out_shape=tuple(out_shape),
        grid=(n // tm,),
        in_specs=[row(d), pl.BlockSpec((1, d), lambda i: (0, 0)),
                  pl.BlockSpec((d, A_PROJ_PAD), lambda i: (0, 0)),
                  tab_spec, tab_spec, tab_spec],
        out_specs=tuple(out_specs),
        compiler_params=_params(1), name="nsa_project",
    )(x, g.reshape(1, d), w, *tabs)


def rope_tables(pos):
    half = ROT_DIM // 2
    inv_freq = ROPE_THETA ** (-(jnp.arange(half, dtype=F32) * (2.0 / ROT_DIM)))
    ang = pos.astype(F32)[:, None] * inv_freq[None, :]
    cos, sin = jnp.cos(ang), jnp.sin(ang)
    t = pos.shape[0]
    one = jnp.ones((t, A_DH - ROT_DIM), F32)
    zero = jnp.zeros((t, A_DH - ROT_DIM), F32)
    z8 = jnp.zeros((t, half), F32)
    c = jnp.concatenate([cos, cos, one], axis=1)
    sa = jnp.concatenate([-sin, z8, zero], axis=1)
    sb = jnp.concatenate([z8, sin, zero], axis=1)
    return tuple(jnp.tile(a, (1, A_KV)) for a in (c, sa, sb))


def _mlstm_body(q_ref, k_ref, v_ref, og_ref, gt_ref, c0_ref, n0_ref, m0_ref, gb_ref, hn_ref,
                hid_ref, c_out, n_out, m_out, c_s, n_s, m_s, *, chunk, t_last, nb):
    ci = pl.program_id(1)

    @pl.when(ci == 0)
    def _():
        c_s[...] = c0_ref[...]
        n_s[...] = n0_ref[...]
        m_s[...] = m0_ref[...]

    L = chunk
    row = _iota((L, L), 0)
    col = _iota((L, L), 1)
    causal = row >= col
    tril = causal.astype(F32)
    eye8 = (_iota((SUBLANE, LANE), 0) == _iota((SUBLANE, LANE), 1)).astype(F32)
    rlane = _iota((L, LANE), 0)
    lane1 = _iota((1, LANE), 1)
    sel_row = _iota((LANE, LANE), 0)
    ones_l = jnp.ones((L, LANE), BF16)
    ones_k = jnp.ones((M_DK, LANE), BF16)
    ones_v = jnp.ones((M_DV, LANE), BF16)

    def replicate(parts, col_idx):
        sel = jnp.where(sel_row == col_idx, 1.0, 0.0).astype(BF16)
        return _nn(parts[0], sel) + _nn(parts[1], sel) + _nn(parts[2], sel)

    units = [(bi, h) for bi in range(nb) for h in range(M_HEADS)]
    b_parts, g_parts, m_parts, g_rows = [], [], [], []
    for bi in range(nb):
        gt = gt_ref[bi] + gb_ref[...]
        log_f = jnp.minimum(gt, 0.0) - jnp.log1p(jnp.exp(-jnp.abs(gt)))
        ba = _nn(tril, log_f, HI)
        g_mat = gt - pltpu.roll(ba, LANE - M_HEADS, 1)
        b_parts.append(_split3(ba))
        g_parts.append(_split3(g_mat))
        m_parts.append(_split3(jnp.broadcast_to(m_s[bi], (SUBLANE, LANE))))
        g_rows.append(_nt(eye8, g_mat, HI))

    qs, ks, vs, cs, ns = {}, {}, {}, {}, {}
    s_raw, q_c, b_rep, g_rep, m_prev = {}, {}, {}, {}, {}
    for u in units:
        bi, h = u
        qs[u] = q_ref[bi, :, h * M_DK:(h + 1) * M_DK]
        ks[u] = k_ref[bi, :, h * M_DK:(h + 1) * M_DK] * (M_DK ** -0.5)
        vs[u] = v_ref[bi, :, h * M_DV:(h + 1) * M_DV]
        cs[u] = c_s[bi, h]
        ns[u] = n_s[bi, h:h + 1, :]
        s_raw[u] = _nt(qs[u].astype(BF16), ks[u].astype(BF16))
        q_c[u] = _nn(qs[u].astype(BF16), cs[u].astype(BF16))
        b_rep[u] = replicate(b_parts[bi], M_HEADS + h)
        g_rep[u] = replicate(g_parts[bi], h)
        m_prev[u] = replicate(m_parts[bi], h)[0:1, :]

    a_rep, m_t, w_d, w_a = {}, {}, {}, {}
    for u in units:
        bi, h = u
        pm = g_rep[u]
        shift = 1
        while shift < L:
            pm = jnp.where(rlane >= shift, jnp.maximum(pm, pltpu.roll(pm, shift, 0)), pm)
            shift *= 2
        a_rep[u] = b_rep[u] + m_prev[u]
        m_t[u] = jnp.maximum(a_rep[u], b_rep[u] + pm)
        dm = jnp.where(causal, b_rep[u][:, :L] + g_rows[bi][h:h + 1, :], NEG_INF)
        w_d[u] = jnp.exp(dm - m_t[u][:, :L])
        w_a[u] = jnp.exp(a_rep[u] - m_t[u])

    for u in units:
        bi, h = u
        s = s_raw[u] * w_d[u]
        s_b = s.astype(BF16)
        num = w_a[u] * q_c[u] + _nn(s_b, vs[u].astype(BF16))
        den = w_a[u] * _nn((qs[u] * ns[u]).astype(BF16), ones_k) + _nn(s_b, ones_l)
        h_out = num / jnp.maximum(jnp.abs(den), jnp.exp(-m_t[u]))
        msq = _nn((h_out * h_out).astype(BF16), ones_v) * (1.0 / M_DV)
        hid = h_out * lax.rsqrt(msq + NORM_EPS) * hn_ref[:, h * M_DV:(h + 1) * M_DV]
        hid_ref[bi, :, h * M_DV:(h + 1) * M_DV] = (
            jax.nn.sigmoid(og_ref[bi, :, h * M_DV:(h + 1) * M_DV]) * hid)

    m_rows = [m_s[bi] for bi in range(nb)]
    for u in units:
        bi, h = u
        m_new = m_t[u][t_last:t_last + 1, :]
        d_last = jnp.where(rlane <= t_last, b_rep[u][t_last:t_last + 1, :] + g_rep[u], NEG_INF)
        w_last = jnp.exp(d_last - m_new)
        decay = jnp.exp(a_rep[u][t_last:t_last + 1, :] - m_new)
        kw = ks[u] * w_last[:, :M_DK]
        c_s[bi, h] = decay * cs[u] + _tn(kw, vs[u], HI)
        n_s[bi, h:h + 1, :] = decay[:, :M_DK] * ns[u] + jnp.sum(kw, axis=0, keepdims=True)
        m_rows[bi] = jnp.where(lane1 == h, m_new, m_rows[bi])
    for bi in range(nb):
        m_s[bi] = m_rows[bi]

    @pl.when(ci == pl.num_programs(1) - 1)
    def _():
        c_out[...] = c_s[...]
        n_out[...] = n_s[...]
        m_out[...] = m_s[...]


MLSTM_SEQ_PER_STEP = 4


def mlstm_scan(proj, c0, n0, m0, gate_bias, head_norm, chunk, t_last):
    b, t, _ = proj.shape
    nb = MLSTM_SEQ_PER_STEP
    assert b % nb == 0
    n_chunks = t // chunk
    hd = M_HEADS * M_DK
    vd = M_HEADS * M_DV
    m0p = jnp.pad(m0, ((0, 0), (0, LANE - M_HEADS))).reshape(b, 1, LANE)
    body = functools.partial(_mlstm_body, chunk=chunk, t_last=t_last, nb=nb)
    hid, c_t, n_t, m_t = pl.pallas_call(
        body,
        out_shape=(jax.ShapeDtypeStruct((b, t, vd), F32),
                   jax.ShapeDtypeStruct((b, M_HEADS, M_DK, M_DV), F32),
                   jax.ShapeDtypeStruct((b, M_HEADS, M_DK), F32),
                   jax.ShapeDtypeStruct((b, 1, LANE), F32)),
        grid=(b // nb, n_chunks),
        in_specs=[pl.BlockSpec((nb, chunk, hd), lambda i, c: (i, c, 0)),
                  pl.BlockSpec((nb, chunk, hd), lambda i, c: (i, c, 1)),
                  pl.BlockSpec((nb, chunk, vd), lambda i, c: (i, c, 1)),
                  pl.BlockSpec((nb, chunk, vd), lambda i, c: (i, c, 2)),
                  pl.BlockSpec((nb, chunk, LANE), lambda i, c: (i, c, (2 * hd + 2 * vd) // LANE)),
                  pl.BlockSpec((nb, M_HEADS, M_DK, M_DV), lambda i, c: (i, 0, 0, 0)),
                  pl.BlockSpec((nb, M_HEADS, M_DK), lambda i, c: (i, 0, 0)),
                  pl.BlockSpec((nb, 1, LANE), lambda i, c: (i, 0, 0)),
                  pl.BlockSpec((1, LANE), lambda i, c: (0, 0)),
                  pl.BlockSpec((1, vd), lambda i, c: (0, 0))],
        out_specs=(pl.BlockSpec((nb, chunk, vd), lambda i, c: (i, c, 0)),
                   pl.BlockSpec((nb, M_HEADS, M_DK, M_DV), lambda i, c: (i, 0, 0, 0)),
                   pl.BlockSpec((nb, M_HEADS, M_DK), lambda i, c: (i, 0, 0)),
                   pl.BlockSpec((nb, 1, LANE), lambda i, c: (i, 0, 0))),
        scratch_shapes=[pltpu.VMEM((nb, M_HEADS, M_DK, M_DV), F32),
                        pltpu.VMEM((nb, M_HEADS, M_DK), F32),
                        pltpu.VMEM((nb, 1, LANE), F32)],
        compiler_params=_params(2), name="mlstm_scan",
    )(proj, proj, proj, proj, proj, c0, n0, m0p, gate_bias, head_norm.reshape(1, vd))
    return hid, c_t, n_t, m_t[:, 0, :M_HEADS]


N_CAND = (2 + SUBLANE) * SUBLANE
RANK_OUT = float(P_TOPK)


def _top_rows(xs, dsts, k, with_rank):
    xs = list(xs)
    ranks = [jnp.full(x.shape, RANK_OUT, F32) if w else None for x, w in zip(xs, with_rank)]
    for r in range(k):
        for i, dst in enumerate(dsts):
            m = jnp.max(xs[i], axis=1, keepdims=True)
            dst[:, r:r + 1, :] = m
            hit = xs[i] == m
            if ranks[i] is not None:
                ranks[i] = jnp.where(hit, float(r), ranks[i])
            xs[i] = jnp.where(hit, NEG_INF, xs[i])
    return ranks


def _peer_select_body(x_ref, g_ref, wq_ref, k1_ref, k2_ref, r2_ref, e2_ref, l_ref, c_ref,
                      s1_s, s2_s, v1_s, v2_s, cand_s, vals_s):
    qp = _nn(_rms(x_ref[...], g_ref[...]).astype(BF16), wq_ref[...])
    for h in range(P_HEADS):
        q1 = qp[:, h * P_QDIM:h * P_QDIM + P_QDIM // 2].astype(BF16)
        q2 = qp[:, h * P_QDIM + P_QDIM // 2:(h + 1) * P_QDIM].astype(BF16)
        s1_s[h] = _nt(k1_ref[h], q1)
        s2_s[h] = _nt(k2_ref[h], q2)
    s1 = s1_s[...]
    s2 = s2_s[...]
    _, rank2 = _top_rows((s1, s2), (v1_s, v2_s), P_TOPK, (False, True))
    v2_lo = v2_s[:, 0:SUBLANE, :]
    cand_s[:, 0:SUBLANE, :] = v2_lo + v1_s[:, 0:1, :]
    cand_s[:, SUBLANE:2 * SUBLANE, :] = v2_s[:, SUBLANE:2 * SUBLANE, :] + v1_s[:, 0:1, :]
    cand_s[:, 2 * SUBLANE:3 * SUBLANE, :] = v1_s[:, SUBLANE:2 * SUBLANE, :] + v2_s[:, 0:1, :]
    for r in range(1, SUBLANE):
        cand_s[:, (2 + r) * SUBLANE:(3 + r) * SUBLANE, :] = v2_lo + v1_s[:, r:r + 1, :]
    _top_rows((cand_s[...],), (vals_s,), P_TOPK, (False,))
    top = vals_s[:, 0:1, :]
    tau = vals_s[:, P_TOPK - 1:P_TOPK, :]
    z = jnp.sum(jnp.exp(vals_s[...] - top), axis=1, keepdims=True)
    v1 = v1_s[...]
    kept = jnp.zeros(v1.shape, F32)
    for r in range(P_TOPK):
        kept = kept + jnp.where(v2_s[:, r:r + 1, :] + v1 >= tau, 1.0, 0.0)
    n_keep = jnp.zeros(s1.shape, F32)
    for r in range(P_TOPK):
        n_keep = jnp.where(s1 == v1_s[:, r:r + 1, :], kept[:, r:r + 1, :], n_keep)
    r2_ref[...] = rank2.astype(BF16)
    e2_ref[...] = jnp.exp(s2 - v2_s[:, 0:1, :]).astype(BF16)
    l_ref[...] = n_keep
    c_ref[...] = 0.5 * jnp.exp(s1 - v1_s[:, 0:1, :]) / z


def peer_select(x, g, w_q, k1, k2, tm):
    n, d = x.shape
    spec = pl.BlockSpec((P_HEADS, P_NKEYS, tm), lambda i: (0, 0, i))
    key_spec = pl.BlockSpec((P_HEADS, P_NKEYS, P_QDIM // 2), lambda i: (0, 0, 0))
    shp = lambda dt: jax.ShapeDtypeStruct((P_HEADS, P_NKEYS, n), dt)
    return pl.pallas_call(
        _peer_select_body,
        out_shape=(shp(BF16), shp(BF16), shp(F32), shp(F32)),
        grid=(n // tm,),
        in_specs=[pl.BlockSpec((tm, d), lambda i: (i, 0)),
                  pl.BlockSpec((1, d), lambda i: (0, 0)),
                  pl.BlockSpec((d, P_HEADS * P_QDIM), lambda i: (0, 0)),
                  key_spec, key_spec],
        out_specs=(spec, spec, spec, spec),
        scratch_shapes=[pltpu.VMEM((P_HEADS, P_NKEYS, tm), F32), pltpu.VMEM((P_HEADS, P_NKEYS, tm), F32),
                        pltpu.VMEM((P_HEADS, P_TOPK, tm), F32), pltpu.VMEM((P_HEADS, P_TOPK, tm), F32),
                        pltpu.VMEM((P_HEADS, N_CAND, tm), F32), pltpu.VMEM((P_HEADS, P_TOPK, tm), F32)],
        compiler_params=_params(1), name="peer_select",
    )(x, g.reshape(1, d), w_q, k1, k2)


def _peer_main_body(x_ref, g_ref, og_ref, u_ref, vt_ref, r2_ref, e2_ref, l_ref, c_ref,
                    o_ref, hn_s, acc_s, *, a_tile, norm_out):
    t = pl.program_id(1)

    @pl.when(t == 0)
    def _():
        hn_s[...] = _rms(x_ref[...], g_ref[...]).astype(BF16)
        acc_s[...] = jnp.zeros_like(acc_s)

    tm = hn_s.shape[0]
    zero = jnp.zeros((P_NKEYS, tm), BF16)
    a_sub = PEER_SUB_EXPERTS // P_NKEYS
    total = None
    for sub in range(a_tile // a_sub):
        es = slice(sub * PEER_SUB_EXPERTS, (sub + 1) * PEER_SUB_EXPERTS)
        sc = _nt(u_ref[es, :], hn_s[...])
        act = (sc + sc * lax.erf(sc * (2.0 ** -0.5))).astype(BF16)
        parts = []
        for k in range(a_sub):
            aa = sub * a_sub + k
            w = None
            for h in range(P_HEADS):
                keep = jnp.broadcast_to(l_ref[h, aa:aa + 1, :], (P_NKEYS, tm)).astype(BF16)
                coef = jnp.broadcast_to(c_ref[h, aa:aa + 1, :], (P_NKEYS, tm)).astype(BF16)
                term = jnp.where(r2_ref[h] < keep, e2_ref[h], zero) * coef
                w = term if w is None else w + term
            parts.append(w * act[k * P_NKEYS:(k + 1) * P_NKEYS])
        part = _nn(vt_ref[:, es], jnp.concatenate(parts, axis=0))
        total = part if total is None else total + part
    acc_s[...] += total

    @pl.when(t == pl.num_programs(1) - 1)
    def _():
        y = x_ref[...] + acc_s[...].T
        o_ref[...] = _rms(y, og_ref[...]) if norm_out else y


def peer_main(x, g, u, vt, layer, sel, tm, te, out_norm=None):
    n, d = x.shape
    og = g if out_norm is None else out_norm
    r2, e2, n_keep, coef = sel
    a_tile = te // P_NKEYS
    n_tiles = P_EXPERTS // te
    assert a_tile % SUBLANE == 0
    full_spec = pl.BlockSpec((P_HEADS, P_NKEYS, tm), lambda i, t: (0, 0, i))
    row_spec = pl.BlockSpec((P_HEADS, a_tile, tm), lambda i, t: (0, t, i))
    body = functools.partial(_peer_main_body, a_tile=a_tile, norm_out=out_norm is not None)
    return pl.pallas_call(
        body,
        out_shape=jax.ShapeDtypeStruct((n, d), F32),
        grid=(n // tm, n_tiles),
        in_specs=[pl.BlockSpec((tm, d), lambda i, t: (i, 0)),
                  pl.BlockSpec((1, d), lambda i, t: (0, 0)),
                  pl.BlockSpec((1, d), lambda i, t: (0, 0)),
                  pl.BlockSpec((None, te, d), lambda i, t: (layer, t, 0)),
                  pl.BlockSpec((None, d, te), lambda i, t: (layer, 0, t)),
                  full_spec, full_spec, row_spec, row_spec],
        out_specs=pl.BlockSpec((tm, d), lambda i, t: (i, 0)),
        scratch_shapes=[pltpu.VMEM((tm, d), BF16), pltpu.VMEM((d, tm), F32)],
        compiler_params=_params(2), name="peer_main",
    )(x, g.reshape(1, d), og.reshape(1, d), u, vt, r2, e2, n_keep, coef)


def peer_ffn_residual(x, g, w_q, k1, k2, u, vt, layer, tm, te, out_norm=None):
    sel = peer_select(x, g, w_q, k1, k2, tm)
    return peer_main(x, g, u, vt, layer, sel, tm, te, out_norm)


def _topk_mask(score, lane_f, k):
    sel = jnp.zeros(score.shape, F32)
    for _ in range(k):
        m = jnp.max(score, axis=1, keepdims=True)
        idx = jnp.min(jnp.where(score == m, lane_f, 1e9), axis=1, keepdims=True)
        hit = lane_f == idx
        sel = jnp.where(hit, 1.0, sel)
        score = jnp.where(hit, NEG_INF, score)
    return sel


def _block_scores(imp, pos_col, lane):
    cur = lax.shift_right_arithmetic(pos_col, SEL_BLK_LOG2)
    forced = (lane == 0) | (lane == cur) | (lane == cur - 1)
    return jnp.where(forced, jnp.inf, jnp.where(lane <= cur, imp, NEG_INF))


def _stack_heads(q, g):
    return jnp.concatenate(
        [q[:, (g * A_REP + r) * A_DH:(g * A_REP + r + 1) * A_DH] for r in range(A_REP)], axis=0)


def _combine_heads(o_ref, gate, o_cmp, o_sel, o_win, g, rows):
    for r in range(A_REP):
        h = g * A_REP + r
        rs = slice(r * rows, (r + 1) * rows)
        o_ref[:, h * A_DH:(h + 1) * A_DH] = (gate[:, 3 * h:3 * h + 1] * o_cmp[rs]
                                             + gate[:, 3 * h + 1:3 * h + 2] * o_sel[rs]
                                             + gate[:, 3 * h + 2:3 * h + 3] * o_win[rs])


def _online_softmax_step(m_ref, l_ref, acc_ref, g, sc, valid, pv):
    sc = jnp.where(valid, sc, NEG_INF)
    m_old = m_ref[g]
    m_new = jnp.maximum(m_old, jnp.max(sc, axis=1, keepdims=True))
    m_safe = jnp.where(m_new == NEG_INF, 0.0, m_new)
    alpha = jnp.exp(m_old - m_safe)
    e = jnp.exp(sc - m_safe)
    l_ref[g] = alpha * l_ref[g] + jnp.sum(e, axis=1, keepdims=True)
    acc_ref[g] = alpha * acc_ref[g] + pv(e.astype(BF16))
    m_ref[g] = m_new


def _compress_weights(w1, b1, w2):
    eye = jnp.eye(A_KV, dtype=F32)
    w1bd = jnp.einsum("cldh,gk->clgdkh", w1, eye).reshape(2, CMP_LEN, GRP_W, GRP_W).astype(BF16)
    w1cat = jnp.concatenate([w1bd[:, :CMP_STRIDE], w1bd[:, CMP_STRIDE:]], axis=-1)
    w2bd = jnp.einsum("chd,gk->cghkd", w2, eye).reshape(2, GRP_W, GRP_W).astype(BF16)
    b1t = jnp.tile(b1, (1, A_KV)).reshape(2, 1, GRP_W)
    return w1cat, b1t, w2bd


def _split3(x):
    hi = x.astype(BF16)
    r1 = x - hi.astype(F32)
    mid = r1.astype(BF16)
    lo = (r1 - mid.astype(F32)).astype(BF16)
    return hi, mid, lo


def _block_importance(p_sum, ov_ref):
    hi, mid, lo = _split3(p_sum)
    ov = ov_ref[...]
    return _nn(hi, ov) + _nn(mid, ov) + _nn(lo, ov)


def _overlap_matrix(n_rows, n_cols, shift):
    c_start = (np.arange(n_rows) - shift) * CMP_STRIDE
    s_start = np.arange(n_cols) * SEL_BLK
    ov = np.clip(np.minimum(c_start[:, None] + CMP_LEN, s_start[None, :] + SEL_BLK)
                 - np.maximum(c_start[:, None], s_start[None, :]), 0, None) / CMP_STRIDE
    ov[c_start < 0] = 0.0
    return jnp.asarray(ov, BF16)


R16_PER_PAGE = PAGE_SIZE // CMP_STRIDE


def _offset_grouping_perm():
    out_row = _iota((PAGE_SIZE, PAGE_SIZE), 0)
    src_tok = (CMP_STRIDE * (out_row & (R16_PER_PAGE - 1))
               + lax.shift_right_logical(out_row, R16_PER_PAGE.bit_length() - 1))
    return jnp.where(_iota((PAGE_SIZE, PAGE_SIZE), 1) == src_tok, 1.0, 0.0).astype(BF16)


def _store_by_offset(x_s, group, x_t):
    for l in range(CMP_STRIDE):
        x_s[l, group * R16_PER_PAGE:(group + 1) * R16_PER_PAGE, :] = (
            x_t[l * R16_PER_PAGE:(l + 1) * R16_PER_PAGE, :])


def _cmp_prompt_body(r_ref, w1_ref, b1_ref, w2_ref, ck_ref, cv_ref, x_s):
    perm = _offset_grouping_perm()
    for k in range(r_ref.shape[0] // PAGE_SIZE):
        _store_by_offset(x_s, k, _nn(perm, r_ref[k * PAGE_SIZE:(k + 1) * PAGE_SIZE, :].astype(BF16)))
    for c, out_ref in ((0, ck_ref), (1, cv_ref)):
        both = None
        for l in range(CMP_STRIDE):
            y = _nn(x_s[l, :, c * GRP_W:(c + 1) * GRP_W].astype(BF16), w1_ref[c, l])
            both = y if both is None else both + y
        first, second = both[:, :GRP_W], both[:, GRP_W:]
        n = first.shape[0]
        hid = _gelu(first + pltpu.roll(second, n - 1, 0) + b1_ref[c])
        out_ref[0] = _nn(hid.astype(BF16), w2_ref[c])


def compress_prompt(rows, bsz, w1bd, b1t, w2bd):
    t = rows.shape[0] // bsz
    n = t // CMP_STRIDE
    assert t % PAGE_SIZE == 0
    out = jax.ShapeDtypeStruct((bsz, n, GRP_W), F32)
    return pl.pallas_call(
        _cmp_prompt_body,
        out_shape=(out, out),
        grid=(bsz,),
        in_specs=[pl.BlockSpec((t, ROW_W), lambda i: (i, 0)),
                  pl.BlockSpec(w1bd.shape, lambda i: (0, 0, 0, 0)),
                  pl.BlockSpec(b1t.shape, lambda i: (0, 0, 0)),
                  pl.BlockSpec(w2bd.shape, lambda i: (0, 0, 0))],
        out_specs=(pl.BlockSpec((1, n, GRP_W), lambda i: (i, 0, 0)),
                   pl.BlockSpec((1, n, GRP_W), lambda i: (i, 0, 0))),
        scratch_shapes=[pltpu.VMEM((CMP_STRIDE, n, ROW_W), F32)],
        compiler_params=_params(1), name="compress_prompt",
    )(rows, w1bd, b1t, w2bd)


NSA_KEY_CHUNK = 512


def _nsa_prompt_body(q_ref, gate_ref, ck_ref, cv_ref, sel_ref, win_ref, ov_ref, ex_ref, bg_ref,
                     o_ref, mask_s, m_s, l_s, acc_s, ocmp_s, *, tq, t_len):
    i = pl.program_id(1)
    q0 = i * tq
    q = q_ref[...] * (ATTN_SCALE * LOG2E)
    pos = q0 + _iota((tq, 1), 0)
    pos4 = jnp.concatenate([pos] * A_REP, axis=0)
    n_cmp_rows = ck_ref.shape[1]
    n_lane = _iota((1, n_cmp_rows), 1)
    valid_cmp = (n_lane * CMP_STRIDE + (CMP_LEN - 1) <= pos) & (n_lane < n_cmp_rows - 1)
    bias_cmp = jnp.where(valid_cmp, 0.0, NEG_INF)
    qgs = [_stack_heads(q, g).astype(BF16) for g in range(A_KV)]

    def add_bias(s, bias):
        return jnp.concatenate([s[r * tq:(r + 1) * tq] + bias for r in range(A_REP)], axis=0)

    imps = []
    for g in range(A_KV):
        gs = slice(g * A_DH, (g + 1) * A_DH)
        s_c = add_bias(_nt(qgs[g], ck_ref[0, :, gs].astype(BF16)), bias_cmp)
        m_c = jnp.max(s_c, axis=1, keepdims=True)
        e_c = jnp.exp2(s_c - jnp.where(m_c == NEG_INF, 0.0, m_c))
        p_cmp = e_c / jnp.maximum(jnp.sum(e_c, axis=1, keepdims=True), 1e-30)
        ocmp_s[g] = _nn(p_cmp.astype(BF16), cv_ref[0, :, gs].astype(BF16))
        p_sum = p_cmp[0:tq]
        for r in range(1, A_REP):
            p_sum = p_sum + p_cmp[r * tq:(r + 1) * tq]
        imps.append(_block_importance(p_sum, ov_ref))
    blk_lane = _iota((A_KV * tq, LANE), 1)
    score = _block_scores(jnp.concatenate(imps, axis=0), pos4, blk_lane)
    few_blocks = q0 + tq <= N_SEL * SEL_BLK

    @pl.when(few_blocks)
    def _():
        mask_s[...] = jnp.where(score == NEG_INF, 0.0, 1.0)

    @pl.when(jnp.logical_not(few_blocks))
    def _():
        mask_s[...] = _topk_mask(score, blk_lane.astype(F32), N_SEL)

    m_s[...] = jnp.full_like(m_s, NEG_INF)
    l_s[...] = jnp.zeros_like(l_s)
    acc_s[...] = jnp.zeros_like(acc_s)
    kc = NSA_KEY_CHUNK
    for c in range(t_len // kc):
        @pl.when(c * kc < q0 + tq)
        def _(c=c):
            key_lane = c * kc + _iota((1, kc), 1)
            causal_bias = jnp.where(key_lane <= pos, 0.0, MASKED)
            for g in range(A_KV):
                blk_bias = _nn((mask_s[g * tq:(g + 1) * tq, :] - 1.0).astype(BF16),
                               ex_ref[:, c * kc:(c + 1) * kc])
                k = sel_ref[c * kc:(c + 1) * kc, g * A_DH:(g + 1) * A_DH].astype(BF16)
                v = sel_ref[c * kc:(c + 1) * kc,
                            GRP_W + g * A_DH:GRP_W + (g + 1) * A_DH].astype(BF16)
                sc = add_bias(_nt(qgs[g], k), blk_bias + causal_bias)
                m_old = m_s[g]
                m_new = jnp.maximum(m_old, jnp.max(sc, axis=1, keepdims=True))
                alpha = jnp.exp2(m_old - m_new)
                e = jnp.exp2(sc - m_new)
                l_s[g] = alpha * l_s[g] + jnp.sum(e, axis=1, keepdims=True)
                acc_s[g] = alpha * acc_s[g] + _nn(e.astype(BF16), v)
                m_s[g] = m_new

    gate = jax.nn.sigmoid(gate_ref[...] + bg_ref[...])
    band = WINDOW + tq
    w_start = pl.multiple_of(jnp.maximum(q0 - WINDOW, 0), tq)
    kpos_w = w_start + _iota((1, band), 1)
    diff_w = pos - kpos_w
    bias_w = jnp.where((diff_w >= 0) & (diff_w <= WINDOW), 0.0, MASKED)
    for g in range(A_KV):
        gs = slice(g * A_DH, (g + 1) * A_DH)
        vs_ = slice(GRP_W + g * A_DH, GRP_W + (g + 1) * A_DH)
        kw = win_ref[pl.ds(w_start, band), gs].astype(BF16)
        vw = win_ref[pl.ds(w_start, band), vs_].astype(BF16)
        s_w = add_bias(_nt(qgs[g], kw), bias_w)
        e_w = jnp.exp2(s_w - jnp.max(s_w, axis=1, keepdims=True))
        o_win = _nn(e_w.astype(BF16), vw) / jnp.sum(e_w, axis=1, keepdims=True)
        o_sel = acc_s[g] / jnp.maximum(l_s[g], 1e-30)
        _combine_heads(o_ref, gate, ocmp_s[g], o_sel, o_win, g, tq)


def nsa_prompt_attention(q, gate_pre, ck, cv, sel_rows, win_rows, b_gate_row, bsz, t_len, tq):
    n = q.shape[0]
    n_q = t_len // tq
    n_blk = t_len // SEL_BLK
    assert t_len % NSA_KEY_CHUNK == 0 and n_blk <= LANE
    ov = jnp.pad(_overlap_matrix(ck.shape[1], n_blk, 0), ((0, 0), (0, LANE - n_blk)))
    ex = (np.arange(LANE)[:, None] == (np.arange(t_len)[None, :] // SEL_BLK))
    ex = jnp.asarray(ex * -MASKED, BF16)
    rows4 = A_REP * tq
    body = functools.partial(_nsa_prompt_body, tq=tq, t_len=t_len)
    return pl.pallas_call(
        body,
        out_shape=jax.ShapeDtypeStruct((n, A_QW), F32),
        grid=(bsz, n_q),
        in_specs=[pl.BlockSpec((tq, A_QW), lambda b, i: (b * n_q + i, 0)),
                  pl.BlockSpec((tq, LANE), lambda b, i: (b * n_q + i, 0)),
                  pl.BlockSpec((1,) + ck.shape[1:], lambda b, i: (b, 0, 0)),
                  pl.BlockSpec((1,) + cv.shape[1:], lambda b, i: (b, 0, 0)),
                  pl.BlockSpec((t_len, ROW_W), lambda b, i: (b, 0)),
                  pl.BlockSpec((t_len, ROW_W), lambda b, i: (b, 0)),
                  pl.BlockSpec(ov.shape, lambda b, i: (0, 0)),
                  pl.BlockSpec(ex.shape, lambda b, i: (0, 0)),
                  pl.BlockSpec((1, LANE), lambda b, i: (0, 0))],
        out_specs=pl.BlockSpec((tq, A_QW), lambda b, i: (b * n_q + i, 0)),
        scratch_shapes=[pltpu.VMEM((A_KV * tq, LANE), F32),
                        pltpu.VMEM((A_KV, rows4, 1), F32),
                        pltpu.VMEM((A_KV, rows4, 1), F32),
                        pltpu.VMEM((A_KV, rows4, A_DH), F32),
                        pltpu.VMEM((A_KV, rows4, A_DH), F32)],
        compiler_params=_params(2), name="nsa_prompt_attention",
    )(q, gate_pre, ck, cv, sel_rows, win_rows, ov, ex, b_gate_row)


DEC_ROWS = 8
CMP_PAGES_PER_STEP = 32
SEL_PAGES_PER_STEP = 32


def _dec_cmp_body(pt_ref, *refs, past_len, n_blk_pad):
    del pt_ref
    pages = refs[:CMP_PAGES_PER_STEP]
    (w1_ref, b1_ref, w2_ref, q_ref, ov_ref, ocmp_ref, msel_ref,
     ck_s, cv_s, carry_s, x_s) = refs[CMP_PAGES_PER_STEP:]
    s = pl.program_id(1)
    rows = CMP_PAGES_PER_STEP * R16_PER_PAGE

    @pl.when(s == 0)
    def _():
        carry_s[...] = jnp.zeros_like(carry_s)

    perm = _offset_grouping_perm()
    for k, p in enumerate(pages):
        _store_by_offset(x_s, k, _nt(perm, p[0].astype(BF16)))

    rid = _iota((rows, 1), 0)
    for c, dst in ((0, ck_s), (1, cv_s)):
        both = None
        for l in range(CMP_STRIDE):
            x = x_s[l, :, c * GRP_W:(c + 1) * GRP_W].astype(BF16)
            y = _nn(x, w1_ref[c, l])
            both = y if both is None else both + y
        first, second = both[:, :GRP_W], both[:, GRP_W:]
        prev = carry_s[c, SUBLANE - 1:SUBLANE, :]
        shifted = jnp.where(rid == 0, prev, pltpu.roll(first, 1, 0))
        hid = _gelu(shifted + second + b1_ref[c])
        dst[pl.ds(pl.multiple_of(s * rows, rows), rows), :] = _nn(hid.astype(BF16), w2_ref[c])
        carry_s[c] = first[rows - SUBLANE:rows, :]

    @pl.when(s == pl.num_programs(1) - 1)
    def _():
        n_rows = ck_s.shape[0]
        q = q_ref[0]
        t_col = _iota((DEC_ROWS, 1), 0)
        pos = past_len + t_col
        pos4 = jnp.concatenate([pos] * A_REP, axis=0)
        m_lane = _iota((1, n_rows), 1)
        valid = (m_lane >= 1) & ((m_lane - 1) * CMP_STRIDE + (CMP_LEN - 1) <= pos4)
        imps = []
        for g in range(A_KV):
            gs = slice(g * A_DH, (g + 1) * A_DH)
            qg = _stack_heads(q, g).astype(BF16)
            p = _masked_softmax(_nt(qg, ck_s[:, gs].astype(BF16)) * ATTN_SCALE, valid)
            o = _nn(p.astype(BF16), cv_s[:, gs].astype(BF16))
            for r in range(A_REP):
                h = g * A_REP + r
                ocmp_ref[0, :, h * A_DH:(h + 1) * A_DH] = o[r * DEC_ROWS:(r + 1) * DEC_ROWS]
            p_sum = p[0:DEC_ROWS]
            for r in range(1, A_REP):
                p_sum = p_sum + p[r * DEC_ROWS:(r + 1) * DEC_ROWS]
            imps.append(_block_importance(p_sum, ov_ref))
        imp = jnp.concatenate(imps, axis=0)
        lane = _iota((A_KV * DEC_ROWS, n_blk_pad), 1)
        score = _block_scores(imp, pos4, lane)
        mask = _topk_mask(score, lane.astype(F32), N_SEL)
        for j in range(n_blk_pad // LANE):
            msel_ref[0, j] = mask[:, j * LANE:(j + 1) * LANE]


def nsa_decode_compress(cache_t, page_table, w1bd, b1t, w2bd, q8, past_len, n_blk_pad):
    dbsz, n_pages = page_table.shape
    assert n_pages % CMP_PAGES_PER_STEP == 0
    n_steps = n_pages // CMP_PAGES_PER_STEP
    n_rows = n_pages * R16_PER_PAGE
    n_tiles = n_blk_pad // LANE
    ov = _overlap_matrix(n_rows, n_blk_pad, 1)

    def page_spec(k):
        return pl.BlockSpec((1, ROW_W, PAGE_SIZE),
                            lambda b, s, pt: (pt[b, s * CMP_PAGES_PER_STEP + k], 0, 0))

    grid_spec = pltpu.PrefetchScalarGridSpec(
        num_scalar_prefetch=1,
        grid=(dbsz, n_steps),
        in_specs=[page_spec(k) for k in range(CMP_PAGES_PER_STEP)] + [
            pl.BlockSpec(w1bd.shape, lambda b, s, pt: (0, 0, 0, 0)),
            pl.BlockSpec(b1t.shape, lambda b, s, pt: (0, 0, 0)),
            pl.BlockSpec(w2bd.shape, lambda b, s, pt: (0, 0, 0)),
            pl.BlockSpec((1, DEC_ROWS, A_QW), lambda b, s, pt: (b, 0, 0)),
            pl.BlockSpec(ov.shape, lambda b, s, pt: (0, 0))],
        out_specs=(pl.BlockSpec((1, DEC_ROWS, A_QW), lambda b, s, pt: (b, 0, 0)),
                   pl.BlockSpec((1, n_tiles, A_KV * DEC_ROWS, LANE), lambda b, s, pt: (b, 0, 0, 0))),
        scratch_shapes=[pltpu.VMEM((n_rows, GRP_W), F32), pltpu.VMEM((n_rows, GRP_W), F32),
                        pltpu.VMEM((2, SUBLANE, GRP_W), F32),
                        pltpu.VMEM((CMP_STRIDE, CMP_PAGES_PER_STEP * R16_PER_PAGE, ROW_W), F32)])
    body = functools.partial(_dec_cmp_body, past_len=past_len, n_blk_pad=n_blk_pad)
    return pl.pallas_call(
        body,
        out_shape=(jax.ShapeDtypeStruct((dbsz, DEC_ROWS, A_QW), F32),
                   jax.ShapeDtypeStruct((dbsz, n_tiles, A_KV * DEC_ROWS, LANE), F32)),
        grid_spec=grid_spec,
        compiler_params=_params(2), name="nsa_decode_compress",
    )(page_table, *([cache_t] * CMP_PAGES_PER_STEP), w1bd, b1t, w2bd, q8, ov)


def _dec_sel_body(pt_ref, *refs, past_len, t_new):
    del pt_ref
    pages = refs[:SEL_PAGES_PER_STEP]
    (q_ref, msel_ref, ocmp_ref, snew_ref, wpast_ref, wnew_ref, gate_ref, bg_ref,
     o_ref, m_s, l_s, acc_s) = refs[SEL_PAGES_PER_STEP:]
    s = pl.program_id(1)
    keys = SEL_PAGES_PER_STEP * PAGE_SIZE
    blks = keys // SEL_BLK
    steps_per_tile = LANE // blks

    @pl.when(s == 0)
    def _():
        m_s[...] = jnp.full_like(m_s, NEG_INF)
        l_s[...] = jnp.zeros_like(l_s)
        acc_s[...] = jnp.zeros_like(acc_s)

    q = q_ref[0]
    tile = s // steps_per_tile
    first_blk = (s - tile * steps_per_tile) * blks
    expand = (_iota((LANE, keys), 0)
              == first_blk + lax.shift_right_arithmetic(_iota((LANE, keys), 1), SEL_BLK_LOG2))
    expand = jnp.where(expand, 1.0, 0.0).astype(BF16)
    m_keys = _nn(msel_ref[0, tile].astype(BF16), expand)

    for g in range(A_KV):
        gs = slice(g * A_DH, (g + 1) * A_DH)
        vs_ = slice(GRP_W + g * A_DH, GRP_W + (g + 1) * A_DH)
        qg = _stack_heads(q, g).astype(BF16)
        k_t = jnp.concatenate([p[0, gs, :] for p in pages], axis=1).astype(BF16)
        v_t = jnp.concatenate([p[0, vs_, :] for p in pages], axis=1).astype(BF16)
        valid = jnp.concatenate([m_keys[g * DEC_ROWS:(g + 1) * DEC_ROWS]] * A_REP, axis=0) > 0.5
        _online_softmax_step(m_s, l_s, acc_s, g, _nn(qg, k_t) * ATTN_SCALE, valid,
                             lambda e, v_t=v_t: _nt(e, v_t))

    @pl.when(s == pl.num_programs(1) - 1)
    def _():
        t_col = _iota((DEC_ROWS, 1), 0)
        t4 = jnp.concatenate([t_col] * A_REP, axis=0)
        t_key = _iota((1, DEC_ROWS), 1)
        valid_new = (t_key <= t4) & (t_key < t_new)
        gate = jax.nn.sigmoid(gate_ref[0] + bg_ref[...])
        n_win = wpast_ref.shape[2]
        kpos_w = past_len - n_win + _iota((1, n_win), 1)
        diff_w = (past_len + t4) - kpos_w
        valid_wp = (diff_w >= 0) & (diff_w <= WINDOW)
        new_blk = past_len // SEL_BLK
        new_tile, new_lane = new_blk // LANE, new_blk % LANE
        for g in range(A_KV):
            gs = slice(g * A_DH, (g + 1) * A_DH)
            vs_ = slice(GRP_W + g * A_DH, GRP_W + (g + 1) * A_DH)
            qg = _stack_heads(q, g).astype(BF16)
            mnew = msel_ref[0, new_tile, g * DEC_ROWS:(g + 1) * DEC_ROWS, new_lane:new_lane + 1]
            mnew4 = jnp.concatenate([mnew] * A_REP, axis=0)
            v_new = snew_ref[0, :, vs_].astype(BF16)
            _online_softmax_step(m_s, l_s, acc_s, g,
                                 _nt(qg, snew_ref[0, :, gs].astype(BF16)) * ATTN_SCALE,
                                 valid_new & (mnew4 > 0.5), lambda e, v_new=v_new: _nn(e, v_new))
            o_sel = acc_s[g] / jnp.maximum(l_s[g], 1e-30)

            s_p = jnp.where(valid_wp, _nn(qg, wpast_ref[0, gs, :].astype(BF16)) * ATTN_SCALE, NEG_INF)
            s_n = jnp.where(valid_new, _nt(qg, wnew_ref[0, :, gs].astype(BF16)) * ATTN_SCALE, NEG_INF)
            mx = jnp.maximum(jnp.max(s_p, axis=1, keepdims=True), jnp.max(s_n, axis=1, keepdims=True))
            mx = jnp.where(mx == NEG_INF, 0.0, mx)
            e_p = jnp.exp(s_p - mx)
            e_n = jnp.exp(s_n - mx)
            den = jnp.maximum(jnp.sum(e_p, axis=1, keepdims=True)
                              + jnp.sum(e_n, axis=1, keepdims=True), 1e-30)
            o_win = (_nt(e_p.astype(BF16), wpast_ref[0, vs_, :].astype(BF16))
                     + _nn(e_n.astype(BF16), wnew_ref[0, :, vs_].astype(BF16))) / den
            o_cmp = _stack_heads(ocmp_ref[0], g)
            _combine_heads(o_ref.at[0], gate, o_cmp, o_sel, o_win, g, DEC_ROWS)


def nsa_decode_attention(cache_t, page_table, q8, msel, ocmp, sel_new, win_past_t, win_new,
                         gate8, b_gate_row, past_len, t_new):
    dbsz, n_pages = page_table.shape
    keys = SEL_PAGES_PER_STEP * PAGE_SIZE
    assert n_pages % SEL_PAGES_PER_STEP == 0 and past_len % SEL_BLK == 0
    assert LANE % (keys // SEL_BLK) == 0
    n_steps = n_pages // SEL_PAGES_PER_STEP
    n_win = win_past_t.shape[2]

    def page_spec(k):
        return pl.BlockSpec((1, ROW_W, PAGE_SIZE),
                            lambda b, s, pt: (pt[b, s * SEL_PAGES_PER_STEP + k], 0, 0))

    per_b = lambda shp: pl.BlockSpec((1,) + shp, lambda b, s, pt: (b,) + (0,) * len(shp))
    grid_spec = pltpu.PrefetchScalarGridSpec(
        num_scalar_prefetch=1,
        grid=(dbsz, n_steps),
        in_specs=[page_spec(k) for k in range(SEL_PAGES_PER_STEP)] + [
            per_b((DEC_ROWS, A_QW)), per_b(msel.shape[1:]), per_b((DEC_ROWS, A_QW)),
            per_b((DEC_ROWS, ROW_W)), per_b((ROW_W, n_win)), per_b((DEC_ROWS, ROW_W)),
            per_b((DEC_ROWS, LANE)), pl.BlockSpec((1, LANE), lambda b, s, pt: (0, 0))],
        out_specs=per_b((DEC_ROWS, A_QW)),
        scratch_shapes=[pltpu.VMEM((A_KV, A_REP * DEC_ROWS, 1), F32),
                        pltpu.VMEM((A_KV, A_REP * DEC_ROWS, 1), F32),
                        pltpu.VMEM((A_KV, A_REP * DEC_ROWS, A_DH), F32)])
    body = functools.partial(_dec_sel_body, past_len=past_len, t_new=t_new)
    return pl.pallas_call(
        body,
        out_shape=jax.ShapeDtypeStruct((dbsz, DEC_ROWS, A_QW), F32),
        grid_spec=grid_spec,
        compiler_params=_params(2), name="nsa_decode_attention",
    )(page_table, *([cache_t] * SEL_PAGES_PER_STEP), q8, msel, ocmp, sel_new, win_past_t,
      win_new, gate8, b_gate_row)


def tokens_minor(rows):
    n, t = rows.shape[:2]
    return jnp.transpose(rows, (0, 2, 3, 4, 1)).reshape(n, ROW_W, t)


PROMPT_TM = 512
PEER_TM = 512
PEER_TE = 2048
PEER_SUB_EXPERTS = 1024
NSA_TQ = 128


def kernel(x_prompt, x_sample, cache_cmp_kv, cache_sel_kv, state_win_kv, state_C, state_n, state_m,
           page_table, norm_mix, norm_ffn, norm_final, mlstm_w_in, mlstm_b_i, mlstm_b_f,
           mlstm_head_norm, mlstm_w_out, nsa_w_in, nsa_b_gate, nsa_cmp_w1, nsa_cmp_b1, nsa_cmp_w2,
           nsa_w_out, peer_w_q, peer_sub_keys, peer_u, peer_v):
    bsz, t_len, d = x_prompt.shape
    dbsz, t_s, _ = x_sample.shape
    n_pages = page_table.shape[1]
    past_len = n_pages * PAGE_SIZE
    assert norm_mix.shape[0] == 2 and d == D_MODEL and t_s <= DEC_ROWS // 2
    assert t_len % M_CHUNK == 0 and t_len % NSA_TQ == 0 and t_len >= WINDOW

    xp = x_prompt.reshape(bsz * t_len, d)
    reps = DEC_ROWS // t_s
    xs = jnp.concatenate([x_sample] * reps, axis=1).reshape(dbsz * DEC_ROWS, d)
    n_s = dbsz * DEC_ROWS

    u_all = peer_u.astype(BF16)
    vt_all = jnp.swapaxes(peer_v, 1, 2).astype(BF16)

    def peer(x, layer, tm, out_norm=None):
        keys = peer_sub_keys[layer].astype(BF16)
        return peer_ffn_residual(x, norm_ffn[layer], peer_w_q[layer].astype(BF16),
                                 keys[:, 0], keys[:, 1], u_all, vt_all, layer, tm, PEER_TE, out_norm)

    w_in = jnp.pad(mlstm_w_in[0], ((0, 0), (0, M_PROJ_PAD - M_PROJ))).astype(BF16)
    w_out = mlstm_w_out[0].astype(BF16)
    gate_bias = jnp.pad(jnp.concatenate([mlstm_b_i[0], mlstm_b_f[0]]),
                        (0, LANE - 2 * M_HEADS)).reshape(1, LANE)
    proj_p = rms_matmul(xp, norm_mix[0], w_in, PROMPT_TM).reshape(bsz, t_len, M_PROJ_PAD)
    proj_s = rms_matmul(xs, norm_mix[0], w_in, n_s).reshape(dbsz, DEC_ROWS, M_PROJ_PAD)
    hid_p, c_p, n_p, m_p = mlstm_scan(
        proj_p, jnp.zeros((bsz, M_HEADS, M_DK, M_DV), F32), jnp.zeros((bsz, M_HEADS, M_DK), F32),
        jnp.zeros((bsz, M_HEADS), F32), gate_bias, mlstm_head_norm[0], M_CHUNK, M_CHUNK - 1)
    hid_s, c_s, n_st, m_st = mlstm_scan(
        proj_s, state_C[0], state_n[0], state_m[0], gate_bias, mlstm_head_norm[0],
        DEC_ROWS, t_s - 1)
    xp = matmul_residual(hid_p.reshape(bsz * t_len, d), w_out, xp, PROMPT_TM)
    xs = matmul_residual(hid_s.reshape(n_s, d), w_out, xs, n_s)
    xp = peer(xp, 0, PEER_TM)
    xs = peer(xs, 0, n_s)

    w_in_a = jnp.pad(nsa_w_in[0], ((0, 0), (0, A_PROJ_PAD - A_PROJ))).astype(BF16)
    w_out_a = nsa_w_out[0].astype(BF16)
    b_gate_row = jnp.pad(nsa_b_gate[0].reshape(-1), (0, LANE - 3 * A_HEADS)).reshape(1, LANE)
    w1bd, b1t, w2bd = _compress_weights(nsa_cmp_w1[0], nsa_cmp_b1[0], nsa_cmp_w2[0])
    tabs_p = rope_tables(jnp.arange(t_len, dtype=jnp.int32))
    pos_s = past_len + jnp.arange(DEC_ROWS, dtype=jnp.int32)
    tabs_s = tuple(jnp.tile(a, (dbsz, 1)) for a in rope_tables(pos_s))

    q_p, cmp_p, sel_p, win_p, gate_p, cmp_pt, sel_pt = nsa_project(
        xp, norm_mix[1], w_in_a, tabs_p, PROMPT_TM, tokens_minor_seqs=bsz)
    q_s, cmp_s, sel_s, win_s, gate_s = nsa_project(xs, norm_mix[1], w_in_a, tabs_s, n_s)

    ck, cv = compress_prompt(cmp_p, bsz, w1bd, b1t, w2bd)
    o_p = nsa_prompt_attention(q_p, gate_p, ck, cv, sel_p, win_p, b_gate_row, bsz, t_len, NSA_TQ)
    xp = matmul_residual(o_p, w_out_a, xp, PROMPT_TM)

    n_blk = -(-(past_len + t_s) // SEL_BLK)
    n_blk_pad = -(-n_blk // LANE) * LANE
    q8 = q_s.reshape(dbsz, DEC_ROWS, A_QW)
    o_cmp, msel = nsa_decode_compress(tokens_minor(cache_cmp_kv[0]), page_table, w1bd, b1t, w2bd,
                                      q8, past_len, n_blk_pad)
    o_s = nsa_decode_attention(
        tokens_minor(cache_sel_kv[0]), page_table, q8, msel, o_cmp,
        sel_s.reshape(dbsz, DEC_ROWS, ROW_W), tokens_minor(state_win_kv[0]),
        win_s.reshape(dbsz, DEC_ROWS, ROW_W), gate_s.reshape(dbsz, DEC_ROWS, LANE), b_gate_row,
        past_len, t_s)
    xs = matmul_residual(o_s.reshape(n_s, d), w_out_a, xs, n_s)
    y_p = peer(xp, 1, PEER_TM, norm_final).reshape(bsz, t_len, d)
    y_s = peer(xs, 1, n_s, norm_final).reshape(dbsz, DEC_ROWS, d)[:, :t_s]

    row_shape = (2, A_KV, A_DH)
    kv_p = lambda a_t: jnp.transpose(a_t.reshape((bsz,) + row_shape + (t_len,)), (0, 4, 1, 2, 3))[None]
    kv_s = lambda a: a.reshape((dbsz, DEC_ROWS) + row_shape)[None, :, :t_s]
    win_rows = min(WINDOW, t_len)
    win_buf_p = win_p.reshape((bsz, t_len) + row_shape)[None, :, t_len - win_rows:]
    win_new_s = win_s.reshape((dbsz, DEC_ROWS) + row_shape)[:, :t_s]
    win_buf_s = jnp.concatenate([state_win_kv[0], win_new_s], axis=1)[None, :, t_s:]
    return (y_p, y_s,
            kv_p(cmp_pt), kv_p(sel_pt), win_buf_p,
            c_p[None], n_p[None], m_p[None],
            kv_s(cmp_s), kv_s(sel_s), win_buf_s,
            c_s[None], n_st[None], m_st[None])
```
